```python
import math
import jax
import jax.numpy as jnp
from jax import lax
import numpy as np

D_MODEL = 2048
BATCH = 4
SEQ = 2048
DEPTH = 2
DEC_BATCH = 128
DEC_SEQ = 4
PAST_LEN = 2048
PAGE_SIZE = 128

N_MIXERS = 2
N_RET_LAYERS = (DEPTH + 1) // 2
N_ATT_LAYERS = DEPTH // 2
RET_HEADS = 8
RET_DK = D_MODEL // RET_HEADS
RET_DV = 2 * D_MODEL // RET_HEADS
RET_CHUNK = 128
ROPE_BASE = 10000.0
ATT_HEADS = 16
ATT_HEAD_DIM = D_MODEL // ATT_HEADS
ATT_KV_HEADS = 4
IDX_HEADS = 16
IDX_DIM = 128
TOPK_MAX = 256
ATT_QBLOCK = 128
REL_BUCKETS = 32
REL_MAX_DIST = 128
FFN_RAW = -(-8 * D_MODEL // 3)
FFN_HIDDEN = -(-FFN_RAW // 256) * 256
PLE_DIM = 256
EPS = 1e-6

RET_Q_W = RET_HEADS * RET_DK
RET_V_W = RET_HEADS * RET_DV
RET_IN_WIDTH = 2 * RET_Q_W + 2 * RET_V_W
ATT_Q_W = ATT_HEADS * ATT_HEAD_DIM
ATT_KV_W = ATT_KV_HEADS * ATT_HEAD_DIM
IDX_Q_W = IDX_HEADS * IDX_DIM
ATT_IN_WIDTH = ATT_Q_W + 2 * ATT_KV_W + IDX_Q_W + IDX_DIM + IDX_HEADS

kernel_name = "retention_dsa_hybrid_step"


def rmsnorm(x, g):
    xf = x.astype(jnp.float32)
    y = xf * lax.rsqrt(jnp.mean(xf * xf, axis=-1, keepdims=True) + EPS)
    return (y * g.astype(jnp.float32)).astype(x.dtype)


def rotate(x, pos):
    half = x.shape[-1] // 2
    freqs = ROPE_BASE ** (-jnp.arange(half, dtype=jnp.float32) / half)
    ang = pos.astype(jnp.float32)[:, None] * freqs[None, :]
    cos = jnp.cos(ang)[None, :, None, :]
    sin = jnp.sin(ang)[None, :, None, :]
    xf = x.astype(jnp.float32)
    x1, x2 = xf[..., :half], xf[..., half:]
    return jnp.concatenate([x1 * cos - x2 * sin, x1 * sin + x2 * cos], axis=-1)


def ret_log_gamma():
    return jnp.log1p(-jnp.exp2(-5.0 - jnp.arange(RET_HEADS, dtype=jnp.float32)))


def retention_chunk(q, k, v, s, lg):
    c = q.shape[2]
    n = jnp.arange(c, dtype=jnp.float32)
    diff = n[:, None] - n[None, :]
    decay = jnp.where(diff >= 0, jnp.exp(lg[:, None, None] * jnp.maximum(diff, 0.0)), 0.0)
    scores = jnp.einsum('bhnd,bhmd->bhnm', q, k) * decay[None]
    o = jnp.einsum('bhnm,bhme->bhne', scores, v)
    o = o + jnp.einsum('bhnd,bhde->bhne', q, s) * jnp.exp(lg[:, None] * (n + 1.0))[None, :, :, None]
    k_dec = k * jnp.exp(lg[:, None] * (c - 1.0 - n))[None, :, :, None]
    s_new = jnp.exp(lg * c)[None, :, None, None] * s + jnp.einsum('bhmd,bhme->bhde', k_dec, v)
    return o, s_new


def retention_project(h, w_in, pos):
    b, s, _ = h.shape
    z = h @ w_in
    q = z[..., :RET_Q_W].reshape(b, s, RET_HEADS, RET_DK)
    k = z[..., RET_Q_W:2 * RET_Q_W].reshape(b, s, RET_HEADS, RET_DK)
    v = z[..., 2 * RET_Q_W:2 * RET_Q_W + RET_V_W].reshape(b, s, RET_HEADS, RET_DV)
    g = z[..., 2 * RET_Q_W + RET_V_W:]
    q = rotate(q, pos).transpose(0, 2, 1, 3)
    k = (rotate(k, pos) * (RET_DK ** -0.5)).transpose(0, 2, 1, 3)
    v = v.astype(jnp.float32).transpose(0, 2, 1, 3)
    return q, k, v, g


def retention_output(o, g, gn_gain, w_out, dtype):
    mu = jnp.mean(o, axis=-1, keepdims=True)
    var = jnp.mean(jnp.square(o - mu), axis=-1, keepdims=True)
    o = (o - mu) * lax.rsqrt(var + EPS)
    b, hh, s, dv = o.shape
    o = o.transpose(0, 2, 1, 3).reshape(b, s, hh * dv) * gn_gain.astype(jnp.float32)
    y = (jax.nn.silu(g.astype(jnp.float32)) * o).astype(dtype)
    return y @ w_out


def retention_prompt(h, w_in, gn_gain, w_out):
    b, s, _ = h.shape
    pos = jnp.arange(s, dtype=jnp.int32)
    q, k, v, g = retention_project(h, w_in, pos)
    nc = s // RET_CHUNK
    lg = ret_log_gamma()

    def to_chunks(t):
        return t.reshape(b, RET_HEADS, nc, RET_CHUNK, t.shape[-1]).transpose(2, 0, 1, 3, 4)

    def step(state, qkv):
        o_c, state = retention_chunk(qkv[0], qkv[1], qkv[2], state, lg)
        return state, o_c

    s0 = jnp.zeros((b, RET_HEADS, RET_DK, RET_DV), jnp.float32)
    s_fin, o = lax.scan(step, s0, (to_chunks(q), to_chunks(k), to_chunks(v)))
    o = o.transpose(1, 2, 0, 3, 4).reshape(b, RET_HEADS, s, RET_DV)
    return retention_output(o, g, gn_gain, w_out, h.dtype), s_fin


def retention_sample(h, state, past_len, w_in, gn_gain, w_out):
    t = h.shape[1]
    pos = past_len + jnp.arange(t, dtype=jnp.int32)
    q, k, v, g = retention_project(h, w_in, pos)
    o, s_new = retention_chunk(q, k, v, state.astype(jnp.float32), ret_log_gamma())
    return retention_output(o, g, gn_gain, w_out, h.dtype), s_new


def rel_bucket(dist):
    n = jnp.maximum(dist, 0)
    max_exact = REL_BUCKETS // 2
    nf = jnp.maximum(n, max_exact).astype(jnp.float32)
    large = max_exact + (jnp.log(nf / max_exact) / math.log(REL_MAX_DIST / max_exact)
                         * (REL_BUCKETS - max_exact)).astype(jnp.int32)
    large = jnp.minimum(large, REL_BUCKETS - 1)
    return jnp.where(n < max_exact, n, large)


def dsa_project(h, w_in):
    b, s, _ = h.shape
    z = h @ w_in
    o1 = ATT_Q_W
    o2 = o1 + ATT_KV_W
    o3 = o2 + ATT_KV_W
    o4 = o3 + IDX_Q_W
    o5 = o4 + IDX_DIM
    q = z[..., :o1].reshape(b, s, ATT_HEADS, ATT_HEAD_DIM)
    k = z[..., o1:o2].reshape(b, s, ATT_KV_HEADS, ATT_HEAD_DIM)
    v = z[..., o2:o3].reshape(b, s, ATT_KV_HEADS, ATT_HEAD_DIM)
    iq = z[..., o3:o4].reshape(b, s, IDX_HEADS, IDX_DIM)
    ik = z[..., o4:o5]
    iw = z[..., o5:]
    return q, k, v, iq, ik, iw


def dsa_select(iq, iw, ik, qpos, topk):
    sc = jnp.einsum('bqhd,bld->bqhl', iq.astype(jnp.float32), ik.astype(jnp.float32)) * (IDX_DIM ** -0.5)
    score = jnp.einsum('bqh,bqhl->bql', iw.astype(jnp.float32) * (IDX_HEADS ** -0.5), jax.nn.relu(sc))
    kpos = jnp.arange(ik.shape[1], dtype=jnp.int32)
    score = jnp.where(kpos[None, None, :] <= qpos[None, :, None], score, -jnp.inf)
    return lax.top_k(score, topk)[1]


def dsa_attend(q, k_sel, v_sel, idx, qpos, rel_table):
    b, nq, nh, d = q.shape
    g = k_sel.shape[3]
    r = nh // g
    qg = q.reshape(b, nq, g, r, d).astype(jnp.float32)
    logits = jnp.einsum('bqgrd,bqkgd->bqgrk', qg, k_sel.astype(jnp.float32)) * (d ** -0.5)
    dist = qpos[None, :, None] - idx
    bias = rel_table.astype(jnp.float32)[rel_bucket(dist)]
    bias = bias.reshape(b, nq, -1, g, r).transpose(0, 1, 3, 4, 2)
    logits = jnp.where((dist >= 0)[:, :, None, None, :], logits + bias, -jnp.inf)
    p = jax.nn.softmax(logits, axis=-1)
    out = jnp.einsum('bqgrk,bqkgd->bqgrd', p, v_sel.astype(jnp.float32))
    return out.reshape(b, nq, nh * d).astype(q.dtype)


def dsa_prompt(h, w_in, w_out, rel_table):
    b, s, _ = h.shape
    q, k, v, iq, ik, iw = dsa_project(h, w_in)
    topk = min(TOPK_MAX, s // 4)
    nb = s // ATT_QBLOCK
    gather = jax.vmap(lambda rows, ids: rows[ids])

    def blocks(t):
        return t.reshape((b, nb, ATT_QBLOCK) + t.shape[2:]).swapaxes(0, 1)

    pos = jnp.arange(s, dtype=jnp.int32).reshape(nb, ATT_QBLOCK)

    def one_block(args):
        qb, iqb, iwb, pb = args
        idx = dsa_select(iqb, iwb, ik, pb, topk)
        return dsa_attend(qb, gather(k, idx), gather(v, idx), idx, pb, rel_table)

    o = lax.map(one_block, (blocks(q), blocks(iq), blocks(iw), pos))
    o = o.swapaxes(0, 1).reshape(b, s, ATT_Q_W)
    return o @ w_out, k, v, ik


def dsa_sample(h, pool_k, pool_v, pool_ik, page_table, w_in, w_out, rel_table):
    b, t, _ = h.shape
    past = page_table.shape[1] * PAGE_SIZE
    q, k, v, iq, ik, iw = dsa_project(h, w_in)
    qpos = past + jnp.arange(t, dtype=jnp.int32)
    ik_past = pool_ik[page_table].reshape(b, past, IDX_DIM).astype(ik.dtype)
    ik_all = jnp.concatenate([ik_past, ik], axis=1)
    topk = min(TOPK_MAX, (past + t) // 4)
    idx = dsa_select(iq, iw, ik_all, qpos, topk)
    in_past = idx < past
    ic = jnp.minimum(idx, past - 1)
    phys = jnp.take_along_axis(page_table, (ic // PAGE_SIZE).reshape(b, -1), axis=1).reshape(idx.shape)
    off = ic % PAGE_SIZE
    jn = jnp.clip(idx - past, 0, t - 1)
    gather = jax.vmap(lambda rows, ids: rows[ids])

    def pick(pool, new):
        return jnp.where(in_past[..., None, None], pool[phys, off].astype(new.dtype), gather(new, jn))

    o = dsa_attend(q, pick(pool_k, k), pick(pool_v, v), idx, qpos, rel_table)
    return o @ w_out, k, v, ik


def layer_tail(h, p, g_ffn, w_ffn_in, w_ffn_out, g_ple, w_ple_gate, w_ple_proj):
    z = rmsnorm(h, g_ffn) @ w_ffn_in
    h = h + (jax.nn.silu(z[..., :FFN_HIDDEN]) * z[..., FFN_HIDDEN:]) @ w_ffn_out
    gate = jax.nn.sigmoid(rmsnorm(h, g_ple) @ w_ple_gate)
    return h + gate * (p.astype(h.dtype) @ w_ple_proj)


def setup_inputs(seed: int = 0) -> dict:
    key = jax.random.key(seed)
    ks = jax.random.split(key, 24)
    f32 = jnp.float32

    def nrm(k, shape, scale=1.0):
        return jax.random.normal(k, shape, f32) * scale

    def gain(k, shape):
        return 1.0 + 0.05 * jax.random.normal(k, shape, f32)

    n_pages = PAST_LEN // PAGE_SIZE
    n_phys = -(-5 * DEC_BATCH * n_pages // 4)
    page_table = jax.random.permutation(ks[8], n_phys)[:DEC_BATCH * n_pages].reshape(DEC_BATCH, n_pages).astype(jnp.int32)
    return {
        "x_prompt": nrm(ks[0], (BATCH, SEQ, D_MODEL)),
        "x_sample": nrm(ks[1], (DEC_BATCH, DEC_SEQ, D_MODEL)),
        "p_prompt": nrm(ks[2], (DEPTH, BATCH, SEQ, PLE_DIM)),
        "p_sample": nrm(ks[3], (DEPTH, DEC_BATCH, DEC_SEQ, PLE_DIM)),
        "state_ret": nrm(ks[4], (N_RET_LAYERS, DEC_BATCH, RET_HEADS, RET_DK, RET_DV), 0.05),
        "cache_k": nrm(ks[5], (N_ATT_LAYERS, n_phys, PAGE_SIZE, ATT_KV_HEADS, ATT_HEAD_DIM)),
        "cache_v": nrm(ks[6], (N_ATT_LAYERS, n_phys, PAGE_SIZE, ATT_KV_HEADS, ATT_HEAD_DIM)),
        "cache_idx_k": nrm(ks[7], (N_ATT_LAYERS, n_phys, PAGE_SIZE, IDX_DIM)),
        "page_table": page_table,
        "norm_mix": gain(ks[9], (DEPTH, D_MODEL)),
        "norm_ffn": gain(ks[10], (DEPTH, D_MODEL)),
        "norm_ple": gain(ks[11], (DEPTH, D_MODEL)),
        "norm_final": gain(ks[12], (D_MODEL,)),
        "w_ret_in": nrm(ks[13], (N_RET_LAYERS, D_MODEL, RET_IN_WIDTH), D_MODEL ** -0.5),
        "ret_gn_gain": gain(ks[14], (N_RET_LAYERS, RET_V_W)),
        "w_ret_out": nrm(ks[15], (N_RET_LAYERS, RET_V_W, D_MODEL), RET_V_W ** -0.5),
        "w_att_in": nrm(ks[16], (N_ATT_LAYERS, D_MODEL, ATT_IN_WIDTH), D_MODEL ** -0.5),
        "w_att_out": nrm(ks[17], (N_ATT_LAYERS, ATT_Q_W, D_MODEL), ATT_Q_W ** -0.5),
        "rel_bias": nrm(ks[18], (REL_BUCKETS, ATT_HEADS), 0.5),
        "w_ffn_in": nrm(ks[19], (DEPTH, D_MODEL, 2 * FFN_HIDDEN), D_MODEL ** -0.5),
        "w_ffn_out": nrm(ks[20], (DEPTH, FFN_HIDDEN, D_MODEL), FFN_HIDDEN ** -0.5),
        "w_ple_gate": nrm(ks[21], (DEPTH, D_MODEL, D_MODEL), D_MODEL ** -0.5),
        "w_ple_proj": nrm(ks[22], (DEPTH, PLE_DIM, D_MODEL), PLE_DIM ** -0.5),
    }


def reference(x_prompt, x_sample, p_prompt, p_sample, state_ret, cache_k, cache_v, cache_idx_k, page_table,
              norm_mix, norm_ffn, norm_ple, norm_final, w_ret_in, ret_gn_gain, w_ret_out,
              w_att_in, w_att_out, rel_bias, w_ffn_in, w_ffn_out, w_ple_gate, w_ple_proj):
    past_len = page_table.shape[1] * PAGE_SIZE
    hp, hs = x_prompt, x_sample
    ret_p, ret_s = [], []
    kp, vp, ikp, ksm, vsm, iks = [], [], [], [], [], []
    for i in range(DEPTH):
        li = i // N_MIXERS
        if i % N_MIXERS == 0:
            op, sp = retention_prompt(rmsnorm(hp, norm_mix[i]), w_ret_in[li], ret_gn_gain[li], w_ret_out[li])
            os_, ss = retention_sample(rmsnorm(hs, norm_mix[i]), state_ret[li], past_len,
                                       w_ret_in[li], ret_gn_gain[li], w_ret_out[li])
            ret_p.append(sp.astype(state_ret.dtype))
            ret_s.append(ss.astype(state_ret.dtype))
        else:
            op, k1, v1, ik1 = dsa_prompt(rmsnorm(hp, norm_mix[i]), w_att_in[li], w_att_out[li], rel_bias)
            os_, k2, v2, ik2 = dsa_sample(rmsnorm(hs, norm_mix[i]), cache_k[li], cache_v[li], cache_idx_k[li],
                                          page_table, w_att_in[li], w_att_out[li], rel_bias)
            kp.append(k1.astype(cache_k.dtype))
            vp.append(v1.astype(cache_v.dtype))
            ikp.append(ik1.astype(cache_idx_k.dtype))
            ksm.append(k2.astype(cache_k.dtype))
            vsm.append(v2.astype(cache_v.dtype))
            iks.append(ik2.astype(cache_idx_k.dtype))
        hp = layer_tail(hp + op, p_prompt[i], norm_ffn[i], w_ffn_in[i], w_ffn_out[i],
                        norm_ple[i], w_ple_gate[i], w_ple_proj[i])
        hs = layer_tail(hs + os_, p_sample[i], norm_ffn[i], w_ffn_in[i], w_ffn_out[i],
                        norm_ple[i], w_ple_gate[i], w_ple_proj[i])
    y_prompt = rmsnorm(hp, norm_final)
    y_sample = rmsnorm(hs, norm_final)
    new_state_ret_prompt = jnp.stack(ret_p)
    new_state_ret_sample = jnp.stack(ret_s)
    new_k_prompt = jnp.stack(kp)
    new_v_prompt = jnp.stack(vp)
    new_idx_k_prompt = jnp.stack(ikp)
    new_k_sample = jnp.stack(ksm)
    new_v_sample = jnp.stack(vsm)
    new_idx_k_sample = jnp.stack(iks)
    return (y_prompt, y_sample, new_state_ret_prompt, new_state_ret_sample, new_k_prompt, new_v_prompt,
            new_idx_k_prompt, new_k_sample, new_v_sample, new_idx_k_sample)
```

```python
import functools
import math

import jax
import jax.numpy as jnp
from jax import lax
from jax.experimental import pallas as pl
from jax.experimental.pallas import tpu as pltpu

F32 = jnp.float32
BF16 = jnp.bfloat16
I32 = jnp.int32

D_MODEL = 2048
BATCH = 4
SEQ = 2048
DEPTH = 2
DEC_BATCH = 128
DEC_SEQ = 4
PAST_LEN = 2048
PAGE_SIZE = 128
N_PAGES = PAST_LEN // PAGE_SIZE

RET_HEADS = 8
RET_DK = D_MODEL // RET_HEADS
RET_DV = 2 * D_MODEL // RET_HEADS
RET_CHUNK = 128
ROPE_BASE = 10000.0
ATT_HEADS = 16
ATT_HEAD_DIM = D_MODEL // ATT_HEADS
ATT_KV_HEADS = 4
ATT_REP = ATT_HEADS // ATT_KV_HEADS
IDX_HEADS = 16
IDX_DIM = 128
TOPK_MAX = 256
REL_BUCKETS = 32
REL_MAX_DIST = 128
FFN_HIDDEN = 5632
PLE_DIM = 256
EPS = 1e-6

RET_Q_W = RET_HEADS * RET_DK
RET_V_W = RET_HEADS * RET_DV
RET_IN_WIDTH = 2 * RET_Q_W + 2 * RET_V_W
ATT_Q_W = ATT_HEADS * ATT_HEAD_DIM
ATT_KV_W = ATT_KV_HEADS * ATT_HEAD_DIM
IDX_Q_W = IDX_HEADS * IDX_DIM

M_PROMPT = BATCH * SEQ
M_SAMPLE = DEC_BATCH * DEC_SEQ

LANES = 128
SUBLANES = 8
V7X_VMEM_BYTES = 64 * 1024 * 1024
MIB = 1024 * 1024

ATT_COL_Q = 0
ATT_COL_IQ = ATT_Q_W
ATT_COL_K = ATT_COL_IQ + IDX_Q_W
ATT_COL_V = ATT_COL_K + ATT_KV_W
ATT_COL_IK = ATT_COL_V + ATT_KV_W
ATT_COL_IW = ATT_COL_IK + IDX_DIM
ATT_IN_PAD = ATT_COL_IW + LANES

QBLOCK = 128
SAMPLE_KEYS = PAST_LEN + LANES
INT_MIN = -(2 ** 31)
NEG_INF_KEY = -2139095041


def _cparams(semantics, vmem_mib):
    assert vmem_mib * MIB < V7X_VMEM_BYTES
    return pltpu.CompilerParams(dimension_semantics=semantics,
                                vmem_limit_bytes=vmem_mib * MIB)


def _dot(a, b):
    return jnp.dot(a, b, preferred_element_type=F32)


def _dot_nt(a, b):
    return lax.dot_general(a, b, (((1,), (1,)), ((), ())), preferred_element_type=F32)


def _rms(x, g):
    return x * lax.rsqrt(jnp.mean(x * x, axis=-1, keepdims=True) + EPS) * g


def _rmsnorm_kernel(x_ref, g_ref, o_ref):
    o_ref[...] = _rms(x_ref[...], g_ref[...]).astype(o_ref.dtype)


def _rmsnorm(x, g, out_dtype, tm=512):
    m, d = x.shape
    return pl.pallas_call(
        _rmsnorm_kernel,
        grid=(m // tm,),
        in_specs=[pl.BlockSpec((tm, d), lambda i: (i, 0)),
                  pl.BlockSpec((1, d), lambda i: (0, 0))],
        out_specs=pl.BlockSpec((tm, d), lambda i: (i, 0)),
        out_shape=jax.ShapeDtypeStruct((m, d), out_dtype),
        compiler_params=_cparams(("parallel",), 32),
    )(x, g.reshape(1, d))


def _mm_kernel(x_ref, w_ref, o_ref):
    o_ref[...] = _dot(x_ref[...], w_ref[...]).astype(o_ref.dtype)


def _mm_res_kernel(x_ref, w_ref, r_ref, o_ref):
    o_ref[...] = r_ref[...] + _dot(x_ref[...], w_ref[...])


def _matmul(x, w, res=None, *, tm, tn, out_dtype=F32, vmem_mib=48):
    m, k = x.shape
    n = w.shape[1]
    assert m % tm == 0 and n % tn == 0
    in_specs = [pl.BlockSpec((tm, k), lambda j, i: (i, 0)),
                pl.BlockSpec((k, tn), lambda j, i: (0, j))]
    args = [x, w]
    kern = _mm_kernel
    if res is not None:
        in_specs.append(pl.BlockSpec((tm, tn), lambda j, i: (i, j)))
        args.append(res)
        kern = _mm_res_kernel
    return pl.pallas_call(
        kern,
        grid=(n // tn, m // tm),
        in_specs=in_specs,
        out_specs=pl.BlockSpec((tm, tn), lambda j, i: (i, j)),
        out_shape=jax.ShapeDtypeStruct((m, n), out_dtype),
        compiler_params=_cparams(("parallel", "parallel"), vmem_mib),
    )(*args)


def _ffn_kernel(x_ref, g_ref, w1_ref, w2_ref, wo_ref, o_ref, xn_ref):
    @pl.when(pl.program_id(1) == 0)
    def _():
        x = x_ref[...]
        xn_ref[...] = _rms(x, g_ref[...]).astype(BF16)
        o_ref[...] = x

    xn = xn_ref[...]
    a = _dot(xn, w1_ref[...])
    b = _dot(xn, w2_ref[...])
    hid = (a * jax.nn.sigmoid(a) * b).astype(BF16)
    o_ref[...] += _dot(hid, wo_ref[...])


def _ffn(h, g, w_in, w_out, tm=512, th=512):
    m, d = h.shape
    nh = FFN_HIDDEN // th
    return pl.pallas_call(
        _ffn_kernel,
        grid=(m // tm, nh),
        in_specs=[pl.BlockSpec((tm, d), lambda i, j: (i, 0)),
                  pl.BlockSpec((1, d), lambda i, j: (0, 0)),
                  pl.BlockSpec((d, th), lambda i, j: (0, j)),
                  pl.BlockSpec((d, th), lambda i, j: (0, j + nh)),
                  pl.BlockSpec((th, d), lambda i, j: (j, 0))],
        out_specs=pl.BlockSpec((tm, d), lambda i, j: (i, 0)),
        out_shape=jax.ShapeDtypeStruct((m, d), F32),
        scratch_shapes=[pltpu.VMEM((tm, d), BF16)],
        compiler_params=_cparams(("parallel", "arbitrary"), 48),
    )(h, g.reshape(1, d), w_in, w_in, w_out)


def _ple_kernel(x_ref, g_ref, wg_ref, p_ref, wp_ref, gf_ref, o_ref, *, final_norm):
    x = x_ref[...]
    xn = _rms(x, g_ref[...]).astype(BF16)
    gate = jax.nn.sigmoid(_dot(xn, wg_ref[...]))
    y = x + gate * _dot(p_ref[...].astype(BF16), wp_ref[...])
    if final_norm:
        y = _rms(y, gf_ref[...])
    o_ref[...] = y


def _ple(h, g, w_gate, p, w_proj, g_final, final_norm, tm=512):
    m, d = h.shape
    pd = p.shape[1]
    return pl.pallas_call(
        functools.partial(_ple_kernel, final_norm=final_norm),
        grid=(m // tm,),
        in_specs=[pl.BlockSpec((tm, d), lambda i: (i, 0)),
                  pl.BlockSpec((1, d), lambda i: (0, 0)),
                  pl.BlockSpec((d, d), lambda i: (0, 0)),
                  pl.BlockSpec((tm, pd), lambda i: (i, 0)),
                  pl.BlockSpec((pd, d), lambda i: (0, 0)),
                  pl.BlockSpec((1, d), lambda i: (0, 0))],
        out_specs=pl.BlockSpec((tm, d), lambda i: (i, 0)),
        out_shape=jax.ShapeDtypeStruct((m, d), F32),
        compiler_params=_cparams(("parallel",), 48),
    )(h, g.reshape(1, d), w_gate, p, w_proj, g_final.reshape(1, d))


def _group_norm_gate(o, gate, gain):
    mu = jnp.mean(o, axis=-1, keepdims=True)
    var = jnp.mean(jnp.square(o - mu), axis=-1, keepdims=True)
    on = (o - mu) * lax.rsqrt(var + EPS)
    return gate * jax.nn.sigmoid(gate) * (on * gain)


RET_CHUNKS_PER_STEP = 4
RET_ROWS = RET_CHUNKS_PER_STEP * RET_CHUNK


def _ret_prompt_kernel(lg_ref, q_ref, k_ref, v_ref, g_ref, cos_ref, sin_ref, gain_ref,
                       y_ref, s_out_ref, s_scr):
    c = pl.program_id(2)
    lg = lg_ref[pl.program_id(1)]
    half = RET_DK // 2

    @pl.when(c == 0)
    def _():
        s_scr[...] = jnp.zeros_like(s_scr)

    n_col = lax.broadcasted_iota(I32, (RET_CHUNK, 1), 0).astype(F32)
    n_row = lax.broadcasted_iota(I32, (1, RET_CHUNK), 1).astype(F32)
    diff = n_col - n_row
    decay = jnp.where(diff >= 0, jnp.exp(lg * jnp.maximum(diff, 0.0)), 0.0)
    q_dec = jnp.exp(lg * (n_col + 1.0))
    k_dec = jnp.exp(lg * (RET_CHUNK - 1.0 - n_col))
    s_dec = jnp.exp(jnp.full((1, 1), lg * RET_CHUNK, F32))
    gain = gain_ref[...]

    def rotate(x, cos, sin):
        x1, x2 = x[:, :half], x[:, half:]
        return jnp.concatenate([x1 * cos - x2 * sin, x1 * sin + x2 * cos], axis=-1)

    for i in range(RET_CHUNKS_PER_STEP):
        rows = pl.ds(i * RET_CHUNK, RET_CHUNK)
        cos, sin = cos_ref[rows, :], sin_ref[rows, :]
        qr = rotate(q_ref[rows, :], cos, sin)
        kr = rotate(k_ref[rows, :], cos, sin) * (RET_DK ** -0.5)
        v = v_ref[rows, :].astype(BF16)
        s = s_scr[...]
        qb = qr.astype(BF16)
        scores = _dot_nt(qb, kr.astype(BF16)) * decay
        o = _dot(scores.astype(BF16), v) + _dot(qb, s.astype(BF16)) * q_dec
        kd_t = (kr * k_dec).T.astype(BF16)
        s_scr[...] = s_dec * s + _dot(kd_t, v)
        y_ref[rows, :] = _group_norm_gate(o, g_ref[rows, :], gain).astype(BF16)

    @pl.when(c == pl.num_programs(2) - 1)
    def _():
        s_out_ref[0, 0] = s_scr[...]


def _ret_prompt(z, lg, cos, sin, gain):
    steps = SEQ // RET_ROWS
    qb, vb = RET_Q_W // RET_DK, (2 * RET_Q_W) // RET_DV
    row = lambda b, h, c: b * steps + c
    return pl.pallas_call(
        _ret_prompt_kernel,
        grid=(BATCH, RET_HEADS, steps),
        in_specs=[pl.BlockSpec(memory_space=pltpu.SMEM),
                  pl.BlockSpec((RET_ROWS, RET_DK), lambda b, h, c: (row(b, h, c), h)),
                  pl.BlockSpec((RET_ROWS, RET_DK), lambda b, h, c: (row(b, h, c), qb + h)),
                  pl.BlockSpec((RET_ROWS, RET_DV), lambda b, h, c: (row(b, h, c), vb + h)),
                  pl.BlockSpec((RET_ROWS, RET_DV), lambda b, h, c: (row(b, h, c), vb + RET_HEADS + h)),
                  pl.BlockSpec((RET_ROWS, RET_DK // 2), lambda b, h, c: (c, 0)),
                  pl.BlockSpec((RET_ROWS, RET_DK // 2), lambda b, h, c: (c, 0)),
                  pl.BlockSpec((1, RET_DV), lambda b, h, c: (0, h))],
        out_specs=[pl.BlockSpec((RET_ROWS, RET_DV), lambda b, h, c: (row(b, h, c), h)),
                   pl.BlockSpec((1, 1, RET_DK, RET_DV), lambda b, h, c: (b, h, 0, 0))],
        out_shape=[jax.ShapeDtypeStruct((M_PROMPT, RET_V_W), BF16),
                   jax.ShapeDtypeStruct((BATCH, RET_HEADS, RET_DK, RET_DV), F32)],
        scratch_shapes=[pltpu.VMEM((RET_DK, RET_DV), F32)],
        compiler_params=_cparams(("parallel", "parallel", "arbitrary"), 32),
    )(lg, z, z, z, z, cos, sin, gain.reshape(1, RET_V_W))


def _ret_sample_kernel(lg_ref, qt_ref, kt_ref, zs_ref, cos_ref, sin_ref, gain_ref, s_ref,
                       y_ref, s_out_ref):
    half = RET_DK // 2
    t = DEC_SEQ
    cos, sin = cos_ref[...], sin_ref[...]
    n_row = lax.broadcasted_iota(I32, (1, t), 1).astype(F32)
    row_id = lax.broadcasted_iota(I32, (t, 1), 0)

    def rotate_t(x):
        x1, x2 = x[:half], x[half:]
        return jnp.concatenate([x1 * cos - x2 * sin, x1 * sin + x2 * cos], axis=0)

    for h in range(RET_HEADS):
        lg = lg_ref[h]
        qr = rotate_t(qt_ref[0, h])
        kr = rotate_t(kt_ref[0, h]) * (RET_DK ** -0.5)
        v = zs_ref[0, :, pl.ds(2 * RET_Q_W + h * RET_DV, RET_DV)]
        gate = zs_ref[0, :, pl.ds(2 * RET_Q_W + RET_V_W + h * RET_DV, RET_DV)]
        s = s_ref[0, 0, h]
        o = jnp.zeros((t, RET_DV), F32)
        for i in range(t):
            qi = qr[:, i:i + 1]
            sc = jnp.sum(qi * kr, axis=0, keepdims=True)
            d = i - n_row
            sc = sc * jnp.where(d >= 0, jnp.exp(lg * jnp.maximum(d, 0.0)), 0.0)
            oi = jnp.sum(qi * s, axis=0, keepdims=True) * jnp.exp(
                jnp.full((1, 1), lg * (i + 1.0), F32))
            for j in range(i + 1):
                oi = oi + sc[:, j:j + 1] * v[j:j + 1, :]
            o = jnp.where(row_id == i, oi, o)
        s_new = jnp.exp(jnp.full((1, 1), lg * t, F32)) * s
        k_dec = kr * jnp.exp(lg * (t - 1.0 - n_row))
        for j in range(t):
            s_new = s_new + k_dec[:, j:j + 1] * v[j:j + 1, :]
        s_out_ref[0, 0, h] = s_new
        y_ref[0, :, pl.ds(h * RET_DV, RET_DV)] = _group_norm_gate(
            o, gate, gain_ref[:, pl.ds(h * RET_DV, RET_DV)])


def _ret_sample(zs, qt, kt, lg, cos_t, sin_t, gain, state):
    half = RET_DK // 2
    return pl.pallas_call(
        _ret_sample_kernel,
        grid=(DEC_BATCH,),
        in_specs=[pl.BlockSpec(memory_space=pltpu.SMEM),
                  pl.BlockSpec((1, RET_HEADS, RET_DK, DEC_SEQ), lambda b: (b, 0, 0, 0)),
                  pl.BlockSpec((1, RET_HEADS, RET_DK, DEC_SEQ), lambda b: (b, 0, 0, 0)),
                  pl.BlockSpec((1, DEC_SEQ, RET_IN_WIDTH), lambda b: (b, 0, 0)),
                  pl.BlockSpec((half, DEC_SEQ), lambda b: (0, 0)),
                  pl.BlockSpec((half, DEC_SEQ), lambda b: (0, 0)),
                  pl.BlockSpec((1, RET_V_W), lambda b: (0, 0)),
                  pl.BlockSpec((1, 1, RET_HEADS, RET_DK, RET_DV), lambda b: (0, b, 0, 0, 0))],
        out_specs=[pl.BlockSpec((1, DEC_SEQ, RET_V_W), lambda b: (b, 0, 0)),
                   pl.BlockSpec((1, 1, RET_HEADS, RET_DK, RET_DV), lambda b: (0, b, 0, 0, 0))],
        out_shape=[jax.ShapeDtypeStruct((DEC_BATCH, DEC_SEQ, RET_V_W), F32),
                   jax.ShapeDtypeStruct(state.shape, F32)],
        compiler_params=_cparams(("parallel",), 40),
    )(lg, qt, kt, zs, cos_t, sin_t, gain.reshape(1, RET_V_W), state)


def _rel_bucket(dist):
    n = jnp.maximum(dist, 0)
    max_exact = REL_BUCKETS // 2
    nf = jnp.maximum(n, max_exact).astype(F32)
    large = max_exact + (jnp.log(nf / max_exact) / math.log(REL_MAX_DIST / max_exact)
                         * (REL_BUCKETS - max_exact)).astype(I32)
    large = jnp.minimum(large, REL_BUCKETS - 1)
    return jnp.where(n < max_exact, n, large)


def _bias_kernel(table_ref, near_ref, samp_ref):
    h = pl.program_id(0)

    def lookup(dist):
        bucket = _rel_bucket(dist)
        acc = jnp.zeros(dist.shape, F32)
        for b in range(REL_BUCKETS):
            acc = jnp.where(bucket == b, table_ref[b, h], acc)
        return acc

    far = table_ref[REL_BUCKETS - 1, h]
    i = lax.broadcasted_iota(I32, (QBLOCK, 2 * QBLOCK), 0)
    j = lax.broadcasted_iota(I32, (QBLOCK, 2 * QBLOCK), 1)
    near_ref[0] = lookup(QBLOCK + i - j) - far
    t = lax.broadcasted_iota(I32, (SUBLANES, SAMPLE_KEYS), 0)
    s = lax.broadcasted_iota(I32, (SUBLANES, SAMPLE_KEYS), 1)
    samp_ref[0] = lookup(PAST_LEN + t - s)


def _bias_tables(rel_bias):
    return pl.pallas_call(
        _bias_kernel,
        grid=(ATT_HEADS,),
        in_specs=[pl.BlockSpec(memory_space=pltpu.SMEM)],
        out_specs=[pl.BlockSpec((1, QBLOCK, 2 * QBLOCK), lambda h: (h, 0, 0)),
                   pl.BlockSpec((1, SUBLANES, SAMPLE_KEYS), lambda h: (h, 0, 0))],
        out_shape=[jax.ShapeDtypeStruct((ATT_HEADS, QBLOCK, 2 * QBLOCK), F32),
                   jax.ShapeDtypeStruct((ATT_HEADS, SUBLANES, SAMPLE_KEYS), F32)],
        compiler_params=_cparams(("parallel",), 16),
    )(rel_bias)


def _order_key(score):
    bits = lax.bitcast_convert_type(score + 0.0, I32)
    return bits ^ ((bits >> 31) & 0x7FFFFFFF)


def _count(mask):
    return jnp.sum(jnp.where(mask, 1.0, 0.0), axis=-1, keepdims=True)


def _topk_neg_mask(score, kpos, causal, topk, key_ref):
    rows, keys = score.shape
    key_ref[...] = _order_key(score)
    kf = float(topk)

    prefix = jnp.where(_count(key_ref[...] >= 0) >= kf, 0, INT_MIN).astype(I32)

    def thr_body(i, prefix):
        cand = prefix | jnp.left_shift(1, 30 - i)
        return jnp.where(_count(key_ref[...] >= cand) >= kf, cand, prefix)

    thr = lax.fori_loop(0, 31, thr_body, prefix)

    need = kf - _count(key_ref[...] > thr)
    nbits = (keys - 1).bit_length()

    def tie_body(i, last):
        cand = last | jnp.left_shift(1, nbits - 1 - i)
        before = (key_ref[...] == thr) & (kpos < cand)
        return jnp.where(_count(before) < need, cand, last)

    last = lax.fori_loop(0, nbits, tie_body, jnp.zeros((rows, 1), I32))
    key = key_ref[...]
    keep = ((key > thr) | ((key == thr) & (kpos <= last))) & causal
    return jnp.where(keep, 0.0, -jnp.inf)


def _dsa_prompt_kernel(q_ref, iq_ref, iw_ref, k_ref, v_ref, ik_ref, near_ref, o_ref,
                       mask_scr, lg_scr, key_scr):
    j = pl.program_id(1)
    topk = min(TOPK_MAX, SEQ // 4)

    ikb = ik_ref[...].astype(BF16)
    iw = iw_ref[...] * (IDX_HEADS ** -0.5 * IDX_DIM ** -0.5)
    for h in range(IDX_HEADS):
        sc = _dot_nt(iq_ref[:, h * IDX_DIM:(h + 1) * IDX_DIM].astype(BF16), ikb)
        term = iw[:, h:h + 1] * jnp.maximum(sc, 0.0)
        if h == 0:
            lg_scr[...] = term
        else:
            lg_scr[...] += term
    qpos = j * QBLOCK + lax.broadcasted_iota(I32, (QBLOCK, 1), 0)
    kpos = lax.broadcasted_iota(I32, (1, SEQ), 1)
    causal = kpos <= qpos
    score = jnp.where(causal, lg_scr[...], -jnp.inf)
    mask_scr[...] = _topk_neg_mask(score, kpos, causal, topk, key_scr)

    scale = ATT_HEAD_DIM ** -0.5
    d = ATT_HEAD_DIM

    def group(g, carry):
        kg = k_ref[:, pl.ds(pl.multiple_of(g * d, d), d)].astype(BF16)
        vg = v_ref[:, pl.ds(pl.multiple_of(g * d, d), d)].astype(BF16)
        for r in range(ATT_REP):
            h = g * ATT_REP + r
            qh = q_ref[:, pl.ds(pl.multiple_of(h * d, d), d)].astype(BF16)
            lg_scr[...] = _dot_nt(qh, kg) * scale

            @pl.when(j == 0)
            def _():
                lg_scr[:, 0:QBLOCK] += near_ref[h, :, QBLOCK:]

            @pl.when(j > 0)
            def _():
                start = pl.multiple_of((j - 1) * QBLOCK, QBLOCK)
                lg_scr[:, pl.ds(start, 2 * QBLOCK)] += near_ref[h]

            x = lg_scr[...] + mask_scr[...]
            p = jnp.exp(x - jnp.max(x, axis=-1, keepdims=True))
            l = jnp.sum(p, axis=-1, keepdims=True)
            out = _dot(p.astype(BF16), vg) / l
            o_ref[:, pl.ds(pl.multiple_of(h * d, d), d)] = out.astype(BF16)
        return carry

    lax.fori_loop(0, ATT_KV_HEADS, group, 0)


def _dsa_prompt(z, near):
    nq = SEQ // QBLOCK
    return pl.pallas_call(
        _dsa_prompt_kernel,
        grid=(BATCH, nq),
        in_specs=[pl.BlockSpec((QBLOCK, ATT_Q_W), lambda b, j: (b * nq + j, ATT_COL_Q // ATT_Q_W)),
                  pl.BlockSpec((QBLOCK, IDX_Q_W), lambda b, j: (b * nq + j, ATT_COL_IQ // IDX_Q_W)),
                  pl.BlockSpec((QBLOCK, LANES), lambda b, j: (b * nq + j, ATT_COL_IW // LANES)),
                  pl.BlockSpec((SEQ, ATT_KV_W), lambda b, j: (b, ATT_COL_K // ATT_KV_W)),
                  pl.BlockSpec((SEQ, ATT_KV_W), lambda b, j: (b, ATT_COL_V // ATT_KV_W)),
                  pl.BlockSpec((SEQ, IDX_DIM), lambda b, j: (b, ATT_COL_IK // IDX_DIM)),
                  pl.BlockSpec((ATT_HEADS, QBLOCK, 2 * QBLOCK), lambda b, j: (0, 0, 0))],
        out_specs=pl.BlockSpec((QBLOCK, ATT_Q_W), lambda b, j: (b * nq + j, 0)),
        out_shape=jax.ShapeDtypeStruct((M_PROMPT, ATT_Q_W), BF16),
        scratch_shapes=[pltpu.VMEM((QBLOCK, SEQ), F32), pltpu.VMEM((QBLOCK, SEQ), F32),
                        pltpu.VMEM((QBLOCK, SEQ), I32)],
        compiler_params=_cparams(("parallel", "arbitrary"), 48),
    )(z, z, z, z, z, z, near)


def _dsa_sample_kernel(pt_ref, q_ref, iq_ref, iw_ref, knew_ref, vnew_ref, iknew_ref, bias_ref,
                       ck_ref, cv_ref, cik_ref, o_ref,
                       kbuf, vbuf, ikbuf, knew_scr, vnew_scr, iknew_scr, key_scr, sem):
    b = pl.program_id(0)
    nb = pl.num_programs(0)
    slot = lax.rem(b, 2)
    t = DEC_SEQ
    d = ATT_HEAD_DIM
    rows16 = ATT_REP * t
    topk = min(TOPK_MAX, (PAST_LEN + DEC_SEQ) // 4)

    def page_copies(batch, slot_):
        copies = []
        for p in range(N_PAGES):
            page = pt_ref[batch * N_PAGES + p]
            dst = pl.ds(p * PAGE_SIZE, PAGE_SIZE)
            copies.append(pltpu.make_async_copy(ck_ref.at[page], kbuf.at[slot_, dst], sem.at[0, slot_]))
            copies.append(pltpu.make_async_copy(cv_ref.at[page], vbuf.at[slot_, dst], sem.at[1, slot_]))
            copies.append(pltpu.make_async_copy(cik_ref.at[page], ikbuf.at[slot_, dst], sem.at[2, slot_]))
        return copies

    @pl.when(b == 0)
    def _():
        knew_scr[...] = jnp.zeros_like(knew_scr)
        vnew_scr[...] = jnp.zeros_like(vnew_scr)
        iknew_scr[...] = jnp.zeros_like(iknew_scr)
        for cp in page_copies(0, 0):
            cp.start()

    @pl.when(b + 1 < nb)
    def _():
        for cp in page_copies(b + 1, 1 - slot):
            cp.start()

    knew_scr[0:t, :] = knew_ref[0]
    vnew_scr[0:t, :] = vnew_ref[0]
    iknew_scr[0:t, :] = iknew_ref[0]

    for cp in page_copies(b, slot):
        cp.wait()

    iq = iq_ref[0].astype(BF16)
    sc = jnp.concatenate([_dot_nt(iq, ikbuf[slot].astype(BF16)),
                          _dot_nt(iq, iknew_scr[...].astype(BF16))], axis=-1)
    ws = (iw_ref[0] * (IDX_HEADS ** -0.5 * IDX_DIM ** -0.5)) * jnp.maximum(sc, 0.0)
    row = lax.broadcasted_iota(I32, (rows16, 1), 0)
    assert t & (t - 1) == 0
    tq = row & (t - 1)
    score = jnp.zeros((rows16, SAMPLE_KEYS), F32)
    for i in range(t):
        si = jnp.sum(ws[i * IDX_HEADS:(i + 1) * IDX_HEADS], axis=0, keepdims=True)
        score = jnp.where(tq == i, si, score)
    kpos = lax.broadcasted_iota(I32, (1, SAMPLE_KEYS), 1)
    causal = kpos <= PAST_LEN + tq
    score = jnp.where(causal, score, -jnp.inf)
    neg_mask = _topk_neg_mask(score, kpos, causal, topk, key_scr)

    scale = d ** -0.5
    for g in range(ATT_KV_HEADS):
        cols = slice(g * d, (g + 1) * d)
        qg = q_ref[0, g].astype(BF16)
        logits = jnp.concatenate([_dot_nt(qg, kbuf[slot, :, cols].astype(BF16)),
                                  _dot_nt(qg, knew_scr[:, cols].astype(BF16))], axis=-1)
        x = logits * scale + bias_ref[g] + neg_mask
        p = jnp.exp(x - jnp.max(x, axis=-1, keepdims=True))
        l = jnp.sum(p, axis=-1, keepdims=True)
        pb = p.astype(BF16)
        out = (_dot(pb[:, :PAST_LEN], vbuf[slot, :, cols].astype(BF16))
               + _dot(pb[:, PAST_LEN:], vnew_scr[:, cols].astype(BF16))) / l
        for r in range(ATT_REP):
            h = g * ATT_REP + r
            o_ref[0, :, h * d:(h + 1) * d] = out[r * t:(r + 1) * t]


def _dsa_sample(q, iq, iw, knew, vnew, iknew, bias, page_table, ck, cv, cik):
    rows16 = ATT_REP * DEC_SEQ
    whole = lambda *shape: pl.BlockSpec(shape, lambda b, pt: (0,) * len(shape))
    per_b = lambda *shape: pl.BlockSpec((1,) + shape, lambda b, pt: (b,) + (0,) * len(shape))
    grid_spec = pltpu.PrefetchScalarGridSpec(
        num_scalar_prefetch=1,
        grid=(DEC_BATCH,),
        in_specs=[per_b(ATT_KV_HEADS, rows16, ATT_HEAD_DIM),
                  per_b(DEC_SEQ * IDX_HEADS, IDX_DIM),
                  per_b(DEC_SEQ * IDX_HEADS, 1),
                  per_b(DEC_SEQ, ATT_KV_W),
                  per_b(DEC_SEQ, ATT_KV_W),
                  per_b(DEC_SEQ, IDX_DIM),
                  whole(ATT_KV_HEADS, rows16, SAMPLE_KEYS),
                  pl.BlockSpec(memory_space=pl.ANY),
                  pl.BlockSpec(memory_space=pl.ANY),
                  pl.BlockSpec(memory_space=pl.ANY)],
        out_specs=per_b(DEC_SEQ, ATT_Q_W),
        scratch_shapes=[pltpu.VMEM((2, PAST_LEN, ATT_KV_W), F32),
                        pltpu.VMEM((2, PAST_LEN, ATT_KV_W), F32),
                        pltpu.VMEM((2, PAST_LEN, IDX_DIM), F32),
                        pltpu.VMEM((LANES, ATT_KV_W), F32),
                        pltpu.VMEM((LANES, ATT_KV_W), F32),
                        pltpu.VMEM((LANES, IDX_DIM), F32),
                        pltpu.VMEM((rows16, SAMPLE_KEYS), I32),
                        pltpu.SemaphoreType.DMA((3, 2))])
    return pl.pallas_call(
        _dsa_sample_kernel,
        grid_spec=grid_spec,
        out_shape=jax.ShapeDtypeStruct((DEC_BATCH, DEC_SEQ, ATT_Q_W), F32),
        compiler_params=_cparams(("arbitrary",), 48),
    )(page_table.reshape(-1), q, iq, iw, knew, vnew, iknew, bias, ck, cv, cik)


def _rope_tables(pos):
    half = RET_DK // 2
    freqs = ROPE_BASE ** (-jnp.arange(half, dtype=F32) / half)
    ang = pos.astype(F32)[:, None] * freqs[None, :]
    return jnp.cos(ang), jnp.sin(ang)


def _permute_att_in(w):
    o1 = ATT_Q_W
    o2 = o1 + ATT_KV_W
    o3 = o2 + ATT_KV_W
    o4 = o3 + IDX_Q_W
    o5 = o4 + IDX_DIM
    parts = [w[:, :o1], w[:, o3:o4], w[:, o1:o2], w[:, o2:o3], w[:, o4:o5],
             jnp.pad(w[:, o5:], ((0, 0), (0, LANES - IDX_HEADS)))]
    return jnp.concatenate(parts, axis=1)


def _row_tile(m):
    return min(m, 1024)


def kernel(x_prompt, x_sample, p_prompt, p_sample, state_ret, cache_k, cache_v, cache_idx_k, page_table,
           norm_mix, norm_ffn, norm_ple, norm_final, w_ret_in, ret_gn_gain, w_ret_out,
           w_att_in, w_att_out, rel_bias, w_ffn_in, w_ffn_out, w_ple_gate, w_ple_proj):
    bf = lambda w: w.astype(BF16)
    hp = x_prompt.reshape(M_PROMPT, D_MODEL)
    hs = x_sample.reshape(M_SAMPLE, D_MODEL)
    pp = p_prompt.reshape(DEPTH, M_PROMPT, PLE_DIM)
    ps = p_sample.reshape(DEPTH, M_SAMPLE, PLE_DIM)

    def mixer_in(h, g, w, tn):
        return _matmul(_rmsnorm(h, g, BF16), w, tm=_row_tile(h.shape[0]), tn=tn)

    def mixer_out(y, w, h, tm):
        return _matmul(y, w, h, tm=min(tm, h.shape[0]), tn=1024)

    def tail(h, p, i, final_norm):
        h = _ffn(h, norm_ffn[i], w_ffn_in_b[i], w_ffn_out_b[i])
        return _ple(h, norm_ple[i], w_ple_gate_b[i], p, w_ple_proj_b[i], norm_final, final_norm)

    w_ffn_in_b, w_ffn_out_b = bf(w_ffn_in), bf(w_ffn_out)
    w_ple_gate_b, w_ple_proj_b = bf(w_ple_gate), bf(w_ple_proj)

    lg = jnp.log1p(-jnp.exp2(-5.0 - jnp.arange(RET_HEADS, dtype=F32)))
    cos_p, sin_p = _rope_tables(jnp.arange(SEQ, dtype=I32))
    cos_s, sin_s = _rope_tables(PAST_LEN + jnp.arange(DEC_SEQ, dtype=I32))
    w_in, w_out = bf(w_ret_in[0]), bf(w_ret_out[0])
    y, ret_state_p = _ret_prompt(mixer_in(hp, norm_mix[0], w_in, 1024), lg, cos_p, sin_p, ret_gn_gain[0])
    zs = mixer_in(hs, norm_mix[0], w_in, 1024).reshape(DEC_BATCH, DEC_SEQ, RET_IN_WIDTH)
    qk_t = zs[..., :2 * RET_Q_W].reshape(DEC_BATCH, DEC_SEQ, 2, RET_HEADS, RET_DK).transpose(2, 0, 3, 4, 1)
    y_s, ret_state_s = _ret_sample(zs, qk_t[0], qk_t[1], lg, cos_s.T, sin_s.T, ret_gn_gain[0], state_ret)
    hp = tail(mixer_out(y, w_out, hp, 512), pp[0], 0, False)
    hs = tail(mixer_out(bf(y_s.reshape(M_SAMPLE, RET_V_W)), w_out, hs, 512), ps[0], 0, False)

    near, samp = _bias_tables(rel_bias)
    w_in, w_out = bf(_permute_att_in(w_att_in[0])), bf(w_att_out[0])
    zp = mixer_in(hp, norm_mix[1], w_in, 896)
    a = _dsa_prompt(zp, near)
    zs = mixer_in(hs, norm_mix[1], w_in, 896).reshape(DEC_BATCH, DEC_SEQ, ATT_IN_PAD)
    q_s = zs[..., :ATT_Q_W].reshape(DEC_BATCH, DEC_SEQ, ATT_KV_HEADS, ATT_REP, ATT_HEAD_DIM)
    q_s = q_s.transpose(0, 2, 3, 1, 4).reshape(DEC_BATCH, ATT_KV_HEADS, ATT_REP * DEC_SEQ, ATT_HEAD_DIM)
    bias_s = samp[:, :DEC_SEQ].reshape(ATT_KV_HEADS, ATT_REP * DEC_SEQ, SAMPLE_KEYS)
    iq_s = zs[..., ATT_COL_IQ:ATT_COL_IQ + IDX_Q_W].reshape(DEC_BATCH, DEC_SEQ * IDX_HEADS, IDX_DIM)
    iw_s = zs[..., ATT_COL_IW:ATT_COL_IW + IDX_HEADS].reshape(DEC_BATCH, DEC_SEQ * IDX_HEADS, 1)
    k_s = zs[..., ATT_COL_K:ATT_COL_K + ATT_KV_W]
    v_s = zs[..., ATT_COL_V:ATT_COL_V + ATT_KV_W]
    ik_s = zs[..., ATT_COL_IK:ATT_COL_IK + IDX_DIM]
    n_phys = cache_k.shape[1]
    a_s = _dsa_sample(q_s, iq_s, iw_s, k_s, v_s, ik_s, bias_s, page_table,
                      cache_k[0].reshape(n_phys, PAGE_SIZE, ATT_KV_W),
                      cache_v[0].reshape(n_phys, PAGE_SIZE, ATT_KV_W),
                      cache_idx_k[0])
    yp = tail(mixer_out(a, w_out, hp, 1024), pp[1], 1, True)
    ys = tail(mixer_out(bf(a_s.reshape(M_SAMPLE, ATT_Q_W)), w_out, hs, 1024), ps[1], 1, True)

    kv_p = lambda col: zp[:, col:col + ATT_KV_W].reshape(1, BATCH, SEQ, ATT_KV_HEADS, ATT_HEAD_DIM)
    kv_s = lambda x: x.reshape(1, DEC_BATCH, DEC_SEQ, ATT_KV_HEADS, ATT_HEAD_DIM)
    return (yp.reshape(BATCH, SEQ, D_MODEL),
            ys.reshape(DEC_BATCH, DEC_SEQ, D_MODEL),
            ret_state_p[None],
            ret_state_s,
            kv_p(ATT_COL_K), kv_p(ATT_COL_V),
            zp[:, ATT_COL_IK:ATT_COL_IK + IDX_DIM].reshape(1, BATCH, SEQ, IDX_DIM),
            kv_s(k_s), kv_s(v_s),
            ik_s.reshape(1, DEC_BATCH, DEC_SEQ, IDX_DIM))
```

```python
import functools
import math

import jax
import jax.numpy as jnp
from jax import lax
from jax.experimental import pallas as pl
from jax.experimental.pallas import tpu as pltpu

F32 = jnp.float32
BF16 = jnp.bfloat16
I32 = jnp.int32

D_MODEL = 2048
BATCH = 4
SEQ = 2048
DEPTH = 2
DEC_BATCH = 128
DEC_SEQ = 4
PAST_LEN = 2048
PAGE_SIZE = 128
N_PAGES = PAST_LEN // PAGE_SIZE

RET_HEADS = 8
RET_DK = D_MODEL // RET_HEADS
RET_DV = 2 * D_MODEL // RET_HEADS
RET_CHUNK = 128
ROPE_BASE = 10000.0
ATT_HEADS = 16
ATT_HEAD_DIM = D_MODEL // ATT_HEADS
ATT_KV_HEADS = 4
ATT_REP = ATT_HEADS // ATT_KV_HEADS
IDX_HEADS = 16
IDX_DIM = 128
TOPK_MAX = 256
REL_BUCKETS = 32
REL_MAX_DIST = 128
FFN_HIDDEN = 5632
PLE_DIM = 256
EPS = 1e-6

RET_Q_W = RET_HEADS * RET_DK
RET_V_W = RET_HEADS * RET_DV
RET_IN_WIDTH = 2 * RET_Q_W + 2 * RET_V_W
ATT_Q_W = ATT_HEADS * ATT_HEAD_DIM
ATT_KV_W = ATT_KV_HEADS * ATT_HEAD_DIM
IDX_Q_W = IDX_HEADS * IDX_DIM

M_PROMPT = BATCH * SEQ
M_SAMPLE = DEC_BATCH * DEC_SEQ

LANES = 128
SUBLANES = 8
V7X_VMEM_BYTES = 64 * 1024 * 1024
MIB = 1024 * 1024

ATT_COL_Q = 0
ATT_COL_IQ = ATT_Q_W
ATT_COL_K = ATT_COL_IQ + IDX_Q_W
ATT_COL_V = ATT_COL_K + ATT_KV_W
ATT_COL_IK = ATT_COL_V + ATT_KV_W
ATT_COL_IW = ATT_COL_IK + IDX_DIM
ATT_IN_PAD = ATT_COL_IW + LANES

QBLOCK = 128
SAMPLE_KEYS = PAST_LEN + LANES
INT_MIN = -(2 ** 31)
NEG_INF_KEY = -2139095041


def _cparams(semantics, vmem_mib):
    assert vmem_mib * MIB < V7X_VMEM_BYTES
    return pltpu.CompilerParams(dimension_semantics=semantics,
                                vmem_limit_bytes=vmem_mib * MIB)


def _dot(a, b):
    return jnp.dot(a, b, preferred_element_type=F32)


def _dot_nt(a, b):
    return lax.dot_general(a, b, (((1,), (1,)), ((), ())), preferred_element_type=F32)


def _rms(x, g):
    return x * lax.rsqrt(jnp.mean(x * x, axis=-1, keepdims=True) + EPS) * g


def _rmsnorm_kernel(x_ref, g_ref, o_ref):
    o_ref[...] = _rms(x_ref[...], g_ref[...]).astype(o_ref.dtype)


def _rmsnorm(x, g, out_dtype, tm=512):
    m, d = x.shape
    return pl.pallas_call(
        _rmsnorm_kernel,
        grid=(m // tm,),
        in_specs=[pl.BlockSpec((tm, d), lambda i: (i, 0)),
                  pl.BlockSpec((1, d), lambda i: (0, 0))],
        out_specs=pl.BlockSpec((tm, d), lambda i: (i, 0)),
        out_shape=jax.ShapeDtypeStruct((m, d), out_dtype),
        compiler_params=_cparams(("parallel",), 32),
    )(x, g.reshape(1, d))


def _mm_kernel(x_ref, w_ref, o_ref):
    o_ref[...] = _dot(x_ref[...], w_ref[...]).astype(o_ref.dtype)


def _mm_res_kernel(x_ref, w_ref, r_ref, o_ref):
    o_ref[...] = r_ref[...] + _dot(x_ref[...], w_ref[...])


def _matmul(x, w, res=None, *, tm, tn, out_dtype=F32, vmem_mib=48):
    m, k = x.shape
    n = w.shape[1]
    assert m % tm == 0 and n % tn == 0
    in_specs = [pl.BlockSpec((tm, k), lambda j, i: (i, 0)),
                pl.BlockSpec((k, tn), lambda j, i: (0, j))]
    args = [x, w]
    kern = _mm_kernel
    if res is not None:
        in_specs.append(pl.BlockSpec((tm, tn), lambda j, i: (i, j)))
        args.append(res)
        kern = _mm_res_kernel
    return pl.pallas_call(
        kern,
        grid=(n // tn, m // tm),
        in_specs=in_specs,
        out_specs=pl.BlockSpec((tm, tn), lambda j, i: (i, j)),
        out_shape=jax.ShapeDtypeStruct((m, n), out_dtype),
        compiler_params=_cparams(("parallel", "parallel"), vmem_mib),
    )(*args)


def _ffn_kernel(x_ref, g_ref, w1_ref, w2_ref, wo_ref, o_ref, xn_ref):
    @pl.when(pl.program_id(1) == 0)
    def _():
        x = x_ref[...]
        xn_ref[...] = _rms(x, g_ref[...]).astype(BF16)
        o_ref[...] = x

    xn = xn_ref[...]
    a = _dot(xn, w1_ref[...])
    b = _dot(xn, w2_ref[...])
    hid = (a * jax.nn.sigmoid(a) * b).astype(BF16)
    o_ref[...] += _dot(hid, wo_ref[...])


def _ffn(h, g, w_in, w_out, tm=512, th=512):
    m, d = h.shape
    nh = FFN_HIDDEN // th
    return pl.pallas_call(
        _ffn_kernel,
        grid=(m // tm, nh),
        in_specs=[pl.BlockSpec((tm, d), lambda i, j: (i, 0)),
                  pl.BlockSpec((1, d), lambda i, j: (0, 0)),
                  pl.BlockSpec((d, th), lambda i, j: (0, j)),
                  pl.BlockSpec((d, th), lambda i, j: (0, j + nh)),
                  pl.BlockSpec((th, d), lambda i, j: (j, 0))],
        out_specs=pl.BlockSpec((tm, d), lambda i, j: (i, 0)),
        out_shape=jax.ShapeDtypeStruct((m, d), F32),
        scratch_shapes=[pltpu.VMEM((tm, d), BF16)],
        compiler_params=_cparams(("parallel", "arbitrary"), 48),
    )(h, g.reshape(1, d), w_in, w_in, w_out)


def _ple_kernel(x_ref, g_ref, wg_ref, p_ref, wp_ref, gf_ref, o_ref, *, final_norm):
    x = x_ref[...]
    xn = _rms(x, g_ref[...]).astype(BF16)
    gate = jax.nn.sigmoid(_dot(xn, wg_ref[...]))
    y = x + gate * _dot(p_ref[...].astype(BF16), wp_ref[...])
    if final_norm:
        y = _rms(y, gf_ref[...])
    o_ref[...] = y


def _ple(h, g, w_gate, p, w_proj, g_final, final_norm, tm=512):
    m, d = h.shape
    pd = p.shape[1]
    return pl.pallas_call(
        functools.partial(_ple_kernel, final_norm=final_norm),
        grid=(m // tm,),
        in_specs=[pl.BlockSpec((tm, d), lambda i: (i, 0)),
                  pl.BlockSpec((1, d), lambda i: (0, 0)),
                  pl.BlockSpec((d, d), lambda i: (0, 0)),
                  pl.BlockSpec((tm, pd), lambda i: (i, 0)),
                  pl.BlockSpec((pd, d), lambda i: (0, 0)),
                  pl.BlockSpec((1, d), lambda i: (0, 0))],
        out_specs=pl.BlockSpec((tm, d), lambda i: (i, 0)),
        out_shape=jax.ShapeDtypeStruct((m, d), F32),
        compiler_params=_cparams(("parallel",), 48),
    )(h, g.reshape(1, d), w_gate, p, w_proj, g_final.reshape(1, d))


def _group_norm_gate(o, gate, gain):
    mu = jnp.mean(o, axis=-1, keepdims=True)
    var = jnp.mean(jnp.square(o - mu), axis=-1, keepdims=True)
    on = (o - mu) * lax.rsqrt(var + EPS)
    return gate * jax.nn.sigmoid(gate) * (on * gain)


RET_CHUNKS_PER_STEP = 4
RET_ROWS = RET_CHUNKS_PER_STEP * RET_CHUNK


def _ret_prompt_kernel(lg_ref, q_ref, k_ref, v_ref, g_ref, cos_ref, sin_ref, gain_ref,
                       y_ref, s_out_ref, s_scr):
    c = pl.program_id(2)
    lg = lg_ref[pl.program_id(1)]
    half = RET_DK // 2

    @pl.when(c == 0)
    def _():
        s_scr[...] = jnp.zeros_like(s_scr)

    n_col = lax.broadcasted_iota(I32, (RET_CHUNK, 1), 0).astype(F32)
    n_row = lax.broadcasted_iota(I32, (1, RET_CHUNK), 1).astype(F32)
    diff = n_col - n_row
    decay = jnp.where(diff >= 0, jnp.exp(lg * jnp.maximum(diff, 0.0)), 0.0)
    q_dec = jnp.exp(lg * (n_col + 1.0))
    k_dec = jnp.exp(lg * (RET_CHUNK - 1.0 - n_col))
    s_dec = jnp.exp(jnp.full((1, 1), lg * RET_CHUNK, F32))
    gain = gain_ref[...]

    def rotate(x, cos, sin):
        x1, x2 = x[:, :half], x[:, half:]
        return jnp.concatenate([x1 * cos - x2 * sin, x1 * sin + x2 * cos], axis=-1)

    for i in range(RET_CHUNKS_PER_STEP):
        rows = pl.ds(i * RET_CHUNK, RET_CHUNK)
        cos, sin = cos_ref[rows, :], sin_ref[rows, :]
        qr = rotate(q_ref[rows, :], cos, sin)
        kr = rotate(k_ref[rows, :], cos, sin) * (RET_DK ** -0.5)
        v = v_ref[rows, :].astype(BF16)
        s = s_scr[...]
        qb = qr.astype(BF16)
        scores = _dot_nt(qb, kr.astype(BF16)) * decay
        o = _dot(scores.astype(BF16), v) + _dot(qb, s.astype(BF16)) * q_dec
        kd_t = (kr * k_dec).T.astype(BF16)
        s_scr[...] = s_dec * s + _dot(kd_t, v)
        y_ref[rows, :] = _group_norm_gate(o, g_ref[rows, :], gain).astype(BF16)

    @pl.when(c == pl.num_programs(2) - 1)
    def _():
        s_out_ref[0, 0] = s_scr[...]


def _ret_prompt(z, lg, cos, sin, gain):
    steps = SEQ // RET_ROWS
    qb, vb = RET_Q_W // RET_DK, (2 * RET_Q_W) // RET_DV
    row = lambda b, h, c: b * steps + c
    return pl.pallas_call(
        _ret_prompt_kernel,
        grid=(BATCH, RET_HEADS, steps),
        in_specs=[pl.BlockSpec(memory_space=pltpu.SMEM),
                  pl.BlockSpec((RET_ROWS, RET_DK), lambda b, h, c: (row(b, h, c), h)),
                  pl.BlockSpec((RET_ROWS, RET_DK), lambda b, h, c: (row(b, h, c), qb + h)),
                  pl.BlockSpec((RET_ROWS, RET_DV), lambda b, h, c: (row(b, h, c), vb + h)),
                  pl.BlockSpec((RET_ROWS, RET_DV), lambda b, h, c: (row(b, h, c), vb + RET_HEADS + h)),
                  pl.BlockSpec((RET_ROWS, RET_DK // 2), lambda b, h, c: (c, 0)),
                  pl.BlockSpec((RET_ROWS, RET_DK // 2), lambda b, h, c: (c, 0)),
                  pl.BlockSpec((1, RET_DV), lambda b, h, c: (0, h))],
        out_specs=[pl.BlockSpec((RET_ROWS, RET_DV), lambda b, h, c: (row(b, h, c), h)),
                   pl.BlockSpec((1, 1, RET_DK, RET_DV), lambda b, h, c: (b, h, 0, 0))],
        out_shape=[jax.ShapeDtypeStruct((M_PROMPT, RET_V_W), BF16),
                   jax.ShapeDtypeStruct((BATCH, RET_HEADS, RET_DK, RET_DV), F32)],
        scratch_shapes=[pltpu.VMEM((RET_DK, RET_DV), F32)],
        compiler_params=_cparams(("parallel", "parallel", "arbitrary"), 32),
    )(lg, z, z, z, z, cos, sin, gain.reshape(1, RET_V_W))


def _ret_sample_kernel(lg_ref, kt_ref, zs_ref, cos_ref, sin_ref, cos_t_ref, sin_t_ref, gain_ref,
                       s_ref, y_ref, s_out_ref, q_pad, kt_pad, ktd_pad, v_pad):
    half = RET_DK // 2
    t = DEC_SEQ
    cos, sin = cos_ref[...], sin_ref[...]
    cos_t, sin_t = cos_t_ref[...], sin_t_ref[...]
    n_row = lax.broadcasted_iota(I32, (1, t), 1).astype(F32)
    n_col = lax.broadcasted_iota(I32, (t, 1), 0).astype(F32)
    pad_rows = q_pad.shape[0]
    qi = lax.broadcasted_iota(I32, (pad_rows, LANES), 0)
    kj = lax.broadcasted_iota(I32, (pad_rows, LANES), 1)
    live = (qi >= kj) & (qi < t)
    diff = jnp.maximum(qi - kj, 0).astype(F32)

    q_pad[...] = jnp.zeros_like(q_pad)
    kt_pad[...] = jnp.zeros_like(kt_pad)
    ktd_pad[...] = jnp.zeros_like(ktd_pad)
    v_pad[...] = jnp.zeros_like(v_pad)

    for h in range(RET_HEADS):
        lg = lg_ref[h]
        k = kt_ref[0, h]
        k1, k2 = k[:half], k[half:]
        kt = jnp.concatenate([k1 * cos_t - k2 * sin_t, k1 * sin_t + k2 * cos_t], axis=0) * (RET_DK ** -0.5)
        kt_pad[:, 0:t] = kt
        ktd_pad[:, 0:t] = kt * jnp.exp(lg * (t - 1.0 - n_row))
        q = zs_ref[0, :, h * RET_DK:(h + 1) * RET_DK]
        q1, q2 = q[:, :half], q[:, half:]
        q_pad[0:t, :] = jnp.concatenate([q1 * cos - q2 * sin, q1 * sin + q2 * cos], axis=-1)
        v_pad[0:t, :] = zs_ref[0, :, pl.ds(2 * RET_Q_W + h * RET_DV, RET_DV)]
        gate = zs_ref[0, :, pl.ds(2 * RET_Q_W + RET_V_W + h * RET_DV, RET_DV)]
        s = s_ref[0, 0, h]

        qb = q_pad[...].astype(BF16)
        vb = v_pad[...].astype(BF16)
        scores = _dot(qb, kt_pad[...].astype(BF16)) * jnp.where(live, jnp.exp(lg * diff), 0.0)
        o = _dot(scores.astype(BF16), vb)[0:t] + _dot(qb, s.astype(BF16))[0:t] * jnp.exp(lg * (n_col + 1.0))
        s_out_ref[0, 0, h] = (jnp.exp(jnp.full((1, 1), lg * t, F32)) * s
                              + _dot(ktd_pad[...].astype(BF16), vb))
        y_ref[0, :, pl.ds(h * RET_DV, RET_DV)] = _group_norm_gate(
            o, gate, gain_ref[:, pl.ds(h * RET_DV, RET_DV)])


def _ret_sample(zs, kt, lg, cos, sin, gain, state):
    half = RET_DK // 2
    whole = lambda *shape: pl.BlockSpec(shape, lambda b: (0,) * len(shape))
    return pl.pallas_call(
        _ret_sample_kernel,
        grid=(DEC_BATCH,),
        in_specs=[pl.BlockSpec(memory_space=pltpu.SMEM),
                  pl.BlockSpec((1, RET_HEADS, RET_DK, DEC_SEQ), lambda b: (b, 0, 0, 0)),
                  pl.BlockSpec((1, DEC_SEQ, RET_IN_WIDTH), lambda b: (b, 0, 0)),
                  whole(DEC_SEQ, half), whole(DEC_SEQ, half),
                  whole(half, DEC_SEQ), whole(half, DEC_SEQ),
                  whole(1, RET_V_W),
                  pl.BlockSpec((1, 1, RET_HEADS, RET_DK, RET_DV), lambda b: (0, b, 0, 0, 0))],
        out_specs=[pl.BlockSpec((1, DEC_SEQ, RET_V_W), lambda b: (b, 0, 0)),
                   pl.BlockSpec((1, 1, RET_HEADS, RET_DK, RET_DV), lambda b: (0, b, 0, 0, 0))],
        out_shape=[jax.ShapeDtypeStruct((DEC_BATCH, DEC_SEQ, RET_V_W), F32),
                   jax.ShapeDtypeStruct(state.shape, F32)],
        scratch_shapes=[pltpu.VMEM((2 * SUBLANES, RET_DK), F32),
                        pltpu.VMEM((RET_DK, LANES), F32),
                        pltpu.VMEM((RET_DK, LANES), F32),
                        pltpu.VMEM((LANES, RET_DV), F32)],
        compiler_params=_cparams(("parallel",), 40),
    )(lg, kt, zs, cos, sin, cos.T, sin.T, gain.reshape(1, RET_V_W), state)


def _rel_bucket(dist):
    n = jnp.maximum(dist, 0)
    max_exact = REL_BUCKETS // 2
    nf = jnp.maximum(n, max_exact).astype(F32)
    large = max_exact + (jnp.log(nf / max_exact) / math.log(REL_MAX_DIST / max_exact)
                         * (REL_BUCKETS - max_exact)).astype(I32)
    large = jnp.minimum(large, REL_BUCKETS - 1)
    return jnp.where(n < max_exact, n, large)


def _bias_kernel(table_ref, near_ref, samp_ref):
    h = pl.program_id(0)

    def lookup(dist):
        bucket = _rel_bucket(dist)
        acc = jnp.zeros(dist.shape, F32)
        for b in range(REL_BUCKETS):
            acc = jnp.where(bucket == b, table_ref[b, h], acc)
        return acc

    far = table_ref[REL_BUCKETS - 1, h]
    i = lax.broadcasted_iota(I32, (QBLOCK, 2 * QBLOCK), 0)
    j = lax.broadcasted_iota(I32, (QBLOCK, 2 * QBLOCK), 1)
    near = lookup(QBLOCK + i - j) - far
    near_ref[0, :, 0:2 * QBLOCK] = near
    near_ref[0, :, 2 * QBLOCK:] = near[:, 0:QBLOCK]
    t = lax.broadcasted_iota(I32, (SUBLANES, SAMPLE_KEYS), 0)
    s = lax.broadcasted_iota(I32, (SUBLANES, SAMPLE_KEYS), 1)
    samp_ref[0] = lookup(PAST_LEN + t - s)


def _bias_tables(rel_bias):
    return pl.pallas_call(
        _bias_kernel,
        grid=(ATT_HEADS,),
        in_specs=[pl.BlockSpec(memory_space=pltpu.SMEM)],
        out_specs=[pl.BlockSpec((1, QBLOCK, 3 * QBLOCK), lambda h: (h, 0, 0)),
                   pl.BlockSpec((1, SUBLANES, SAMPLE_KEYS), lambda h: (h, 0, 0))],
        out_shape=[jax.ShapeDtypeStruct((ATT_HEADS, QBLOCK, 3 * QBLOCK), F32),
                   jax.ShapeDtypeStruct((ATT_HEADS, SUBLANES, SAMPLE_KEYS), F32)],
        compiler_params=_cparams(("parallel",), 16),
    )(rel_bias)


def _order_key(score):
    bits = lax.bitcast_convert_type(score + 0.0, I32)
    return bits ^ ((bits >> 31) & 0x7FFFFFFF)


def _count(mask):
    return jnp.sum(jnp.where(mask, 1.0, 0.0), axis=-1, keepdims=True)


def _topk_neg_mask(score, kpos, causal, topk, key_ref, mask_ref):
    rows, keys = score.shape
    key_ref[...] = _order_key(score)
    kf = float(topk)

    prefix = jnp.where(_count(key_ref[...] >= 0) >= kf, 0, INT_MIN).astype(I32)

    def thr_body(i, prefix):
        cand = prefix | jnp.left_shift(1, 30 - i)
        return jnp.where(_count(key_ref[...] >= cand) >= kf, cand, prefix)

    thr = lax.fori_loop(0, 31, thr_body, prefix)
    ge = key_ref[...] >= thr
    mask_ref[...] = jnp.where(ge & causal, 0.0, -jnp.inf)

    tied = (_count(ge) > kf) & (thr > NEG_INF_KEY)

    @pl.when(jnp.max(jnp.where(tied, 1.0, 0.0)) > 0.0)
    def _():
        need = kf - _count(key_ref[...] > thr)
        nbits = (keys - 1).bit_length()

        def tie_body(i, last):
            cand = last | jnp.left_shift(1, nbits - 1 - i)
            before = (key_ref[...] == thr) & (kpos < cand)
            return jnp.where(_count(before) < need, cand, last)

        last = lax.fori_loop(0, nbits, tie_body, jnp.zeros((rows, 1), I32))
        key = key_ref[...]
        keep = ((key > thr) | ((key == thr) & (kpos <= last))) & causal
        mask_ref[...] = jnp.where(keep, 0.0, -jnp.inf)


PROMPT_SPLITS = 8
SPLIT_QBLOCKS = SEQ // QBLOCK // PROMPT_SPLITS


def _dsa_prompt_kernel(q_ref, iq_ref, iw_ref, k_ref, v_ref, ik_ref, near_ref, prev_ref, o_ref,
                       score_scr, mask_scr, key_scr, lg_scr, *, split):
    del prev_ref
    nkeys = (split + 1) * SPLIT_QBLOCKS * QBLOCK
    j = split * SPLIT_QBLOCKS + pl.program_id(1)
    topk = min(TOPK_MAX, SEQ // 4)
    d = ATT_HEAD_DIM

    ikb = ik_ref[0:nkeys, :].astype(BF16)
    iw = iw_ref[...] * (IDX_HEADS ** -0.5 * IDX_DIM ** -0.5)
    for h in range(IDX_HEADS):
        sc = _dot_nt(iq_ref[:, h * IDX_DIM:(h + 1) * IDX_DIM].astype(BF16), ikb)
        term = iw[:, h:h + 1] * jnp.maximum(sc, 0.0)
        if h == 0:
            score_scr[...] = term
        else:
            score_scr[...] += term
    qpos = j * QBLOCK + lax.broadcasted_iota(I32, (QBLOCK, 1), 0)
    kpos = lax.broadcasted_iota(I32, (1, nkeys), 1)
    causal = kpos <= qpos
    score = jnp.where(causal, score_scr[...], -jnp.inf)
    _topk_neg_mask(score, kpos, causal, topk, key_scr, mask_scr)

    scale = d ** -0.5
    win_start = pl.multiple_of(jnp.maximum(j - 1, 0) * QBLOCK, QBLOCK)
    near_start = pl.multiple_of(jnp.where(j == 0, QBLOCK, 0), QBLOCK)

    def group(g, carry):
        lanes = lambda i: pl.ds(pl.multiple_of(i * d, d), d)
        kg = k_ref[0:nkeys, lanes(g)].astype(BF16)
        vg = v_ref[0:nkeys, lanes(g)].astype(BF16)
        for r in range(ATT_REP):
            h = g * ATT_REP + r
            rows = slice(r * QBLOCK, (r + 1) * QBLOCK)
            lg_scr[rows, :] = _dot_nt((q_ref[:, lanes(h)] * scale).astype(BF16), kg)

            lg_scr[rows, pl.ds(win_start, 2 * QBLOCK)] += near_ref[h, :, pl.ds(near_start, 2 * QBLOCK)]

            x = lg_scr[rows, :] + mask_scr[...]
            p = jnp.exp(x - jnp.max(x, axis=-1, keepdims=True))
            l = jnp.sum(p, axis=-1, keepdims=True)
            o_ref[:, lanes(h)] = (_dot(p.astype(BF16), vg) / l).astype(BF16)
        return carry

    lax.fori_loop(0, ATT_KV_HEADS, group, 0)


def _dsa_prompt(z, near):
    nq = SEQ // QBLOCK
    out = jnp.zeros((M_PROMPT, ATT_Q_W), BF16)
    for split in range(PROMPT_SPLITS):
        nkeys = (split + 1) * SPLIT_QBLOCKS * QBLOCK
        qrow = lambda b, j, s=split: b * nq + s * SPLIT_QBLOCKS + j
        out = pl.pallas_call(
            functools.partial(_dsa_prompt_kernel, split=split),
            grid=(BATCH, SPLIT_QBLOCKS),
            in_specs=[pl.BlockSpec((QBLOCK, ATT_Q_W), lambda b, j: (qrow(b, j), ATT_COL_Q // ATT_Q_W)),
                      pl.BlockSpec((QBLOCK, IDX_Q_W), lambda b, j: (qrow(b, j), ATT_COL_IQ // IDX_Q_W)),
                      pl.BlockSpec((QBLOCK, LANES), lambda b, j: (qrow(b, j), ATT_COL_IW // LANES)),
                      pl.BlockSpec((SEQ, ATT_KV_W), lambda b, j: (b, ATT_COL_K // ATT_KV_W)),
                      pl.BlockSpec((SEQ, ATT_KV_W), lambda b, j: (b, ATT_COL_V // ATT_KV_W)),
                      pl.BlockSpec((SEQ, IDX_DIM), lambda b, j: (b, ATT_COL_IK // IDX_DIM)),
                      pl.BlockSpec((ATT_HEADS, QBLOCK, 3 * QBLOCK), lambda b, j: (0, 0, 0)),
                      pl.BlockSpec(memory_space=pl.ANY)],
            out_specs=pl.BlockSpec((QBLOCK, ATT_Q_W), lambda b, j: (qrow(b, j), 0)),
            out_shape=jax.ShapeDtypeStruct((M_PROMPT, ATT_Q_W), BF16),
            input_output_aliases={7: 0},
            scratch_shapes=[pltpu.VMEM((QBLOCK, nkeys), F32), pltpu.VMEM((QBLOCK, nkeys), F32),
                            pltpu.VMEM((QBLOCK, nkeys), I32),
                            pltpu.VMEM((ATT_REP * QBLOCK, nkeys), F32)],
            compiler_params=_cparams(("parallel", "arbitrary"), 56),
        )(z, z, z, z, z, z, near, out)
    return out


def _dsa_sample_kernel(pt_ref, q_ref, iq_ref, iw_ref, knew_ref, vnew_ref, iknew_ref, bias_ref,
                       ck_ref, cv_ref, cik_ref, o_ref,
                       kbuf, vbuf, ikbuf, knew_scr, vnew_scr, iknew_scr, key_scr, mask_scr, sem):
    b = pl.program_id(0)
    nb = pl.num_programs(0)
    slot = lax.rem(b, 2)
    t = DEC_SEQ
    d = ATT_HEAD_DIM
    rows16 = ATT_REP * t
    topk = min(TOPK_MAX, (PAST_LEN + DEC_SEQ) // 4)

    def page_copies(batch, slot_):
        copies = []
        for p in range(N_PAGES):
            page = pt_ref[batch * N_PAGES + p]
            dst = pl.ds(p * PAGE_SIZE, PAGE_SIZE)
            kv_rows = PAGE_SIZE * ATT_KV_HEADS
            kv_src = pl.ds(pl.multiple_of(page * kv_rows, kv_rows), kv_rows)
            kv_dst = pl.ds(p * kv_rows, kv_rows)
            copies.append(pltpu.make_async_copy(ck_ref.at[kv_src], kbuf.at[slot_, kv_dst], sem.at[0, slot_]))
            copies.append(pltpu.make_async_copy(cv_ref.at[kv_src], vbuf.at[slot_, kv_dst], sem.at[1, slot_]))
            copies.append(pltpu.make_async_copy(cik_ref.at[page], ikbuf.at[slot_, dst], sem.at[2, slot_]))
        return copies

    def kv_head(buf, g):
        return buf[slot, pl.ds(g, PAST_LEN, stride=ATT_KV_HEADS), :]

    @pl.when(b == 0)
    def _():
        knew_scr[...] = jnp.zeros_like(knew_scr)
        vnew_scr[...] = jnp.zeros_like(vnew_scr)
        iknew_scr[...] = jnp.zeros_like(iknew_scr)
        for cp in page_copies(0, 0):
            cp.start()

    @pl.when(b + 1 < nb)
    def _():
        for cp in page_copies(b + 1, 1 - slot):
            cp.start()

    knew_scr[0:t, :] = knew_ref[0]
    vnew_scr[0:t, :] = vnew_ref[0]
    iknew_scr[0:t, :] = iknew_ref[0]

    for cp in page_copies(b, slot):
        cp.wait()

    iq = iq_ref[0].astype(BF16)
    sc = jnp.concatenate([_dot_nt(iq, ikbuf[slot].astype(BF16)),
                          _dot_nt(iq, iknew_scr[...].astype(BF16))], axis=-1)
    ws = (iw_ref[0] * (IDX_HEADS ** -0.5 * IDX_DIM ** -0.5)) * jnp.maximum(sc, 0.0)
    row = lax.broadcasted_iota(I32, (rows16, 1), 0)
    assert t & (t - 1) == 0
    tq = row & (t - 1)
    score = jnp.zeros((rows16, SAMPLE_KEYS), F32)
    for i in range(t):
        si = jnp.sum(ws[i * IDX_HEADS:(i + 1) * IDX_HEADS], axis=0, keepdims=True)
        score = jnp.where(tq == i, si, score)
    kpos = lax.broadcasted_iota(I32, (1, SAMPLE_KEYS), 1)
    causal = kpos <= PAST_LEN + tq
    score = jnp.where(causal, score, -jnp.inf)
    _topk_neg_mask(score, kpos, causal, topk, key_scr, mask_scr)
    neg_mask = mask_scr[...]

    scale = d ** -0.5
    for g in range(ATT_KV_HEADS):
        cols = slice(g * d, (g + 1) * d)
        qg = q_ref[0, g].astype(BF16)
        logits = jnp.concatenate([_dot_nt(qg, kv_head(kbuf, g).astype(BF16)),
                                  _dot_nt(qg, knew_scr[:, cols].astype(BF16))], axis=-1)
        x = logits * scale + bias_ref[g] + neg_mask
        p = jnp.exp(x - jnp.max(x, axis=-1, keepdims=True))
        l = jnp.sum(p, axis=-1, keepdims=True)
        pb = p.astype(BF16)
        out = (_dot(pb[:, :PAST_LEN], kv_head(vbuf, g).astype(BF16))
               + _dot(pb[:, PAST_LEN:], vnew_scr[:, cols].astype(BF16))) / l
        for r in range(ATT_REP):
            h = g * ATT_REP + r
            o_ref[0, :, h * d:(h + 1) * d] = out[r * t:(r + 1) * t]


def _dsa_sample(q, iq, iw, knew, vnew, iknew, bias, page_table, ck, cv, cik):
    rows16 = ATT_REP * DEC_SEQ
    whole = lambda *shape: pl.BlockSpec(shape, lambda b, pt: (0,) * len(shape))
    per_b = lambda *shape: pl.BlockSpec((1,) + shape, lambda b, pt: (b,) + (0,) * len(shape))
    grid_spec = pltpu.PrefetchScalarGridSpec(
        num_scalar_prefetch=1,
        grid=(DEC_BATCH,),
        in_specs=[per_b(ATT_KV_HEADS, rows16, ATT_HEAD_DIM),
                  per_b(DEC_SEQ * IDX_HEADS, IDX_DIM),
                  per_b(DEC_SEQ * IDX_HEADS, 1),
                  per_b(DEC_SEQ, ATT_KV_W),
                  per_b(DEC_SEQ, ATT_KV_W),
                  per_b(DEC_SEQ, IDX_DIM),
                  whole(ATT_KV_HEADS, rows16, SAMPLE_KEYS),
                  pl.BlockSpec(memory_space=pl.ANY),
                  pl.BlockSpec(memory_space=pl.ANY),
                  pl.BlockSpec(memory_space=pl.ANY)],
        out_specs=per_b(DEC_SEQ, ATT_Q_W),
        scratch_shapes=[pltpu.VMEM((2, PAST_LEN * ATT_KV_HEADS, ATT_HEAD_DIM), F32),
                        pltpu.VMEM((2, PAST_LEN * ATT_KV_HEADS, ATT_HEAD_DIM), F32),
                        pltpu.VMEM((2, PAST_LEN, IDX_DIM), F32),
                        pltpu.VMEM((LANES, ATT_KV_W), F32),
                        pltpu.VMEM((LANES, ATT_KV_W), F32),
                        pltpu.VMEM((LANES, IDX_DIM), F32),
                        pltpu.VMEM((rows16, SAMPLE_KEYS), I32),
                        pltpu.VMEM((rows16, SAMPLE_KEYS), F32),
                        pltpu.SemaphoreType.DMA((3, 2))])
    return pl.pallas_call(
        _dsa_sample_kernel,
        grid_spec=grid_spec,
        out_shape=jax.ShapeDtypeStruct((DEC_BATCH, DEC_SEQ, ATT_Q_W), F32),
        compiler_params=_cparams(("arbitrary",), 48),
    )(page_table.reshape(-1), q, iq, iw, knew, vnew, iknew, bias, ck, cv, cik)


def _rope_tables(pos):
    half = RET_DK // 2
    freqs = ROPE_BASE ** (-jnp.arange(half, dtype=F32) / half)
    ang = pos.astype(F32)[:, None] * freqs[None, :]
    return jnp.cos(ang), jnp.sin(ang)


def _permute_att_in(w):
    o1 = ATT_Q_W
    o2 = o1 + ATT_KV_W
    o3 = o2 + ATT_KV_W
    o4 = o3 + IDX_Q_W
    o5 = o4 + IDX_DIM
    parts = [w[:, :o1], w[:, o3:o4], w[:, o1:o2], w[:, o2:o3], w[:, o4:o5],
             jnp.pad(w[:, o5:], ((0, 0), (0, LANES - IDX_HEADS)))]
    return jnp.concatenate(parts, axis=1)


def _row_tile(m):
    return min(m, 1024)


def kernel(x_prompt, x_sample, p_prompt, p_sample, state_ret, cache_k, cache_v, cache_idx_k, page_table,
           norm_mix, norm_ffn, norm_ple, norm_final, w_ret_in, ret_gn_gain, w_ret_out,
           w_att_in, w_att_out, rel_bias, w_ffn_in, w_ffn_out, w_ple_gate, w_ple_proj):
    bf = lambda w: w.astype(BF16)
    hp = x_prompt.reshape(M_PROMPT, D_MODEL)
    hs = x_sample.reshape(M_SAMPLE, D_MODEL)
    pp = p_prompt.reshape(DEPTH, M_PROMPT, PLE_DIM)
    ps = p_sample.reshape(DEPTH, M_SAMPLE, PLE_DIM)

    def mixer_in(h, g, w, tn):
        return _matmul(_rmsnorm(h, g, BF16), w, tm=_row_tile(h.shape[0]), tn=tn)

    def mixer_out(y, w, h, tm):
        return _matmul(y, w, h, tm=min(tm, h.shape[0]), tn=1024)

    def tail(h, p, i, final_norm):
        h = _ffn(h, norm_ffn[i], w_ffn_in_b[i], w_ffn_out_b[i])
        return _ple(h, norm_ple[i], w_ple_gate_b[i], p, w_ple_proj_b[i], norm_final, final_norm)

    w_ffn_in_b, w_ffn_out_b = bf(w_ffn_in), bf(w_ffn_out)
    w_ple_gate_b, w_ple_proj_b = bf(w_ple_gate), bf(w_ple_proj)

    lg = jnp.log1p(-jnp.exp2(-5.0 - jnp.arange(RET_HEADS, dtype=F32)))
    cos_p, sin_p = _rope_tables(jnp.arange(SEQ, dtype=I32))
    cos_s, sin_s = _rope_tables(PAST_LEN + jnp.arange(DEC_SEQ, dtype=I32))
    w_in, w_out = bf(w_ret_in[0]), bf(w_ret_out[0])
    y, ret_state_p = _ret_prompt(mixer_in(hp, norm_mix[0], w_in, 1024), lg, cos_p, sin_p, ret_gn_gain[0])
    zs = mixer_in(hs, norm_mix[0], w_in, 1024).reshape(DEC_BATCH, DEC_SEQ, RET_IN_WIDTH)
    k_t = zs[..., RET_Q_W:2 * RET_Q_W].reshape(DEC_BATCH, DEC_SEQ, RET_HEADS, RET_DK).transpose(0, 2, 3, 1)
    y_s, ret_state_s = _ret_sample(zs, k_t, lg, cos_s, sin_s, ret_gn_gain[0], state_ret)
    hp = tail(mixer_out(y, w_out, hp, 512), pp[0], 0, False)
    hs = tail(mixer_out(bf(y_s.reshape(M_SAMPLE, RET_V_W)), w_out, hs, 512), ps[0], 0, False)

    near, samp = _bias_tables(rel_bias)
    w_in, w_out = bf(_permute_att_in(w_att_in[0])), bf(w_att_out[0])
    zp = mixer_in(hp, norm_mix[1], w_in, 896)
    a = _dsa_prompt(zp, near)
    zs = mixer_in(hs, norm_mix[1], w_in, 896).reshape(DEC_BATCH, DEC_SEQ, ATT_IN_PAD)
    q_s = zs[..., :ATT_Q_W].reshape(DEC_BATCH, DEC_SEQ, ATT_KV_HEADS, ATT_REP, ATT_HEAD_DIM)
    q_s = q_s.transpose(0, 2, 3, 1, 4).reshape(DEC_BATCH, ATT_KV_HEADS, ATT_REP * DEC_SEQ, ATT_HEAD_DIM)
    bias_s = samp[:, :DEC_SEQ].reshape(ATT_KV_HEADS, ATT_REP * DEC_SEQ, SAMPLE_KEYS)
    iq_s = zs[..., ATT_COL_IQ:ATT_COL_IQ + IDX_Q_W].reshape(DEC_BATCH, DEC_SEQ * IDX_HEADS, IDX_DIM)
    iw_s = zs[..., ATT_COL_IW:ATT_COL_IW + IDX_HEADS].reshape(DEC_BATCH, DEC_SEQ * IDX_HEADS, 1)
    k_s = zs[..., ATT_COL_K:ATT_COL_K + ATT_KV_W]
    v_s = zs[..., ATT_COL_V:ATT_COL_V + ATT_KV_W]
    ik_s = zs[..., ATT_COL_IK:ATT_COL_IK + IDX_DIM]
    n_phys = cache_k.shape[1]
    a_s = _dsa_sample(q_s, iq_s, iw_s, k_s, v_s, ik_s, bias_s, page_table,
                      cache_k[0].reshape(n_phys * PAGE_SIZE * ATT_KV_HEADS, ATT_HEAD_DIM),
                      cache_v[0].reshape(n_phys * PAGE_SIZE * ATT_KV_HEADS, ATT_HEAD_DIM),
                      cache_idx_k[0])
    yp = tail(mixer_out(a, w_out, hp, 1024), pp[1], 1, True)
    ys = tail(mixer_out(bf(a_s.reshape(M_SAMPLE, ATT_Q_W)), w_out, hs, 1024), ps[1], 1, True)

    kv_p = lambda col: zp[:, col:col + ATT_KV_W].reshape(1, BATCH, SEQ, ATT_KV_HEADS, ATT_HEAD_DIM)
    kv_s = lambda x: x.reshape(1, DEC_BATCH, DEC_SEQ, ATT_KV_HEADS, ATT_HEAD_DIM)
    return (yp.reshape(BATCH, SEQ, D_MODEL),
            ys.reshape(DEC_BATCH, DEC_SEQ, D_MODEL),
            ret_state_p[None],
            ret_state_s,
            kv_p(ATT_COL_K), kv_p(ATT_COL_V),
            zp[:, ATT_COL_IK:ATT_COL_IK + IDX_DIM].reshape(1, BATCH, SEQ, IDX_DIM),
            kv_s(k_s), kv_s(v_s),
            ik_s.reshape(1, DEC_BATCH, DEC_SEQ, IDX_DIM))
```

```python
import functools
import math

import jax
import jax.numpy as jnp
from jax import lax
from jax.experimental import pallas as pl
from jax.experimental.pallas import tpu as pltpu

F32 = jnp.float32
BF16 = jnp.bfloat16
I32 = jnp.int32

D_MODEL = 2048
BATCH = 4
SEQ = 2048
DEPTH = 2
DEC_BATCH = 128
DEC_SEQ = 4
PAST_LEN = 2048
PAGE_SIZE = 128
N_PAGES = PAST_LEN // PAGE_SIZE

RET_HEADS = 8
RET_DK = D_MODEL // RET_HEADS
RET_DV = 2 * D_MODEL // RET_HEADS
RET_CHUNK = 128
ROPE_BASE = 10000.0
ATT_HEADS = 16
ATT_HEAD_DIM = D_MODEL // ATT_HEADS
ATT_KV_HEADS = 4
ATT_REP = ATT_HEADS // ATT_KV_HEADS
IDX_HEADS = 16
IDX_DIM = 128
TOPK_MAX = 256
REL_BUCKETS = 32
REL_MAX_DIST = 128
FFN_HIDDEN = 5632
PLE_DIM = 256
EPS = 1e-6

RET_Q_W = RET_HEADS * RET_DK
RET_V_W = RET_HEADS * RET_DV
RET_IN_WIDTH = 2 * RET_Q_W + 2 * RET_V_W
ATT_Q_W = ATT_HEADS * ATT_HEAD_DIM
ATT_KV_W = ATT_KV_HEADS * ATT_HEAD_DIM
IDX_Q_W = IDX_HEADS * IDX_DIM

M_PROMPT = BATCH * SEQ
M_SAMPLE = DEC_BATCH * DEC_SEQ

LANES = 128
SUBLANES = 8
V7X_VMEM_BYTES = 64 * 1024 * 1024
MIB = 1024 * 1024

ATT_COL_Q = 0
ATT_COL_K = ATT_Q_W
ATT_COL_V = ATT_COL_K + ATT_KV_W
ATT_COL_IQ = ATT_COL_V + ATT_KV_W
ATT_COL_IK = ATT_COL_IQ + IDX_Q_W
ATT_COL_IW = ATT_COL_IK + IDX_DIM
ATT_IN_WIDTH = ATT_COL_IW + IDX_HEADS
ATT_IN_PAD = ATT_COL_IW + LANES
IQ_HALF_W = IDX_Q_W // 2
assert ATT_COL_IQ % IQ_HALF_W == 0

QBLOCK = 128
SAMPLE_KEYS = PAST_LEN + LANES
LOG2_E = math.log2(math.e)
INT_MIN = -(2 ** 31)
NEG_INF_KEY = -2139095041


def _cparams(semantics, vmem_mib):
    assert vmem_mib * MIB < V7X_VMEM_BYTES
    return pltpu.CompilerParams(dimension_semantics=semantics,
                                vmem_limit_bytes=vmem_mib * MIB)


def _dot(a, b):
    return jnp.dot(a, b, preferred_element_type=F32)


def _dot_nt(a, b):
    return lax.dot_general(a, b, (((1,), (1,)), ((), ())), preferred_element_type=F32)


def _rms(x, g):
    return x * lax.rsqrt(jnp.mean(x * x, axis=-1, keepdims=True) + EPS) * g


def _rmsnorm_kernel(x_ref, g_ref, o_ref):
    o_ref[...] = _rms(x_ref[...], g_ref[...]).astype(o_ref.dtype)


def _rmsnorm(x, g, out_dtype, tm=512):
    m, d = x.shape
    return pl.pallas_call(
        _rmsnorm_kernel,
        grid=(m // tm,),
        in_specs=[pl.BlockSpec((tm, d), lambda i: (i, 0)),
                  pl.BlockSpec((1, d), lambda i: (0, 0))],
        out_specs=pl.BlockSpec((tm, d), lambda i: (i, 0)),
        out_shape=jax.ShapeDtypeStruct((m, d), out_dtype),
        compiler_params=_cparams(("parallel",), 32),
    )(x, g.reshape(1, d))


def _mm_kernel(x_ref, w_ref, o_ref):
    o_ref[...] = _dot(x_ref[...], w_ref[...]).astype(o_ref.dtype)


def _mm_res_kernel(x_ref, w_ref, r_ref, o_ref):
    o_ref[...] = r_ref[...] + _dot(x_ref[...], w_ref[...])


def _matmul(x, w, res=None, *, tm, tn, out_dtype=F32, vmem_mib=48):
    m, k = x.shape
    n = w.shape[1]
    assert m % tm == 0 and n % tn == 0
    in_specs = [pl.BlockSpec((tm, k), lambda j, i: (i, 0)),
                pl.BlockSpec((k, tn), lambda j, i: (0, j))]
    args = [x, w]
    kern = _mm_kernel
    if res is not None:
        in_specs.append(pl.BlockSpec((tm, tn), lambda j, i: (i, j)))
        args.append(res)
        kern = _mm_res_kernel
    return pl.pallas_call(
        kern,
        grid=(n // tn, m // tm),
        in_specs=in_specs,
        out_specs=pl.BlockSpec((tm, tn), lambda j, i: (i, j)),
        out_shape=jax.ShapeDtypeStruct((m, n), out_dtype),
        compiler_params=_cparams(("parallel", "parallel"), vmem_mib),
    )(*args)


def _mm_cast_kernel(x_ref, w_ref, o_ref, wb_ref):
    @pl.when(pl.program_id(1) == 0)
    def _():
        wb_ref[...] = w_ref[...].astype(BF16)

    o_ref[...] = _dot(x_ref[...], wb_ref[...]).astype(o_ref.dtype)


def _mm_cast_res_kernel(x_ref, w_ref, r_ref, o_ref, wb_ref):
    @pl.when(pl.program_id(1) == 0)
    def _():
        wb_ref[...] = w_ref[...].astype(BF16)

    o_ref[...] = r_ref[...] + _dot(x_ref[...], wb_ref[...])


def _matmul_cast(x, w, res=None, *, tm, tn, vmem_mib=48):
    m, k = x.shape
    n = w.shape[1]
    assert m % tm == 0 and n % tn == 0
    in_specs = [pl.BlockSpec((tm, k), lambda j, i: (i, 0)),
                pl.BlockSpec((k, tn), lambda j, i: (0, j))]
    args = [x, w]
    kern = _mm_cast_kernel
    if res is not None:
        in_specs.append(pl.BlockSpec((tm, tn), lambda j, i: (i, j)))
        args.append(res)
        kern = _mm_cast_res_kernel
    return pl.pallas_call(
        kern,
        grid=(n // tn, m // tm),
        in_specs=in_specs,
        out_specs=[pl.BlockSpec((tm, tn), lambda j, i: (i, j)),
                   pl.BlockSpec((k, tn), lambda j, i: (0, j))],
        out_shape=[jax.ShapeDtypeStruct((m, n), F32), jax.ShapeDtypeStruct((k, n), BF16)],
        compiler_params=_cparams(("parallel", "arbitrary"), vmem_mib),
    )(*args)


def _ffn_kernel(x_ref, g_ref, w1_ref, w2_ref, wo_ref, o_ref, xn_ref):
    @pl.when(pl.program_id(1) == 0)
    def _():
        x = x_ref[...]
        xn_ref[...] = _rms(x, g_ref[...]).astype(BF16)
        o_ref[...] = x

    xn = xn_ref[...]
    a = _dot(xn, w1_ref[...])
    b = _dot(xn, w2_ref[...])
    hid = (a * jax.nn.sigmoid(a) * b).astype(BF16)
    o_ref[...] += _dot(hid, wo_ref[...])


def _ffn(h, g, w_in, w_out, tm=512, th=512):
    m, d = h.shape
    nh = FFN_HIDDEN // th
    return pl.pallas_call(
        _ffn_kernel,
        grid=(m // tm, nh),
        in_specs=[pl.BlockSpec((tm, d), lambda i, j: (i, 0)),
                  pl.BlockSpec((1, d), lambda i, j: (0, 0)),
                  pl.BlockSpec((d, th), lambda i, j: (0, j)),
                  pl.BlockSpec((d, th), lambda i, j: (0, j + nh)),
                  pl.BlockSpec((th, d), lambda i, j: (j, 0))],
        out_specs=pl.BlockSpec((tm, d), lambda i, j: (i, 0)),
        out_shape=jax.ShapeDtypeStruct((m, d), F32),
        scratch_shapes=[pltpu.VMEM((tm, d), BF16)],
        compiler_params=_cparams(("parallel", "arbitrary"), 48),
    )(h, g.reshape(1, d), w_in, w_in, w_out)


def _ple_kernel(x_ref, g_ref, wg_ref, p_ref, wp_ref, gf_ref, o_ref, *, final_norm):
    x = x_ref[...]
    xn = _rms(x, g_ref[...]).astype(BF16)
    gate = jax.nn.sigmoid(_dot(xn, wg_ref[...]))
    y = x + gate * _dot(p_ref[0].astype(BF16), wp_ref[...])
    if final_norm:
        y = _rms(y, gf_ref[...])
    o_ref[...] = y


def _ple(h, g, w_gate, p, layer, w_proj, g_final, final_norm, tm=512):
    m, d = h.shape
    pd = p.shape[2]
    return pl.pallas_call(
        functools.partial(_ple_kernel, final_norm=final_norm),
        grid=(m // tm,),
        in_specs=[pl.BlockSpec((tm, d), lambda i: (i, 0)),
                  pl.BlockSpec((1, d), lambda i: (0, 0)),
                  pl.BlockSpec((d, d), lambda i: (0, 0)),
                  pl.BlockSpec((1, tm, pd), lambda i: (layer, i, 0)),
                  pl.BlockSpec((pd, d), lambda i: (0, 0)),
                  pl.BlockSpec((1, d), lambda i: (0, 0))],
        out_specs=pl.BlockSpec((tm, d), lambda i: (i, 0)),
        out_shape=jax.ShapeDtypeStruct((m, d), F32),
        compiler_params=_cparams(("parallel",), 48),
    )(h, g.reshape(1, d), w_gate, p, w_proj, g_final.reshape(1, d))


def _group_norm_gate(o, gate, gain):
    mu = jnp.mean(o, axis=-1, keepdims=True)
    var = jnp.mean(jnp.square(o - mu), axis=-1, keepdims=True)
    on = (o - mu) * lax.rsqrt(var + EPS)
    return gate * jax.nn.sigmoid(gate) * (on * gain)


RET_CHUNKS_PER_STEP = 4
RET_ROWS = RET_CHUNKS_PER_STEP * RET_CHUNK


def _ret_prompt_kernel(lg_ref, q_ref, k_ref, v_ref, g_ref, cos_ref, sin_ref, gain_ref,
                       y_ref, s_out_ref, s_scr):
    c = pl.program_id(2)
    lg = lg_ref[pl.program_id(1)]
    half = RET_DK // 2

    @pl.when(c == 0)
    def _():
        s_scr[...] = jnp.zeros_like(s_scr)

    n_col = lax.broadcasted_iota(I32, (RET_CHUNK, 1), 0).astype(F32)
    n_row = lax.broadcasted_iota(I32, (1, RET_CHUNK), 1).astype(F32)
    diff = n_col - n_row
    decay = jnp.where(diff >= 0, jnp.exp(lg * jnp.maximum(diff, 0.0)), 0.0)
    q_dec = jnp.exp(lg * (n_col + 1.0))
    k_dec = jnp.exp(lg * (RET_CHUNK - 1.0 - n_col))
    s_dec = jnp.exp(jnp.full((1, 1), lg * RET_CHUNK, F32))
    gain = gain_ref[...]

    def rotate(x, cos, sin):
        x1, x2 = x[:, :half], x[:, half:]
        return jnp.concatenate([x1 * cos - x2 * sin, x1 * sin + x2 * cos], axis=-1)

    for i in range(RET_CHUNKS_PER_STEP):
        rows = pl.ds(i * RET_CHUNK, RET_CHUNK)
        cos, sin = cos_ref[rows, :], sin_ref[rows, :]
        qr = rotate(q_ref[rows, :], cos, sin)
        kr = rotate(k_ref[rows, :], cos, sin) * (RET_DK ** -0.5)
        v = v_ref[rows, :].astype(BF16)
        s = s_scr[...]
        qb = qr.astype(BF16)
        scores = _dot_nt(qb, kr.astype(BF16)) * decay
        o = _dot(scores.astype(BF16), v) + _dot(qb, s.astype(BF16)) * q_dec
        kd_t = (kr * k_dec).T.astype(BF16)
        s_scr[...] = s_dec * s + _dot(kd_t, v)
        y_ref[rows, :] = _group_norm_gate(o, g_ref[rows, :], gain).astype(BF16)

    @pl.when(c == pl.num_programs(2) - 1)
    def _():
        s_out_ref[0, 0] = s_scr[...]


def _ret_prompt(z, lg, cos, sin, gain):
    steps = SEQ // RET_ROWS
    qb, vb = RET_Q_W // RET_DK, (2 * RET_Q_W) // RET_DV
    row = lambda b, h, c: b * steps + c
    return pl.pallas_call(
        _ret_prompt_kernel,
        grid=(BATCH, RET_HEADS, steps),
        in_specs=[pl.BlockSpec(memory_space=pltpu.SMEM),
                  pl.BlockSpec((RET_ROWS, RET_DK), lambda b, h, c: (row(b, h, c), h)),
                  pl.BlockSpec((RET_ROWS, RET_DK), lambda b, h, c: (row(b, h, c), qb + h)),
                  pl.BlockSpec((RET_ROWS, RET_DV), lambda b, h, c: (row(b, h, c), vb + h)),
                  pl.BlockSpec((RET_ROWS, RET_DV), lambda b, h, c: (row(b, h, c), vb + RET_HEADS + h)),
                  pl.BlockSpec((RET_ROWS, RET_DK // 2), lambda b, h, c: (c, 0)),
                  pl.BlockSpec((RET_ROWS, RET_DK // 2), lambda b, h, c: (c, 0)),
                  pl.BlockSpec((1, RET_DV), lambda b, h, c: (0, h))],
        out_specs=[pl.BlockSpec((RET_ROWS, RET_DV), lambda b, h, c: (row(b, h, c), h)),
                   pl.BlockSpec((1, 1, RET_DK, RET_DV), lambda b, h, c: (b, h, 0, 0))],
        out_shape=[jax.ShapeDtypeStruct((M_PROMPT, RET_V_W), BF16),
                   jax.ShapeDtypeStruct((BATCH, RET_HEADS, RET_DK, RET_DV), F32)],
        scratch_shapes=[pltpu.VMEM((RET_DK, RET_DV), F32)],
        compiler_params=_cparams(("parallel", "parallel", "arbitrary"), 32),
    )(lg, z, z, z, z, cos, sin, gain.reshape(1, RET_V_W))


def _ret_sample_kernel(lg_ref, zs_ref, cos_ref, sin_ref, gain_ref, s_ref, y_ref, s_out_ref,
                       q_pad, k_pad, kd_pad, v_pad):
    half = RET_DK // 2
    t = DEC_SEQ
    cos, sin = cos_ref[...], sin_ref[...]
    n_col = lax.broadcasted_iota(I32, (t, 1), 0).astype(F32)
    pad_rows = q_pad.shape[0]
    qi = lax.broadcasted_iota(I32, (pad_rows, LANES), 0)
    kj = lax.broadcasted_iota(I32, (pad_rows, LANES), 1)
    live = (qi >= kj) & (qi < t)
    diff = jnp.maximum(qi - kj, 0).astype(F32)

    def rotate(x):
        x1, x2 = x[:, :half], x[:, half:]
        return jnp.concatenate([x1 * cos - x2 * sin, x1 * sin + x2 * cos], axis=-1)

    q_pad[...] = jnp.zeros_like(q_pad)
    k_pad[...] = jnp.zeros_like(k_pad)
    kd_pad[...] = jnp.zeros_like(kd_pad)
    v_pad[...] = jnp.zeros_like(v_pad)

    for h in range(RET_HEADS):
        lg = lg_ref[h]
        kr = rotate(zs_ref[0, :, RET_Q_W + h * RET_DK:RET_Q_W + (h + 1) * RET_DK]) * (RET_DK ** -0.5)
        k_pad[0:t, :] = kr
        kd_pad[0:t, :] = kr * jnp.exp(lg * (t - 1.0 - n_col))
        q_pad[0:t, :] = rotate(zs_ref[0, :, h * RET_DK:(h + 1) * RET_DK])
        v_pad[0:t, :] = zs_ref[0, :, pl.ds(2 * RET_Q_W + h * RET_DV, RET_DV)]
        gate = zs_ref[0, :, pl.ds(2 * RET_Q_W + RET_V_W + h * RET_DV, RET_DV)]
        s = s_ref[0, 0, h]

        qb = q_pad[...].astype(BF16)
        vb = v_pad[...].astype(BF16)
        scores = _dot_nt(qb, k_pad[...].astype(BF16)) * jnp.where(live, jnp.exp(lg * diff), 0.0)
        o = _dot(scores.astype(BF16), vb)[0:t] + _dot(qb, s.astype(BF16))[0:t] * jnp.exp(lg * (n_col + 1.0))
        s_out_ref[0, 0, h] = (jnp.exp(jnp.full((1, 1), lg * t, F32)) * s
                              + _dot(kd_pad[...].T.astype(BF16), vb))
        y_ref[0, :, pl.ds(h * RET_DV, RET_DV)] = _group_norm_gate(
            o, gate, gain_ref[:, pl.ds(h * RET_DV, RET_DV)])


def _ret_sample(zs, lg, cos, sin, gain, state):
    half = RET_DK // 2
    whole = lambda *shape: pl.BlockSpec(shape, lambda b: (0,) * len(shape))
    return pl.pallas_call(
        _ret_sample_kernel,
        grid=(DEC_BATCH,),
        in_specs=[pl.BlockSpec(memory_space=pltpu.SMEM),
                  pl.BlockSpec((1, DEC_SEQ, RET_IN_WIDTH), lambda b: (b, 0, 0)),
                  whole(DEC_SEQ, half), whole(DEC_SEQ, half),
                  whole(1, RET_V_W),
                  pl.BlockSpec((1, 1, RET_HEADS, RET_DK, RET_DV), lambda b: (0, b, 0, 0, 0))],
        out_specs=[pl.BlockSpec((1, DEC_SEQ, RET_V_W), lambda b: (b, 0, 0)),
                   pl.BlockSpec((1, 1, RET_HEADS, RET_DK, RET_DV), lambda b: (0, b, 0, 0, 0))],
        out_shape=[jax.ShapeDtypeStruct((DEC_BATCH, DEC_SEQ, RET_V_W), F32),
                   jax.ShapeDtypeStruct(state.shape, F32)],
        scratch_shapes=[pltpu.VMEM((2 * SUBLANES, RET_DK), F32),
                        pltpu.VMEM((LANES, RET_DK), F32),
                        pltpu.VMEM((LANES, RET_DK), F32),
                        pltpu.VMEM((LANES, RET_DV), F32)],
        compiler_params=_cparams(("parallel",), 40),
    )(lg, zs, cos, sin, gain.reshape(1, RET_V_W), state)


def _rel_bucket(dist):
    n = jnp.maximum(dist, 0)
    max_exact = REL_BUCKETS // 2
    nf = jnp.maximum(n, max_exact).astype(F32)
    large = max_exact + (jnp.log(nf / max_exact) / math.log(REL_MAX_DIST / max_exact)
                         * (REL_BUCKETS - max_exact)).astype(I32)
    large = jnp.minimum(large, REL_BUCKETS - 1)
    return jnp.where(n < max_exact, n, large)


def _bias_kernel(table_ref, near_ref, samp_ref):
    h = pl.program_id(0)

    def lookup(dist):
        bucket = _rel_bucket(dist)
        acc = jnp.zeros(dist.shape, F32)
        for b in range(REL_BUCKETS):
            acc = jnp.where(bucket == b, table_ref[b, h], acc)
        return acc

    far = table_ref[REL_BUCKETS - 1, h]
    i = lax.broadcasted_iota(I32, (QBLOCK, 2 * QBLOCK), 0)
    j = lax.broadcasted_iota(I32, (QBLOCK, 2 * QBLOCK), 1)
    near = (lookup(QBLOCK + i - j) - far) * LOG2_E
    near_ref[0, :, 0:2 * QBLOCK] = near
    near_ref[0, :, 2 * QBLOCK:] = near[:, 0:QBLOCK]
    t = lax.broadcasted_iota(I32, (SUBLANES, SAMPLE_KEYS), 0)
    s = lax.broadcasted_iota(I32, (SUBLANES, SAMPLE_KEYS), 1)
    samp_ref[0] = lookup(PAST_LEN + t - s)


def _bias_tables(rel_bias):
    return pl.pallas_call(
        _bias_kernel,
        grid=(ATT_HEADS,),
        in_specs=[pl.BlockSpec(memory_space=pltpu.SMEM)],
        out_specs=[pl.BlockSpec((1, QBLOCK, 3 * QBLOCK), lambda h: (h, 0, 0)),
                   pl.BlockSpec((1, SUBLANES, SAMPLE_KEYS), lambda h: (h, 0, 0))],
        out_shape=[jax.ShapeDtypeStruct((ATT_HEADS, QBLOCK, 3 * QBLOCK), F32),
                   jax.ShapeDtypeStruct((ATT_HEADS, SUBLANES, SAMPLE_KEYS), F32)],
        compiler_params=_cparams(("parallel",), 16),
    )(rel_bias)


def _order_key(score):
    bits = lax.bitcast_convert_type(score + 0.0, I32)
    return bits ^ ((bits >> 31) & 0x7FFFFFFF)


def _count(mask):
    return jnp.sum(jnp.where(mask, 1.0, 0.0), axis=-1, keepdims=True)


def _topk_neg_mask(score, kpos, causal, topk, key_ref, mask_ref):
    rows, keys = score.shape
    key_ref[...] = _order_key(score)
    kf = float(topk)

    prefix = jnp.where(_count(key_ref[...] >= 0) >= kf, 0, INT_MIN).astype(I32)

    def thr_body(i, prefix):
        cand = prefix | jnp.left_shift(1, 30 - i)
        return jnp.where(_count(key_ref[...] >= cand) >= kf, cand, prefix)

    thr = lax.fori_loop(0, 31, thr_body, prefix)
    ge = key_ref[...] >= thr
    mask_ref[...] = jnp.where(ge & causal, 0.0, -jnp.inf)

    tied = (_count(ge) > kf) & (thr > NEG_INF_KEY)

    @pl.when(jnp.max(jnp.where(tied, 1.0, 0.0)) > 0.0)
    def _():
        need = kf - _count(key_ref[...] > thr)
        nbits = (keys - 1).bit_length()

        def tie_body(i, last):
            cand = last | jnp.left_shift(1, nbits - 1 - i)
            before = (key_ref[...] == thr) & (kpos < cand)
            return jnp.where(_count(before) < need, cand, last)

        last = lax.fori_loop(0, nbits, tie_body, jnp.zeros((rows, 1), I32))
        key = key_ref[...]
        keep = ((key > thr) | ((key == thr) & (kpos <= last))) & causal
        mask_ref[...] = jnp.where(keep, 0.0, -jnp.inf)


PROMPT_SPLITS = 8
SPLIT_QBLOCKS = SEQ // QBLOCK // PROMPT_SPLITS


def _dsa_prompt_kernel(q_ref, iq_lo_ref, iq_hi_ref, iw_ref, k_ref, v_ref, ik_ref, near_ref, prev_ref, o_ref,
                       score_scr, mask_scr, key_scr, lg_scr, *, split):
    del prev_ref
    nkeys = (split + 1) * SPLIT_QBLOCKS * QBLOCK
    j = split * SPLIT_QBLOCKS + pl.program_id(1)
    topk = min(TOPK_MAX, SEQ // 4)
    d = ATT_HEAD_DIM

    ikb = ik_ref[0:nkeys, :].astype(BF16)
    iw = iw_ref[...] * (IDX_HEADS ** -0.5 * IDX_DIM ** -0.5)
    for h in range(IDX_HEADS):
        iq_ref, hh = (iq_lo_ref, h) if h < IDX_HEADS // 2 else (iq_hi_ref, h - IDX_HEADS // 2)
        sc = _dot_nt(iq_ref[:, hh * IDX_DIM:(hh + 1) * IDX_DIM].astype(BF16), ikb)
        term = iw[:, h:h + 1] * jnp.maximum(sc, 0.0)
        if h == 0:
            score_scr[...] = term
        else:
            score_scr[...] += term
    qpos = j * QBLOCK + lax.broadcasted_iota(I32, (QBLOCK, 1), 0)
    kpos = lax.broadcasted_iota(I32, (1, nkeys), 1)
    causal = kpos <= qpos
    score = jnp.where(causal, score_scr[...], -jnp.inf)
    _topk_neg_mask(score, kpos, causal, topk, key_scr, mask_scr)

    scale = d ** -0.5 * LOG2_E
    win_start = pl.multiple_of(jnp.maximum(j - 1, 0) * QBLOCK, QBLOCK)
    near_start = pl.multiple_of(jnp.where(j == 0, QBLOCK, 0), QBLOCK)

    def group(g, carry):
        lanes = lambda i: pl.ds(pl.multiple_of(i * d, d), d)
        kg = k_ref[0:nkeys, lanes(g)].astype(BF16)
        vg = v_ref[0:nkeys, lanes(g)].astype(BF16)
        def logits_and_max(r):
            h = g * ATT_REP + r
            rows = slice(r * QBLOCK, (r + 1) * QBLOCK)
            lg_scr[rows, :] = _dot_nt((q_ref[:, lanes(h)] * scale).astype(BF16), kg)
            lg_scr[rows, pl.ds(win_start, 2 * QBLOCK)] += near_ref[h, :, pl.ds(near_start, 2 * QBLOCK)]
            return jnp.max(lg_scr[rows, :] + mask_scr[...], axis=-1, keepdims=True)

        def weights_and_values(r, m):
            h = g * ATT_REP + r
            rows = slice(r * QBLOCK, (r + 1) * QBLOCK)
            p = jnp.exp2(lg_scr[rows, :] + mask_scr[...] - m)
            l = jnp.sum(p, axis=-1, keepdims=True)
            o_ref[:, lanes(h)] = (_dot(p.astype(BF16), vg) / l).astype(BF16)

        m = logits_and_max(0)
        for r in range(1, ATT_REP):
            m_next = logits_and_max(r)
            weights_and_values(r - 1, m)
            m = m_next
        weights_and_values(ATT_REP - 1, m)
        return carry

    lax.fori_loop(0, ATT_KV_HEADS, group, 0)


def _dsa_prompt(z, near):
    nq = SEQ // QBLOCK
    out = jnp.zeros((M_PROMPT, ATT_Q_W), BF16)
    for split in range(PROMPT_SPLITS):
        nkeys = (split + 1) * SPLIT_QBLOCKS * QBLOCK
        qrow = lambda b, j, s=split: b * nq + s * SPLIT_QBLOCKS + j
        out = pl.pallas_call(
            functools.partial(_dsa_prompt_kernel, split=split),
            grid=(BATCH, SPLIT_QBLOCKS),
            in_specs=[pl.BlockSpec((QBLOCK, ATT_Q_W), lambda b, j: (qrow(b, j), ATT_COL_Q // ATT_Q_W)),
                      pl.BlockSpec((QBLOCK, IQ_HALF_W), lambda b, j: (qrow(b, j), ATT_COL_IQ // IQ_HALF_W)),
                      pl.BlockSpec((QBLOCK, IQ_HALF_W), lambda b, j: (qrow(b, j), ATT_COL_IQ // IQ_HALF_W + 1)),
                      pl.BlockSpec((QBLOCK, LANES), lambda b, j: (qrow(b, j), ATT_COL_IW // LANES)),
                      pl.BlockSpec((SEQ, ATT_KV_W), lambda b, j: (b, ATT_COL_K // ATT_KV_W)),
                      pl.BlockSpec((SEQ, ATT_KV_W), lambda b, j: (b, ATT_COL_V // ATT_KV_W)),
                      pl.BlockSpec((SEQ, IDX_DIM), lambda b, j: (b, ATT_COL_IK // IDX_DIM)),
                      pl.BlockSpec((ATT_HEADS, QBLOCK, 3 * QBLOCK), lambda b, j: (0, 0, 0)),
                      pl.BlockSpec(memory_space=pl.ANY)],
            out_specs=pl.BlockSpec((QBLOCK, ATT_Q_W), lambda b, j: (qrow(b, j), 0)),
            out_shape=jax.ShapeDtypeStruct((M_PROMPT, ATT_Q_W), BF16),
            input_output_aliases={8: 0},
            scratch_shapes=[pltpu.VMEM((QBLOCK, nkeys), F32), pltpu.VMEM((QBLOCK, nkeys), F32),
                            pltpu.VMEM((QBLOCK, nkeys), I32),
                            pltpu.VMEM((ATT_REP * QBLOCK, nkeys), F32)],
            compiler_params=_cparams(("parallel", "arbitrary"), 56),
        )(z, z, z, z, z, z, z, near, out)
    return out


SELECT_BATCHES = 8


def _sample_select_kernel(pt_ref, iq_ref, iw_ref, iknew_ref, cik_ref, mask_ref,
                          ikbuf, iknew_scr, score_scr, key_scr, mask_scr, sem):
    step = pl.program_id(0)
    nsteps = pl.num_programs(0)
    slot = lax.rem(step, 2)
    t = DEC_SEQ
    topk = min(TOPK_MAX, (PAST_LEN + DEC_SEQ) // 4)

    def page_copies(step_, slot_):
        copies = []
        for bb in range(SELECT_BATCHES):
            for p in range(N_PAGES):
                page = pt_ref[(step_ * SELECT_BATCHES + bb) * N_PAGES + p]
                dst = pl.ds(p * PAGE_SIZE, PAGE_SIZE)
                copies.append(pltpu.make_async_copy(cik_ref.at[page], ikbuf.at[slot_, bb, dst], sem.at[slot_]))
        return copies

    @pl.when(step == 0)
    def _():
        iknew_scr[...] = jnp.zeros_like(iknew_scr)
        for cp in page_copies(0, 0):
            cp.start()

    @pl.when(step + 1 < nsteps)
    def _():
        for cp in page_copies(step + 1, 1 - slot):
            cp.start()

    for cp in page_copies(step, slot):
        cp.wait()

    for bb in range(SELECT_BATCHES):
        iknew_scr[bb, 0:t, :] = iknew_ref[bb]
        iq = iq_ref[bb].astype(BF16)
        sc = jnp.concatenate([_dot_nt(iq, ikbuf[slot, bb].astype(BF16)),
                              _dot_nt(iq, iknew_scr[bb].astype(BF16))], axis=-1)
        ws = (iw_ref[bb] * (IDX_HEADS ** -0.5 * IDX_DIM ** -0.5)) * jnp.maximum(sc, 0.0)
        for i in range(t):
            score_scr[bb * t + i:bb * t + i + 1, :] = jnp.sum(
                ws[i * IDX_HEADS:(i + 1) * IDX_HEADS], axis=0, keepdims=True)

    rows = SELECT_BATCHES * t
    assert t & (t - 1) == 0
    tq = lax.broadcasted_iota(I32, (rows, 1), 0) & (t - 1)
    kpos = lax.broadcasted_iota(I32, (1, SAMPLE_KEYS), 1)
    causal = kpos <= PAST_LEN + tq
    score = jnp.where(causal, score_scr[...], -jnp.inf)
    _topk_neg_mask(score, kpos, causal, topk, key_scr, mask_scr)
    for bb in range(SELECT_BATCHES):
        mask_ref[bb] = mask_scr[bb * t:(bb + 1) * t, :]


def _sample_select(iq, iw, iknew, page_table, cik):
    rows = SELECT_BATCHES * DEC_SEQ
    per_step = lambda *shape: pl.BlockSpec((SELECT_BATCHES,) + shape, lambda s, pt: (s,) + (0,) * len(shape))
    grid_spec = pltpu.PrefetchScalarGridSpec(
        num_scalar_prefetch=1,
        grid=(DEC_BATCH // SELECT_BATCHES,),
        in_specs=[per_step(DEC_SEQ * IDX_HEADS, IDX_DIM),
                  per_step(DEC_SEQ * IDX_HEADS, 1),
                  per_step(DEC_SEQ, IDX_DIM),
                  pl.BlockSpec(memory_space=pl.ANY)],
        out_specs=per_step(DEC_SEQ, SAMPLE_KEYS),
        scratch_shapes=[pltpu.VMEM((2, SELECT_BATCHES, PAST_LEN, IDX_DIM), F32),
                        pltpu.VMEM((SELECT_BATCHES, LANES, IDX_DIM), F32),
                        pltpu.VMEM((rows, SAMPLE_KEYS), F32),
                        pltpu.VMEM((rows, SAMPLE_KEYS), I32),
                        pltpu.VMEM((rows, SAMPLE_KEYS), F32),
                        pltpu.SemaphoreType.DMA((2,))])
    return pl.pallas_call(
        _sample_select_kernel,
        grid_spec=grid_spec,
        out_shape=jax.ShapeDtypeStruct((DEC_BATCH, DEC_SEQ, SAMPLE_KEYS), F32),
        compiler_params=_cparams(("arbitrary",), 40),
    )(page_table.reshape(-1), iq, iw, iknew, cik)


def _dsa_sample_kernel(pt_ref, q_ref, mask_ref, knew_ref, vnew_ref, bias_ref,
                       ck_ref, cv_ref, o_ref, kbuf, vbuf, knew_scr, vnew_scr, sem):
    b = pl.program_id(0)
    nb = pl.num_programs(0)
    slot = lax.rem(b, 2)
    t = DEC_SEQ
    d = ATT_HEAD_DIM
    rows16 = ATT_REP * t

    def page_copies(batch, slot_):
        copies = []
        for p in range(N_PAGES):
            page = pt_ref[batch * N_PAGES + p]
            kv_rows = PAGE_SIZE * ATT_KV_HEADS
            kv_src = pl.ds(pl.multiple_of(page * kv_rows, kv_rows), kv_rows)
            kv_dst = pl.ds(p * kv_rows, kv_rows)
            copies.append(pltpu.make_async_copy(ck_ref.at[kv_src], kbuf.at[slot_, kv_dst], sem.at[0, slot_]))
            copies.append(pltpu.make_async_copy(cv_ref.at[kv_src], vbuf.at[slot_, kv_dst], sem.at[1, slot_]))
        return copies

    def kv_head(buf, g):
        return buf[slot, pl.ds(g, PAST_LEN, stride=ATT_KV_HEADS), :]

    @pl.when(b == 0)
    def _():
        knew_scr[...] = jnp.zeros_like(knew_scr)
        vnew_scr[...] = jnp.zeros_like(vnew_scr)
        for cp in page_copies(0, 0):
            cp.start()

    @pl.when(b + 1 < nb)
    def _():
        for cp in page_copies(b + 1, 1 - slot):
            cp.start()

    knew_scr[0:t, :] = knew_ref[0]
    vnew_scr[0:t, :] = vnew_ref[0]

    assert t & (t - 1) == 0
    tq = lax.broadcasted_iota(I32, (rows16, 1), 0) & (t - 1)
    neg_mask = jnp.zeros((rows16, SAMPLE_KEYS), F32)
    for i in range(t):
        neg_mask = jnp.where(tq == i, mask_ref[0, i:i + 1, :], neg_mask)

    for cp in page_copies(b, slot):
        cp.wait()

    scale = d ** -0.5
    for g in range(ATT_KV_HEADS):
        cols = slice(g * d, (g + 1) * d)
        qg = q_ref[0, g].astype(BF16)
        logits = jnp.concatenate([_dot_nt(qg, kv_head(kbuf, g).astype(BF16)),
                                  _dot_nt(qg, knew_scr[:, cols].astype(BF16))], axis=-1)
        x = logits * scale + bias_ref[g] + neg_mask
        p = jnp.exp(x - jnp.max(x, axis=-1, keepdims=True))
        l = jnp.sum(p, axis=-1, keepdims=True)
        pb = p.astype(BF16)
        out = (_dot(pb[:, :PAST_LEN], kv_head(vbuf, g).astype(BF16))
               + _dot(pb[:, PAST_LEN:], vnew_scr[:, cols].astype(BF16))) / l
        for r in range(ATT_REP):
            h = g * ATT_REP + r
            o_ref[0, :, h * d:(h + 1) * d] = out[r * t:(r + 1) * t]


def _dsa_sample(q, mask, knew, vnew, bias, page_table, ck, cv):
    rows16 = ATT_REP * DEC_SEQ
    whole = lambda *shape: pl.BlockSpec(shape, lambda b, pt: (0,) * len(shape))
    per_b = lambda *shape: pl.BlockSpec((1,) + shape, lambda b, pt: (b,) + (0,) * len(shape))
    grid_spec = pltpu.PrefetchScalarGridSpec(
        num_scalar_prefetch=1,
        grid=(DEC_BATCH,),
        in_specs=[per_b(ATT_KV_HEADS, rows16, ATT_HEAD_DIM),
                  per_b(DEC_SEQ, SAMPLE_KEYS),
                  per_b(DEC_SEQ, ATT_KV_W),
                  per_b(DEC_SEQ, ATT_KV_W),
                  whole(ATT_KV_HEADS, rows16, SAMPLE_KEYS),
                  pl.BlockSpec(memory_space=pl.ANY),
                  pl.BlockSpec(memory_space=pl.ANY)],
        out_specs=per_b(DEC_SEQ, ATT_Q_W),
        scratch_shapes=[pltpu.VMEM((2, PAST_LEN * ATT_KV_HEADS, ATT_HEAD_DIM), F32),
                        pltpu.VMEM((2, PAST_LEN * ATT_KV_HEADS, ATT_HEAD_DIM), F32),
                        pltpu.VMEM((LANES, ATT_KV_W), F32),
                        pltpu.VMEM((LANES, ATT_KV_W), F32),
                        pltpu.SemaphoreType.DMA((2, 2))])
    return pl.pallas_call(
        _dsa_sample_kernel,
        grid_spec=grid_spec,
        out_shape=jax.ShapeDtypeStruct((DEC_BATCH, DEC_SEQ, ATT_Q_W), F32),
        compiler_params=_cparams(("arbitrary",), 40),
    )(page_table.reshape(-1), q, mask, knew, vnew, bias, ck, cv)


def _rope_tables(pos):
    half = RET_DK // 2
    freqs = ROPE_BASE ** (-jnp.arange(half, dtype=F32) / half)
    ang = pos.astype(F32)[:, None] * freqs[None, :]
    return jnp.cos(ang), jnp.sin(ang)


def _pad_att_in(w):
    assert w.shape[1] == ATT_IN_WIDTH
    return jnp.pad(w, ((0, 0), (0, ATT_IN_PAD - ATT_IN_WIDTH)))


def _row_tile(m):
    return min(m, 1024)


def kernel(x_prompt, x_sample, p_prompt, p_sample, state_ret, cache_k, cache_v, cache_idx_k, page_table,
           norm_mix, norm_ffn, norm_ple, norm_final, w_ret_in, ret_gn_gain, w_ret_out,
           w_att_in, w_att_out, rel_bias, w_ffn_in, w_ffn_out, w_ple_gate, w_ple_proj):
    bf = lambda w: w.astype(BF16)
    hp = x_prompt.reshape(M_PROMPT, D_MODEL)
    hs = x_sample.reshape(M_SAMPLE, D_MODEL)
    pp = p_prompt.reshape(DEPTH, M_PROMPT, PLE_DIM)
    ps = p_sample.reshape(DEPTH, M_SAMPLE, PLE_DIM)

    def mixer_in(h, g, w, tn):
        return _matmul(_rmsnorm(h, g, BF16), w, tm=_row_tile(h.shape[0]), tn=tn)

    def mixer_in_cast(h, g, w, tn):
        return _matmul_cast(_rmsnorm(h, g, BF16), w, tm=_row_tile(h.shape[0]), tn=tn)

    def mixer_out(y, w, h):
        return _matmul(y, w, h, tm=h.shape[0], tn=1024)

    def mixer_out_cast(y, w, h, tm):
        return _matmul_cast(y, w, h, tm=tm, tn=512)

    ffn_w = [(bf(w_ffn_in[i]), bf(w_ffn_out[i]), bf(w_ple_gate[i]), bf(w_ple_proj[i])) for i in range(DEPTH)]

    def tail(h, p, i, final_norm):
        w_in_b, w_out_b, w_gate_b, w_proj_b = ffn_w[i]
        h = _ffn(h, norm_ffn[i], w_in_b, w_out_b)
        return _ple(h, norm_ple[i], w_gate_b, p, i, w_proj_b, norm_final, final_norm)

    lg = jnp.log1p(-jnp.exp2(-5.0 - jnp.arange(RET_HEADS, dtype=F32)))
    cos_p, sin_p = _rope_tables(jnp.arange(SEQ, dtype=I32))
    cos_s, sin_s = _rope_tables(PAST_LEN + jnp.arange(DEC_SEQ, dtype=I32))
    zp, w_in = mixer_in_cast(hp, norm_mix[0], w_ret_in[0], 1024)
    y, ret_state_p = _ret_prompt(zp, lg, cos_p, sin_p, ret_gn_gain[0])
    zs = mixer_in(hs, norm_mix[0], w_in, 1024).reshape(DEC_BATCH, DEC_SEQ, RET_IN_WIDTH)
    y_s, ret_state_s = _ret_sample(zs, lg, cos_s, sin_s, ret_gn_gain[0], state_ret)
    hp_mix, w_out = mixer_out_cast(y, w_ret_out[0], hp, 512)
    hp = tail(hp_mix, pp, 0, False)
    hs = tail(mixer_out(bf(y_s.reshape(M_SAMPLE, RET_V_W)), w_out, hs), ps, 0, False)

    near, samp = _bias_tables(rel_bias)
    w_in = bf(_pad_att_in(w_att_in[0]))
    zp = mixer_in(hp, norm_mix[1], w_in, 896)
    a = _dsa_prompt(zp, near)
    zs = mixer_in(hs, norm_mix[1], w_in, 896).reshape(DEC_BATCH, DEC_SEQ, ATT_IN_PAD)
    q_s = zs[..., :ATT_Q_W].reshape(DEC_BATCH, DEC_SEQ, ATT_KV_HEADS, ATT_REP, ATT_HEAD_DIM)
    q_s = q_s.transpose(0, 2, 3, 1, 4).reshape(DEC_BATCH, ATT_KV_HEADS, ATT_REP * DEC_SEQ, ATT_HEAD_DIM)
    bias_s = samp[:, :DEC_SEQ].reshape(ATT_KV_HEADS, ATT_REP * DEC_SEQ, SAMPLE_KEYS)
    iq_s = zs[..., ATT_COL_IQ:ATT_COL_IQ + IDX_Q_W].reshape(DEC_BATCH, DEC_SEQ * IDX_HEADS, IDX_DIM)
    iw_s = zs[..., ATT_COL_IW:ATT_COL_IW + IDX_HEADS].reshape(DEC_BATCH, DEC_SEQ * IDX_HEADS, 1)
    k_s = zs[..., ATT_COL_K:ATT_COL_K + ATT_KV_W]
    v_s = zs[..., ATT_COL_V:ATT_COL_V + ATT_KV_W]
    ik_s = zs[..., ATT_COL_IK:ATT_COL_IK + IDX_DIM]
    n_phys = cache_k.shape[1]
    mask_s = _sample_select(iq_s, iw_s, ik_s, page_table, cache_idx_k[0])
    a_s = _dsa_sample(q_s, mask_s, k_s, v_s, bias_s, page_table,
                      cache_k[0].reshape(n_phys * PAGE_SIZE * ATT_KV_HEADS, ATT_HEAD_DIM),
                      cache_v[0].reshape(n_phys * PAGE_SIZE * ATT_KV_HEADS, ATT_HEAD_DIM))
    hp_mix, w_out = mixer_out_cast(a, w_att_out[0], hp, 1024)
    yp = tail(hp_mix, pp, 1, True)
    ys = tail(mixer_out(bf(a_s.reshape(M_SAMPLE, ATT_Q_W)), w_out, hs), ps, 1, True)

    kv_p = lambda col: zp[:, col:col + ATT_KV_W].reshape(1, BATCH, SEQ, ATT_KV_HEADS, ATT_HEAD_DIM)
    kv_s = lambda x: x.reshape(1, DEC_BATCH, DEC_SEQ, ATT_KV_HEADS, ATT_HEAD_DIM)
    return (yp.reshape(BATCH, SEQ, D_MODEL),
            ys.reshape(DEC_BATCH, DEC_SEQ, D_MODEL),
            ret_state_p[None],
            ret_state_s,
            kv_p(ATT_COL_K), kv_p(ATT_COL_V),
            zp[:, ATT_COL_IK:ATT_COL_IK + IDX_DIM].reshape(1, BATCH, SEQ, IDX_DIM),
            kv_s(k_s), kv_s(v_s),
            ik_s.reshape(1, DEC_BATCH, DEC_SEQ, IDX_DIM))
```

```python
import functools
import math

import jax
import jax.numpy as jnp
from jax import lax
from jax.experimental import pallas as pl
from jax.experimental.pallas import tpu as pltpu

F32 = jnp.float32
BF16 = jnp.bfloat16
I32 = jnp.int32

D_MODEL = 2048
BATCH = 4
SEQ = 2048
DEPTH = 2
DEC_BATCH = 128
DEC_SEQ = 4
PAST_LEN = 2048
PAGE_SIZE = 128
N_PAGES = PAST_LEN // PAGE_SIZE

RET_HEADS = 8
RET_DK = D_MODEL // RET_HEADS
RET_DV = 2 * D_MODEL // RET_HEADS
RET_CHUNK = 128
ROPE_BASE = 10000.0
ATT_HEADS = 16
ATT_HEAD_DIM = D_MODEL // ATT_HEADS
ATT_KV_HEADS = 4
ATT_REP = ATT_HEADS // ATT_KV_HEADS
IDX_HEADS = 16
IDX_DIM = 128
TOPK_MAX = 256
REL_BUCKETS = 32
REL_MAX_DIST = 128
FFN_HIDDEN = 5632
PLE_DIM = 256
EPS = 1e-6

RET_Q_W = RET_HEADS * RET_DK
RET_V_W = RET_HEADS * RET_DV
RET_IN_WIDTH = 2 * RET_Q_W + 2 * RET_V_W
ATT_Q_W = ATT_HEADS * ATT_HEAD_DIM
ATT_KV_W = ATT_KV_HEADS * ATT_HEAD_DIM
IDX_Q_W = IDX_HEADS * IDX_DIM

M_PROMPT = BATCH * SEQ
M_SAMPLE = DEC_BATCH * DEC_SEQ

LANES = 128
SUBLANES = 8
V7X_VMEM_BYTES = 64 * 1024 * 1024
MIB = 1024 * 1024

ATT_COL_Q = 0
ATT_COL_K = ATT_Q_W
ATT_COL_V = ATT_COL_K + ATT_KV_W
ATT_COL_IQ = ATT_COL_V + ATT_KV_W
ATT_COL_IK = ATT_COL_IQ + IDX_Q_W
ATT_COL_IW = ATT_COL_IK + IDX_DIM
ATT_IN_WIDTH = ATT_COL_IW + IDX_HEADS
ATT_IN_PAD = ATT_COL_IW + LANES
IQ_HALF_W = IDX_Q_W // 2
assert ATT_COL_IQ % IQ_HALF_W == 0

QBLOCK = 128
SAMPLE_KEYS = PAST_LEN + LANES
LOG2_E = math.log2(math.e)
INT_MIN = -(2 ** 31)
NEG_INF_KEY = -2139095041


def _cparams(semantics, vmem_mib):
    assert vmem_mib * MIB < V7X_VMEM_BYTES
    return pltpu.CompilerParams(dimension_semantics=semantics,
                                vmem_limit_bytes=vmem_mib * MIB)


def _dot(a, b):
    return jnp.dot(a, b, preferred_element_type=F32)


def _dot_nt(a, b):
    return lax.dot_general(a, b, (((1,), (1,)), ((), ())), preferred_element_type=F32)


def _rms(x, g):
    return x * lax.rsqrt(jnp.mean(x * x, axis=-1, keepdims=True) + EPS) * g


def _rmsnorm_kernel(x_ref, g_ref, o_ref):
    o_ref[...] = _rms(x_ref[...], g_ref[...]).astype(o_ref.dtype)


def _rmsnorm(x, g, out_dtype, tm=512):
    m, d = x.shape
    return pl.pallas_call(
        _rmsnorm_kernel,
        grid=(m // tm,),
        in_specs=[pl.BlockSpec((tm, d), lambda i: (i, 0)),
                  pl.BlockSpec((1, d), lambda i: (0, 0))],
        out_specs=pl.BlockSpec((tm, d), lambda i: (i, 0)),
        out_shape=jax.ShapeDtypeStruct((m, d), out_dtype),
        compiler_params=_cparams(("parallel",), 32),
    )(x, g.reshape(1, d))


def _mm_kernel(x_ref, w_ref, o_ref):
    o_ref[...] = _dot(x_ref[...], w_ref[...]).astype(o_ref.dtype)


def _mm_res_kernel(x_ref, w_ref, r_ref, o_ref):
    o_ref[...] = r_ref[...] + _dot(x_ref[...], w_ref[...])


def _matmul(x, w, res=None, *, tm, tn, out_dtype=F32, vmem_mib=48):
    m, k = x.shape
    n = w.shape[1]
    assert m % tm == 0 and n % tn == 0
    in_specs = [pl.BlockSpec((tm, k), lambda j, i: (i, 0)),
                pl.BlockSpec((k, tn), lambda j, i: (0, j))]
    args = [x, w]
    kern = _mm_kernel
    if res is not None:
        in_specs.append(pl.BlockSpec((tm, tn), lambda j, i: (i, j)))
        args.append(res)
        kern = _mm_res_kernel
    return pl.pallas_call(
        kern,
        grid=(n // tn, m // tm),
        in_specs=in_specs,
        out_specs=pl.BlockSpec((tm, tn), lambda j, i: (i, j)),
        out_shape=jax.ShapeDtypeStruct((m, n), out_dtype),
        compiler_params=_cparams(("parallel", "parallel"), vmem_mib),
    )(*args)


def _ffn_kernel(x_ref, g_ref, w1_ref, w2_ref, wo_ref, o_ref, xn_ref):
    @pl.when(pl.program_id(1) == 0)
    def _():
        x = x_ref[...]
        xn_ref[...] = _rms(x, g_ref[...]).astype(BF16)
        o_ref[...] = x

    xn = xn_ref[...]
    a = _dot(xn, w1_ref[0])
    b = _dot(xn, w2_ref[0])
    hid = (a * jax.nn.sigmoid(a) * b).astype(BF16)
    o_ref[...] += _dot(hid, wo_ref[0])


def _ffn(h, g, w_in, w_out, layer, th=512):
    m, d = h.shape
    tm = _row_tile(m)
    nh = FFN_HIDDEN // th
    return pl.pallas_call(
        _ffn_kernel,
        grid=(m // tm, nh),
        in_specs=[pl.BlockSpec((tm, d), lambda i, j: (i, 0), pipeline_mode=pl.Buffered(1)),
                  pl.BlockSpec((1, d), lambda i, j: (0, 0)),
                  pl.BlockSpec((1, d, th), lambda i, j: (layer, 0, j)),
                  pl.BlockSpec((1, d, th), lambda i, j: (layer, 0, j + nh)),
                  pl.BlockSpec((1, th, d), lambda i, j: (layer, j, 0))],
        out_specs=pl.BlockSpec((tm, d), lambda i, j: (i, 0)),
        out_shape=jax.ShapeDtypeStruct((m, d), F32),
        scratch_shapes=[pltpu.VMEM((tm, d), BF16)],
        compiler_params=_cparams(("parallel", "arbitrary"), 56),
    )(h, g.reshape(1, d), w_in, w_in, w_out)


def _ple_kernel(x_ref, g_ref, wg_ref, p_ref, wp_ref, gf_ref, o_ref, *, final_norm):
    x = x_ref[...]
    xn = _rms(x, g_ref[...]).astype(BF16)
    gate = jax.nn.sigmoid(_dot(xn, wg_ref[0]))
    y = x + gate * _dot(p_ref[0].astype(BF16), wp_ref[0])
    if final_norm:
        y = _rms(y, gf_ref[...])
    o_ref[...] = y


def _ple(h, g, w_gate, p, layer, w_proj, g_final, final_norm, tm=512):
    m, d = h.shape
    pd = p.shape[2]
    return pl.pallas_call(
        functools.partial(_ple_kernel, final_norm=final_norm),
        grid=(m // tm,),
        in_specs=[pl.BlockSpec((tm, d), lambda i: (i, 0)),
                  pl.BlockSpec((1, d), lambda i: (0, 0)),
                  pl.BlockSpec((1, d, d), lambda i: (layer, 0, 0)),
                  pl.BlockSpec((1, tm, pd), lambda i: (layer, i, 0)),
                  pl.BlockSpec((1, pd, d), lambda i: (layer, 0, 0)),
                  pl.BlockSpec((1, d), lambda i: (0, 0))],
        out_specs=pl.BlockSpec((tm, d), lambda i: (i, 0)),
        out_shape=jax.ShapeDtypeStruct((m, d), F32),
        compiler_params=_cparams(("parallel",), 48),
    )(h, g.reshape(1, d), w_gate, p, w_proj, g_final.reshape(1, d))


def _group_norm_gate(o, gate, gain):
    mu = jnp.mean(o, axis=-1, keepdims=True)
    var = jnp.mean(jnp.square(o - mu), axis=-1, keepdims=True)
    on = (o - mu) * lax.rsqrt(var + EPS)
    return gate * jax.nn.sigmoid(gate) * (on * gain)


RET_CHUNKS_PER_STEP = 8
RET_ROWS = RET_CHUNKS_PER_STEP * RET_CHUNK


def _ret_prompt_kernel(lg_ref, q_ref, k_ref, v_ref, g_ref, cos_ref, sin_ref, gain_ref,
                       y_ref, s_out_ref, s_scr):
    c = pl.program_id(2)
    lg = lg_ref[pl.program_id(1)]
    half = RET_DK // 2

    @pl.when(c == 0)
    def _():
        s_scr[...] = jnp.zeros_like(s_scr)

    n_col = lax.broadcasted_iota(I32, (RET_CHUNK, 1), 0).astype(F32)
    n_row = lax.broadcasted_iota(I32, (1, RET_CHUNK), 1).astype(F32)
    diff = n_col - n_row
    decay = jnp.where(diff >= 0, jnp.exp(lg * jnp.maximum(diff, 0.0)), 0.0)
    q_dec = jnp.exp(lg * (n_col + 1.0))
    k_dec = jnp.exp(lg * (RET_CHUNK - 1.0 - n_col))
    s_dec = jnp.exp(jnp.full((1, 1), lg * RET_CHUNK, F32))
    gain = gain_ref[...]

    def rotate(x, cos, sin):
        x1, x2 = x[:, :half], x[:, half:]
        return jnp.concatenate([x1 * cos - x2 * sin, x1 * sin + x2 * cos], axis=-1)

    for i in range(RET_CHUNKS_PER_STEP):
        rows = pl.ds(i * RET_CHUNK, RET_CHUNK)
        cos, sin = cos_ref[rows, :], sin_ref[rows, :]
        qr = rotate(q_ref[rows, :], cos, sin)
        kr = rotate(k_ref[rows, :], cos, sin) * (RET_DK ** -0.5)
        v = v_ref[rows, :].astype(BF16)
        s = s_scr[...]
        qb = qr.astype(BF16)
        scores = _dot_nt(qb, kr.astype(BF16)) * decay
        o = _dot(scores.astype(BF16), v) + _dot(qb, s.astype(BF16)) * q_dec
        kd_t = (kr * k_dec).T.astype(BF16)
        s_scr[...] = s_dec * s + _dot(kd_t, v)
        y_ref[rows, :] = _group_norm_gate(o, g_ref[rows, :], gain).astype(BF16)

    @pl.when(c == pl.num_programs(2) - 1)
    def _():
        s_out_ref[0, 0] = s_scr[...]


def _ret_prompt(z, lg, cos, sin, gain):
    steps = SEQ // RET_ROWS
    qb, vb = RET_Q_W // RET_DK, (2 * RET_Q_W) // RET_DV
    row = lambda b, h, c: b * steps + c
    return pl.pallas_call(
        _ret_prompt_kernel,
        grid=(BATCH, RET_HEADS, steps),
        in_specs=[pl.BlockSpec(memory_space=pltpu.SMEM),
                  pl.BlockSpec((RET_ROWS, RET_DK), lambda b, h, c: (row(b, h, c), h)),
                  pl.BlockSpec((RET_ROWS, RET_DK), lambda b, h, c: (row(b, h, c), qb + h)),
                  pl.BlockSpec((RET_ROWS, RET_DV), lambda b, h, c: (row(b, h, c), vb + h)),
                  pl.BlockSpec((RET_ROWS, RET_DV), lambda b, h, c: (row(b, h, c), vb + RET_HEADS + h)),
                  pl.BlockSpec((RET_ROWS, RET_DK // 2), lambda b, h, c: (c, 0)),
                  pl.BlockSpec((RET_ROWS, RET_DK // 2), lambda b, h, c: (c, 0)),
                  pl.BlockSpec((1, RET_DV), lambda b, h, c: (0, h))],
        out_specs=[pl.BlockSpec((RET_ROWS, RET_DV), lambda b, h, c: (row(b, h, c), h)),
                   pl.BlockSpec((1, 1, RET_DK, RET_DV), lambda b, h, c: (b, h, 0, 0))],
        out_shape=[jax.ShapeDtypeStruct((M_PROMPT, RET_V_W), BF16),
                   jax.ShapeDtypeStruct((BATCH, RET_HEADS, RET_DK, RET_DV), F32)],
        scratch_shapes=[pltpu.VMEM((RET_DK, RET_DV), F32)],
        compiler_params=_cparams(("parallel", "parallel", "arbitrary"), 32),
    )(lg, z, z, z, z, cos, sin, gain.reshape(1, RET_V_W))


def _ret_sample_kernel(lg_ref, zs_ref, cos_ref, sin_ref, gain_ref, s_ref, y_ref, s_out_ref,
                       q_pad, k_pad, kd_pad, v_pad):
    half = RET_DK // 2
    t = DEC_SEQ
    cos, sin = cos_ref[...], sin_ref[...]
    n_col = lax.broadcasted_iota(I32, (t, 1), 0).astype(F32)
    pad_rows = q_pad.shape[0]
    qi = lax.broadcasted_iota(I32, (pad_rows, LANES), 0)
    kj = lax.broadcasted_iota(I32, (pad_rows, LANES), 1)
    live = (qi >= kj) & (qi < t)
    diff = jnp.maximum(qi - kj, 0).astype(F32)

    def rotate(x):
        x1, x2 = x[:, :half], x[:, half:]
        return jnp.concatenate([x1 * cos - x2 * sin, x1 * sin + x2 * cos], axis=-1)

    q_pad[...] = jnp.zeros_like(q_pad)
    k_pad[...] = jnp.zeros_like(k_pad)
    kd_pad[...] = jnp.zeros_like(kd_pad)
    v_pad[...] = jnp.zeros_like(v_pad)

    for h in range(RET_HEADS):
        lg = lg_ref[h]
        kr = rotate(zs_ref[0, :, RET_Q_W + h * RET_DK:RET_Q_W + (h + 1) * RET_DK]) * (RET_DK ** -0.5)
        k_pad[0:t, :] = kr
        kd_pad[0:t, :] = kr * jnp.exp(lg * (t - 1.0 - n_col))
        q_pad[0:t, :] = rotate(zs_ref[0, :, h * RET_DK:(h + 1) * RET_DK])
        v_pad[0:t, :] = zs_ref[0, :, pl.ds(2 * RET_Q_W + h * RET_DV, RET_DV)]
        gate = zs_ref[0, :, pl.ds(2 * RET_Q_W + RET_V_W + h * RET_DV, RET_DV)]
        s = s_ref[0, 0, h]

        qb = q_pad[...].astype(BF16)
        vb = v_pad[...].astype(BF16)
        scores = _dot_nt(qb, k_pad[...].astype(BF16)) * jnp.where(live, jnp.exp(lg * diff), 0.0)
        o = _dot(scores.astype(BF16), vb)[0:t] + _dot(qb, s.astype(BF16))[0:t] * jnp.exp(lg * (n_col + 1.0))
        s_out_ref[0, 0, h] = (jnp.exp(jnp.full((1, 1), lg * t, F32)) * s
                              + _dot(kd_pad[...].T.astype(BF16), vb))
        y_ref[0, :, pl.ds(h * RET_DV, RET_DV)] = _group_norm_gate(
            o, gate, gain_ref[:, pl.ds(h * RET_DV, RET_DV)])


def _ret_sample(zs, lg, cos, sin, gain, state):
    half = RET_DK // 2
    whole = lambda *shape: pl.BlockSpec(shape, lambda b: (0,) * len(shape))
    return pl.pallas_call(
        _ret_sample_kernel,
        grid=(DEC_BATCH,),
        in_specs=[pl.BlockSpec(memory_space=pltpu.SMEM),
                  pl.BlockSpec((1, DEC_SEQ, RET_IN_WIDTH), lambda b: (b, 0, 0)),
                  whole(DEC_SEQ, half), whole(DEC_SEQ, half),
                  whole(1, RET_V_W),
                  pl.BlockSpec((1, 1, RET_HEADS, RET_DK, RET_DV), lambda b: (0, b, 0, 0, 0))],
        out_specs=[pl.BlockSpec((1, DEC_SEQ, RET_V_W), lambda b: (b, 0, 0)),
                   pl.BlockSpec((1, 1, RET_HEADS, RET_DK, RET_DV), lambda b: (0, b, 0, 0, 0))],
        out_shape=[jax.ShapeDtypeStruct((DEC_BATCH, DEC_SEQ, RET_V_W), F32),
                   jax.ShapeDtypeStruct(state.shape, F32)],
        scratch_shapes=[pltpu.VMEM((2 * SUBLANES, RET_DK), F32),
                        pltpu.VMEM((LANES, RET_DK), F32),
                        pltpu.VMEM((LANES, RET_DK), F32),
                        pltpu.VMEM((LANES, RET_DV), F32)],
        compiler_params=_cparams(("parallel",), 40),
    )(lg, zs, cos, sin, gain.reshape(1, RET_V_W), state)


def _rel_bucket(dist):
    n = jnp.maximum(dist, 0)
    max_exact = REL_BUCKETS // 2
    nf = jnp.maximum(n, max_exact).astype(F32)
    large = max_exact + (jnp.log(nf / max_exact) / math.log(REL_MAX_DIST / max_exact)
                         * (REL_BUCKETS - max_exact)).astype(I32)
    large = jnp.minimum(large, REL_BUCKETS - 1)
    return jnp.where(n < max_exact, n, large)


def _bias_kernel(table_ref, near_ref, samp_ref):
    h = pl.program_id(0)

    def lookup(dist):
        bucket = _rel_bucket(dist)
        acc = jnp.zeros(dist.shape, F32)
        for b in range(REL_BUCKETS):
            acc = jnp.where(bucket == b, table_ref[b, h], acc)
        return acc

    far = table_ref[REL_BUCKETS - 1, h]
    i = lax.broadcasted_iota(I32, (QBLOCK, 2 * QBLOCK), 0)
    j = lax.broadcasted_iota(I32, (QBLOCK, 2 * QBLOCK), 1)
    near = (lookup(QBLOCK + i - j) - far) * LOG2_E
    near_ref[0, :, 0:2 * QBLOCK] = near
    near_ref[0, :, 2 * QBLOCK:] = near[:, 0:QBLOCK]
    t = lax.broadcasted_iota(I32, (SUBLANES, SAMPLE_KEYS), 0)
    s = lax.broadcasted_iota(I32, (SUBLANES, SAMPLE_KEYS), 1)
    samp_ref[0] = lookup(PAST_LEN + t - s)


def _bias_tables(rel_bias):
    return pl.pallas_call(
        _bias_kernel,
        grid=(ATT_HEADS,),
        in_specs=[pl.BlockSpec(memory_space=pltpu.SMEM)],
        out_specs=[pl.BlockSpec((1, QBLOCK, 3 * QBLOCK), lambda h: (h, 0, 0)),
                   pl.BlockSpec((1, SUBLANES, SAMPLE_KEYS), lambda h: (h, 0, 0))],
        out_shape=[jax.ShapeDtypeStruct((ATT_HEADS, QBLOCK, 3 * QBLOCK), F32),
                   jax.ShapeDtypeStruct((ATT_HEADS, SUBLANES, SAMPLE_KEYS), F32)],
        compiler_params=_cparams(("parallel",), 16),
    )(rel_bias)


def _order_key(score):
    bits = lax.bitcast_convert_type(score + 0.0, I32)
    return bits ^ ((bits >> 31) & 0x7FFFFFFF)


def _topk_neg_mask(score, kpos, causal, topk, key_ref, mask_ref, *, key_axis=1):
    keys = score.shape[key_axis]
    key_ref[...] = _order_key(score)
    kf = float(topk)

    def count(mask):
        ind = jnp.where(mask, 1.0, 0.0)
        slab = 8 * SUBLANES
        if key_axis == 0 and keys % slab == 0 and keys > slab:
            ind = jnp.sum(ind.reshape(keys // slab, slab, ind.shape[1]), axis=0)
        return jnp.sum(ind, axis=key_axis, keepdims=True)

    prefix = jnp.where(count(key_ref[...] >= 0) >= kf, 0, INT_MIN).astype(I32)

    def thr_body(i, prefix):
        cand = prefix | jnp.left_shift(1, 30 - i)
        return jnp.where(count(key_ref[...] >= cand) >= kf, cand, prefix)

    thr = lax.fori_loop(0, 31, thr_body, prefix)
    ge = key_ref[...] >= thr
    mask_ref[...] = jnp.where(ge & causal, 0.0, -jnp.inf)

    tied = (count(ge) > kf) & (thr > NEG_INF_KEY)

    @pl.when(jnp.max(jnp.where(tied, 1.0, 0.0)) > 0.0)
    def _():
        need = kf - count(key_ref[...] > thr)
        nbits = (keys - 1).bit_length()

        def tie_body(i, last):
            cand = last | jnp.left_shift(1, nbits - 1 - i)
            before = (key_ref[...] == thr) & (kpos < cand)
            return jnp.where(count(before) < need, cand, last)

        last = lax.fori_loop(0, nbits, tie_body, jnp.zeros_like(thr))
        key = key_ref[...]
        keep = ((key > thr) | ((key == thr) & (kpos <= last))) & causal
        mask_ref[...] = jnp.where(keep, 0.0, -jnp.inf)


PROMPT_SPLITS = 8
SPLIT_QBLOCKS = SEQ // QBLOCK // PROMPT_SPLITS


def _dsa_prompt_kernel(q_ref, iq_lo_ref, iq_hi_ref, iw_ref, k_ref, v_ref, ik_ref, near_ref, prev_ref, o_ref,
                       score_scr, mask_t_scr, key_scr, mask_scr, lg_scr, *, split):
    del prev_ref
    nkeys = (split + 1) * SPLIT_QBLOCKS * QBLOCK
    j = split * SPLIT_QBLOCKS + pl.program_id(1)
    topk = min(TOPK_MAX, SEQ // 4)
    d = ATT_HEAD_DIM

    ikb = ik_ref[0:nkeys, :].astype(BF16)
    iw_t = (iw_ref[...] * (IDX_HEADS ** -0.5 * IDX_DIM ** -0.5)).T
    for h in range(IDX_HEADS):
        iq_ref, hh = (iq_lo_ref, h) if h < IDX_HEADS // 2 else (iq_hi_ref, h - IDX_HEADS // 2)
        sc = _dot_nt(ikb, iq_ref[:, hh * IDX_DIM:(hh + 1) * IDX_DIM].astype(BF16))
        term = iw_t[h:h + 1, :] * jnp.maximum(sc, 0.0)
        if h == 0:
            score_scr[...] = term
        else:
            score_scr[...] += term
    qpos = j * QBLOCK + lax.broadcasted_iota(I32, (1, QBLOCK), 1)
    kpos = lax.broadcasted_iota(I32, (nkeys, 1), 0)
    causal = kpos <= qpos
    score = jnp.where(causal, score_scr[...], -jnp.inf)
    _topk_neg_mask(score, kpos, causal, topk, key_scr, mask_t_scr, key_axis=0)
    mask_scr[...] = mask_t_scr[...].T

    scale = d ** -0.5 * LOG2_E
    win_start = pl.multiple_of(jnp.maximum(j - 1, 0) * QBLOCK, QBLOCK)
    near_start = pl.multiple_of(jnp.where(j == 0, QBLOCK, 0), QBLOCK)

    def group(g, carry):
        lanes = lambda i: pl.ds(pl.multiple_of(i * d, d), d)
        kg = k_ref[0:nkeys, lanes(g)].astype(BF16)
        vg = v_ref[0:nkeys, lanes(g)].astype(BF16)
        def logits_and_max(r):
            h = g * ATT_REP + r
            rows = slice(r * QBLOCK, (r + 1) * QBLOCK)
            lg_scr[rows, :] = _dot_nt((q_ref[:, lanes(h)] * scale).astype(BF16), kg)
            lg_scr[rows, pl.ds(win_start, 2 * QBLOCK)] += near_ref[h, :, pl.ds(near_start, 2 * QBLOCK)]
            return jnp.max(lg_scr[rows, :] + mask_scr[...], axis=-1, keepdims=True)

        def weights_and_values(r, m):
            h = g * ATT_REP + r
            rows = slice(r * QBLOCK, (r + 1) * QBLOCK)
            p = jnp.exp2(lg_scr[rows, :] + mask_scr[...] - m)
            l = jnp.sum(p, axis=-1, keepdims=True)
            o_ref[:, lanes(h)] = (_dot(p.astype(BF16), vg) / l).astype(BF16)

        m = logits_and_max(0)
        for r in range(1, ATT_REP):
            m_next = logits_and_max(r)
            weights_and_values(r - 1, m)
            m = m_next
        weights_and_values(ATT_REP - 1, m)
        return carry

    lax.fori_loop(0, ATT_KV_HEADS, group, 0)


def _dsa_prompt(z, near):
    nq = SEQ // QBLOCK
    out = jnp.zeros((M_PROMPT, ATT_Q_W), BF16)
    for split in range(PROMPT_SPLITS):
        nkeys = (split + 1) * SPLIT_QBLOCKS * QBLOCK
        qrow = lambda b, j, s=split: b * nq + s * SPLIT_QBLOCKS + j
        out = pl.pallas_call(
            functools.partial(_dsa_prompt_kernel, split=split),
            grid=(BATCH, SPLIT_QBLOCKS),
            in_specs=[pl.BlockSpec((QBLOCK, ATT_Q_W), lambda b, j: (qrow(b, j), ATT_COL_Q // ATT_Q_W)),
                      pl.BlockSpec((QBLOCK, IQ_HALF_W), lambda b, j: (qrow(b, j), ATT_COL_IQ // IQ_HALF_W)),
                      pl.BlockSpec((QBLOCK, IQ_HALF_W), lambda b, j: (qrow(b, j), ATT_COL_IQ // IQ_HALF_W + 1)),
                      pl.BlockSpec((QBLOCK, LANES), lambda b, j: (qrow(b, j), ATT_COL_IW // LANES)),
                      pl.BlockSpec((SEQ, ATT_KV_W), lambda b, j: (b, ATT_COL_K // ATT_KV_W)),
                      pl.BlockSpec((SEQ, ATT_KV_W), lambda b, j: (b, ATT_COL_V // ATT_KV_W)),
                      pl.BlockSpec((SEQ, IDX_DIM), lambda b, j: (b, ATT_COL_IK // IDX_DIM)),
                      pl.BlockSpec((ATT_HEADS, QBLOCK, 3 * QBLOCK), lambda b, j: (0, 0, 0)),
                      pl.BlockSpec(memory_space=pl.ANY)],
            out_specs=pl.BlockSpec((QBLOCK, ATT_Q_W), lambda b, j: (qrow(b, j), 0)),
            out_shape=jax.ShapeDtypeStruct((M_PROMPT, ATT_Q_W), BF16),
            input_output_aliases={8: 0},
            scratch_shapes=[pltpu.VMEM((nkeys, QBLOCK), F32), pltpu.VMEM((nkeys, QBLOCK), F32),
                            pltpu.VMEM((nkeys, QBLOCK), I32),
                            pltpu.VMEM((QBLOCK, nkeys), F32),
                            pltpu.VMEM((ATT_REP * QBLOCK, nkeys), F32)],
            compiler_params=_cparams(("parallel", "arbitrary"), 56),
        )(z, z, z, z, z, z, z, near, out)
    return out


SELECT_BATCHES = 8


def _sample_select_kernel(pt_ref, iq_ref, iw_ref, iknew_ref, cik_ref, mask_ref,
                          ikbuf, iknew_scr, score_scr, key_scr, mask_scr, sem):
    step = pl.program_id(0)
    nsteps = pl.num_programs(0)
    slot = lax.rem(step, 2)
    t = DEC_SEQ
    topk = min(TOPK_MAX, (PAST_LEN + DEC_SEQ) // 4)

    def page_copies(step_, slot_):
        copies = []
        for bb in range(SELECT_BATCHES):
            for p in range(N_PAGES):
                page = pt_ref[(step_ * SELECT_BATCHES + bb) * N_PAGES + p]
                dst = pl.ds(p * PAGE_SIZE, PAGE_SIZE)
                copies.append(pltpu.make_async_copy(cik_ref.at[page], ikbuf.at[slot_, bb, dst], sem.at[slot_]))
        return copies

    @pl.when(step == 0)
    def _():
        iknew_scr[...] = jnp.zeros_like(iknew_scr)
        for cp in page_copies(0, 0):
            cp.start()

    @pl.when(step + 1 < nsteps)
    def _():
        for cp in page_copies(step + 1, 1 - slot):
            cp.start()

    for cp in page_copies(step, slot):
        cp.wait()

    for bb in range(SELECT_BATCHES):
        iknew_scr[bb, 0:t, :] = iknew_ref[bb]
        iq = iq_ref[bb].astype(BF16)
        sc = jnp.concatenate([_dot_nt(iq, ikbuf[slot, bb].astype(BF16)),
                              _dot_nt(iq, iknew_scr[bb].astype(BF16))], axis=-1)
        ws = (iw_ref[bb] * (IDX_HEADS ** -0.5 * IDX_DIM ** -0.5)) * jnp.maximum(sc, 0.0)
        for i in range(t):
            score_scr[bb * t + i:bb * t + i + 1, :] = jnp.sum(
                ws[i * IDX_HEADS:(i + 1) * IDX_HEADS], axis=0, keepdims=True)

    rows = SELECT_BATCHES * t
    assert t & (t - 1) == 0
    tq = lax.broadcasted_iota(I32, (rows, 1), 0) & (t - 1)
    kpos = lax.broadcasted_iota(I32, (1, SAMPLE_KEYS), 1)
    causal = kpos <= PAST_LEN + tq
    score = jnp.where(causal, score_scr[...], -jnp.inf)
    _topk_neg_mask(score, kpos, causal, topk, key_scr, mask_scr)
    for bb in range(SELECT_BATCHES):
        mask_ref[bb] = mask_scr[bb * t:(bb + 1) * t, :]


def _sample_select(iq, iw, iknew, page_table, cik):
    rows = SELECT_BATCHES * DEC_SEQ
    per_step = lambda *shape: pl.BlockSpec((SELECT_BATCHES,) + shape, lambda s, pt: (s,) + (0,) * len(shape))
    grid_spec = pltpu.PrefetchScalarGridSpec(
        num_scalar_prefetch=1,
        grid=(DEC_BATCH // SELECT_BATCHES,),
        in_specs=[per_step(DEC_SEQ * IDX_HEADS, IDX_DIM),
                  per_step(DEC_SEQ * IDX_HEADS, 1),
                  per_step(DEC_SEQ, IDX_DIM),
                  pl.BlockSpec(memory_space=pl.ANY)],
        out_specs=per_step(DEC_SEQ, SAMPLE_KEYS),
        scratch_shapes=[pltpu.VMEM((2, SELECT_BATCHES, PAST_LEN, IDX_DIM), F32),
                        pltpu.VMEM((SELECT_BATCHES, LANES, IDX_DIM), F32),
                        pltpu.VMEM((rows, SAMPLE_KEYS), F32),
                        pltpu.VMEM((rows, SAMPLE_KEYS), I32),
                        pltpu.VMEM((rows, SAMPLE_KEYS), F32),
                        pltpu.SemaphoreType.DMA((2,))])
    return pl.pallas_call(
        _sample_select_kernel,
        grid_spec=grid_spec,
        out_shape=jax.ShapeDtypeStruct((DEC_BATCH, DEC_SEQ, SAMPLE_KEYS), F32),
        compiler_params=_cparams(("arbitrary",), 40),
    )(page_table.reshape(-1), iq, iw, iknew, cik)


def _dsa_sample_kernel(pt_ref, q_ref, mask_ref, knew_ref, vnew_ref, bias_ref,
                       ck_ref, cv_ref, o_ref, kbuf, vbuf, knew_scr, vnew_scr, sem):
    b = pl.program_id(0)
    nb = pl.num_programs(0)
    slot = lax.rem(b, 2)
    t = DEC_SEQ
    d = ATT_HEAD_DIM
    rows16 = ATT_REP * t

    def page_copies(batch, slot_):
        copies = []
        for p in range(N_PAGES):
            page = pt_ref[batch * N_PAGES + p]
            kv_rows = PAGE_SIZE * ATT_KV_HEADS
            kv_src = pl.ds(pl.multiple_of(page * kv_rows, kv_rows), kv_rows)
            kv_dst = pl.ds(p * kv_rows, kv_rows)
            copies.append(pltpu.make_async_copy(ck_ref.at[kv_src], kbuf.at[slot_, kv_dst], sem.at[0, slot_]))
            copies.append(pltpu.make_async_copy(cv_ref.at[kv_src], vbuf.at[slot_, kv_dst], sem.at[1, slot_]))
        return copies

    def kv_head(buf, g):
        return buf[slot, pl.ds(g, PAST_LEN, stride=ATT_KV_HEADS), :]

    @pl.when(b == 0)
    def _():
        knew_scr[...] = jnp.zeros_like(knew_scr)
        vnew_scr[...] = jnp.zeros_like(vnew_scr)
        for cp in page_copies(0, 0):
            cp.start()

    @pl.when(b + 1 < nb)
    def _():
        for cp in page_copies(b + 1, 1 - slot):
            cp.start()

    knew_scr[0:t, :] = knew_ref[0]
    vnew_scr[0:t, :] = vnew_ref[0]

    assert t & (t - 1) == 0
    tq = lax.broadcasted_iota(I32, (rows16, 1), 0) & (t - 1)
    neg_mask = jnp.zeros((rows16, SAMPLE_KEYS), F32)
    for i in range(t):
        neg_mask = jnp.where(tq == i, mask_ref[0, i:i + 1, :], neg_mask)

    for cp in page_copies(b, slot):
        cp.wait()

    scale = d ** -0.5
    for g in range(ATT_KV_HEADS):
        cols = slice(g * d, (g + 1) * d)
        qg = q_ref[0, g].astype(BF16)
        logits = jnp.concatenate([_dot_nt(qg, kv_head(kbuf, g).astype(BF16)),
                                  _dot_nt(qg, knew_scr[:, cols].astype(BF16))], axis=-1)
        x = logits * scale + bias_ref[g] + neg_mask
        p = jnp.exp(x - jnp.max(x, axis=-1, keepdims=True))
        l = jnp.sum(p, axis=-1, keepdims=True)
        pb = p.astype(BF16)
        out = (_dot(pb[:, :PAST_LEN], kv_head(vbuf, g).astype(BF16))
               + _dot(pb[:, PAST_LEN:], vnew_scr[:, cols].astype(BF16))) / l
        for r in range(ATT_REP):
            h = g * ATT_REP + r
            o_ref[0, :, h * d:(h + 1) * d] = out[r * t:(r + 1) * t]


def _dsa_sample(q, mask, knew, vnew, bias, page_table, ck, cv):
    rows16 = ATT_REP * DEC_SEQ
    whole = lambda *shape: pl.BlockSpec(shape, lambda b, pt: (0,) * len(shape))
    per_b = lambda *shape: pl.BlockSpec((1,) + shape, lambda b, pt: (b,) + (0,) * len(shape))
    grid_spec = pltpu.PrefetchScalarGridSpec(
        num_scalar_prefetch=1,
        grid=(DEC_BATCH,),
        in_specs=[per_b(ATT_KV_HEADS, rows16, ATT_HEAD_DIM),
                  per_b(DEC_SEQ, SAMPLE_KEYS),
                  per_b(DEC_SEQ, ATT_KV_W),
                  per_b(DEC_SEQ, ATT_KV_W),
                  whole(ATT_KV_HEADS, rows16, SAMPLE_KEYS),
                  pl.BlockSpec(memory_space=pl.ANY),
                  pl.BlockSpec(memory_space=pl.ANY)],
        out_specs=per_b(DEC_SEQ, ATT_Q_W),
        scratch_shapes=[pltpu.VMEM((2, PAST_LEN * ATT_KV_HEADS, ATT_HEAD_DIM), F32),
                        pltpu.VMEM((2, PAST_LEN * ATT_KV_HEADS, ATT_HEAD_DIM), F32),
                        pltpu.VMEM((LANES, ATT_KV_W), F32),
                        pltpu.VMEM((LANES, ATT_KV_W), F32),
                        pltpu.SemaphoreType.DMA((2, 2))])
    return pl.pallas_call(
        _dsa_sample_kernel,
        grid_spec=grid_spec,
        out_shape=jax.ShapeDtypeStruct((DEC_BATCH, DEC_SEQ, ATT_Q_W), F32),
        compiler_params=_cparams(("arbitrary",), 40),
    )(page_table.reshape(-1), q, mask, knew, vnew, bias, ck, cv)


def _rope_tables(pos):
    half = RET_DK // 2
    freqs = ROPE_BASE ** (-jnp.arange(half, dtype=F32) / half)
    ang = pos.astype(F32)[:, None] * freqs[None, :]
    return jnp.cos(ang), jnp.sin(ang)


def _pad_att_in(w):
    assert w.shape[1] == ATT_IN_WIDTH
    return jnp.pad(w, ((0, 0), (0, ATT_IN_PAD - ATT_IN_WIDTH)))


def _row_tile(m):
    return min(m, 1024)


def kernel(x_prompt, x_sample, p_prompt, p_sample, state_ret, cache_k, cache_v, cache_idx_k, page_table,
           norm_mix, norm_ffn, norm_ple, norm_final, w_ret_in, ret_gn_gain, w_ret_out,
           w_att_in, w_att_out, rel_bias, w_ffn_in, w_ffn_out, w_ple_gate, w_ple_proj):
    bf = lambda w: w.astype(BF16)
    hp = x_prompt.reshape(M_PROMPT, D_MODEL)
    hs = x_sample.reshape(M_SAMPLE, D_MODEL)
    pp = p_prompt.reshape(DEPTH, M_PROMPT, PLE_DIM)
    ps = p_sample.reshape(DEPTH, M_SAMPLE, PLE_DIM)

    def mixer_in(h, g, w, tn):
        return _matmul(_rmsnorm(h, g, BF16), w, tm=_row_tile(h.shape[0]), tn=tn)

    def mixer_out(y, w, h, tm):
        return _matmul(y, w, h, tm=min(tm, h.shape[0]), tn=1024)

    w_ffn_in_b, w_ffn_out_b = bf(w_ffn_in), bf(w_ffn_out)
    w_ple_gate_b, w_ple_proj_b = bf(w_ple_gate), bf(w_ple_proj)

    def tail(h, p, i, final_norm):
        h = _ffn(h, norm_ffn[i], w_ffn_in_b, w_ffn_out_b, i)
        return _ple(h, norm_ple[i], w_ple_gate_b, p, i, w_ple_proj_b, norm_final, final_norm)

    lg = jnp.log1p(-jnp.exp2(-5.0 - jnp.arange(RET_HEADS, dtype=F32)))
    cos_p, sin_p = _rope_tables(jnp.arange(SEQ, dtype=I32))
    cos_s, sin_s = _rope_tables(PAST_LEN + jnp.arange(DEC_SEQ, dtype=I32))
    w_in, w_out = bf(w_ret_in[0]), bf(w_ret_out[0])
    y, ret_state_p = _ret_prompt(mixer_in(hp, norm_mix[0], w_in, 2048), lg, cos_p, sin_p, ret_gn_gain[0])
    zs = mixer_in(hs, norm_mix[0], w_in, 2048).reshape(DEC_BATCH, DEC_SEQ, RET_IN_WIDTH)
    y_s, ret_state_s = _ret_sample(zs, lg, cos_s, sin_s, ret_gn_gain[0], state_ret)
    hp = tail(mixer_out(y, w_out, hp, 512), pp, 0, False)
    hs = tail(mixer_out(bf(y_s.reshape(M_SAMPLE, RET_V_W)), w_out, hs, 512), ps, 0, False)

    near, samp = _bias_tables(rel_bias)
    w_in, w_out = bf(_pad_att_in(w_att_in[0])), bf(w_att_out[0])
    zp = mixer_in(hp, norm_mix[1], w_in, 1792)
    a = _dsa_prompt(zp, near)
    zs = mixer_in(hs, norm_mix[1], w_in, 1792).reshape(DEC_BATCH, DEC_SEQ, ATT_IN_PAD)
    q_s = zs[..., :ATT_Q_W].reshape(DEC_BATCH, DEC_SEQ, ATT_KV_HEADS, ATT_REP, ATT_HEAD_DIM)
    q_s = q_s.transpose(0, 2, 3, 1, 4).reshape(DEC_BATCH, ATT_KV_HEADS, ATT_REP * DEC_SEQ, ATT_HEAD_DIM)
    bias_s = samp[:, :DEC_SEQ].reshape(ATT_KV_HEADS, ATT_REP * DEC_SEQ, SAMPLE_KEYS)
    iq_s = zs[..., ATT_COL_IQ:ATT_COL_IQ + IDX_Q_W].reshape(DEC_BATCH, DEC_SEQ * IDX_HEADS, IDX_DIM)
    iw_s = zs[..., ATT_COL_IW:ATT_COL_IW + IDX_HEADS].reshape(DEC_BATCH, DEC_SEQ * IDX_HEADS, 1)
    k_s = zs[..., ATT_COL_K:ATT_COL_K + ATT_KV_W]
    v_s = zs[..., ATT_COL_V:ATT_COL_V + ATT_KV_W]
    ik_s = zs[..., ATT_COL_IK:ATT_COL_IK + IDX_DIM]
    n_phys = cache_k.shape[1]
    mask_s = _sample_select(iq_s, iw_s, ik_s, page_table, cache_idx_k[0])
    a_s = _dsa_sample(q_s, mask_s, k_s, v_s, bias_s, page_table,
                      cache_k[0].reshape(n_phys * PAGE_SIZE * ATT_KV_HEADS, ATT_HEAD_DIM),
                      cache_v[0].reshape(n_phys * PAGE_SIZE * ATT_KV_HEADS, ATT_HEAD_DIM))
    yp = tail(mixer_out(a, w_out, hp, 1024), pp, 1, True)
    ys = tail(mixer_out(bf(a_s.reshape(M_SAMPLE, ATT_Q_W)), w_out, hs, 1024), ps, 1, True)

    kv_p = lambda col: zp[:, col:col + ATT_KV_W].reshape(1, BATCH, SEQ, ATT_KV_HEADS, ATT_HEAD_DIM)
    kv_s = lambda x: x.reshape(1, DEC_BATCH, DEC_SEQ, ATT_KV_HEADS, ATT_HEAD_DIM)
    return (yp.reshape(BATCH, SEQ, D_MODEL),
            ys.reshape(DEC_BATCH, DEC_SEQ, D_MODEL),
            ret_state_p[None],
            ret_state_s,
            kv_p(ATT_COL_K), kv_p(ATT_COL_V),
            zp[:, ATT_COL_IK:ATT_COL_IK + IDX_DIM].reshape(1, BATCH, SEQ, IDX_DIM),
            kv_s(k_s), kv_s(v_s),
            ik_s.reshape(1, DEC_BATCH, DEC_SEQ, IDX_DIM))
```

```python
import functools
import math

import jax
import jax.numpy as jnp
from jax import lax
from jax.experimental import pallas as pl
from jax.experimental.pallas import tpu as pltpu

F32 = jnp.float32
BF16 = jnp.bfloat16
I32 = jnp.int32

D_MODEL = 2048
BATCH = 4
SEQ = 2048
DEPTH = 2
DEC_BATCH = 128
DEC_SEQ = 4
PAST_LEN = 2048
PAGE_SIZE = 128
N_PAGES = PAST_LEN // PAGE_SIZE

RET_HEADS = 8
RET_DK = D_MODEL // RET_HEADS
RET_DV = 2 * D_MODEL // RET_HEADS
RET_CHUNK = 128
ROPE_BASE = 10000.0
ATT_HEADS = 16
ATT_HEAD_DIM = D_MODEL // ATT_HEADS
ATT_KV_HEADS = 4
ATT_REP = ATT_HEADS // ATT_KV_HEADS
IDX_HEADS = 16
IDX_DIM = 128
TOPK_MAX = 256
REL_BUCKETS = 32
REL_MAX_DIST = 128
FFN_HIDDEN = 5632
PLE_DIM = 256
EPS = 1e-6

RET_Q_W = RET_HEADS * RET_DK
RET_V_W = RET_HEADS * RET_DV
RET_IN_WIDTH = 2 * RET_Q_W + 2 * RET_V_W
ATT_Q_W = ATT_HEADS * ATT_HEAD_DIM
ATT_KV_W = ATT_KV_HEADS * ATT_HEAD_DIM
IDX_Q_W = IDX_HEADS * IDX_DIM

M_PROMPT = BATCH * SEQ
M_SAMPLE = DEC_BATCH * DEC_SEQ

LANES = 128
SUBLANES = 8
V7X_VMEM_BYTES = 64 * 1024 * 1024
MIB = 1024 * 1024

ATT_COL_Q = 0
ATT_COL_K = ATT_Q_W
ATT_COL_V = ATT_COL_K + ATT_KV_W
ATT_COL_IQ = ATT_COL_V + ATT_KV_W
ATT_COL_IK = ATT_COL_IQ + IDX_Q_W
ATT_COL_IW = ATT_COL_IK + IDX_DIM
ATT_IN_WIDTH = ATT_COL_IW + IDX_HEADS
ATT_IN_PAD = ATT_COL_IW + LANES
IQ_HALF_W = IDX_Q_W // 2
assert ATT_COL_IQ % IQ_HALF_W == 0

QBLOCK = 128
SAMPLE_KEYS = PAST_LEN + LANES
LOG2_E = math.log2(math.e)
INT_MIN = -(2 ** 31)
NEG_INF_KEY = -2139095041


def _cparams(semantics, vmem_mib):
    assert vmem_mib * MIB < V7X_VMEM_BYTES
    return pltpu.CompilerParams(dimension_semantics=semantics,
                                vmem_limit_bytes=vmem_mib * MIB)


def _dot(a, b):
    return jnp.dot(a, b, preferred_element_type=F32)


def _dot_nt(a, b):
    return lax.dot_general(a, b, (((1,), (1,)), ((), ())), preferred_element_type=F32)


def _rms(x, g):
    return x * lax.rsqrt(jnp.mean(x * x, axis=-1, keepdims=True) + EPS) * g


def _rmsnorm_kernel(x_ref, g_ref, o_ref):
    o_ref[...] = _rms(x_ref[...], g_ref[...]).astype(o_ref.dtype)


def _rmsnorm(x, g, out_dtype, tm=512):
    m, d = x.shape
    return pl.pallas_call(
        _rmsnorm_kernel,
        grid=(m // tm,),
        in_specs=[pl.BlockSpec((tm, d), lambda i: (i, 0)),
                  pl.BlockSpec((1, d), lambda i: (0, 0))],
        out_specs=pl.BlockSpec((tm, d), lambda i: (i, 0)),
        out_shape=jax.ShapeDtypeStruct((m, d), out_dtype),
        compiler_params=_cparams(("parallel",), 32),
    )(x, g.reshape(1, d))


def _mm_kernel(x_ref, w_ref, o_ref):
    o_ref[...] = _dot(x_ref[...], w_ref[...]).astype(o_ref.dtype)


def _mm_res_kernel(x_ref, w_ref, r_ref, o_ref):
    o_ref[...] = r_ref[...] + _dot(x_ref[...], w_ref[...])


def _matmul(x, w, res=None, *, tm, tn, out_dtype=F32, vmem_mib=48):
    m, k = x.shape
    n = w.shape[1]
    assert m % tm == 0 and n % tn == 0
    in_specs = [pl.BlockSpec((tm, k), lambda j, i: (i, 0)),
                pl.BlockSpec((k, tn), lambda j, i: (0, j))]
    args = [x, w]
    kern = _mm_kernel
    if res is not None:
        in_specs.append(pl.BlockSpec((tm, tn), lambda j, i: (i, j)))
        args.append(res)
        kern = _mm_res_kernel
    return pl.pallas_call(
        kern,
        grid=(n // tn, m // tm),
        in_specs=in_specs,
        out_specs=pl.BlockSpec((tm, tn), lambda j, i: (i, j)),
        out_shape=jax.ShapeDtypeStruct((m, n), out_dtype),
        compiler_params=_cparams(("parallel", "parallel"), vmem_mib),
    )(*args)


def _ffn_kernel(x_ref, g_ref, w1_ref, w2_ref, wo_ref, o_ref, xn_ref):
    @pl.when(pl.program_id(1) == 0)
    def _():
        x = x_ref[...]
        xn_ref[...] = _rms(x, g_ref[...]).astype(BF16)
        o_ref[...] = x

    xn = xn_ref[...]
    a = _dot(xn, w1_ref[0])
    b = _dot(xn, w2_ref[0])
    hid = (a * jax.nn.sigmoid(a) * b).astype(BF16)
    o_ref[...] += _dot(hid, wo_ref[0])


def _ffn(h, g, w_in, w_out, layer, tm=512, th=512):
    m, d = h.shape
    nh = FFN_HIDDEN // th
    return pl.pallas_call(
        _ffn_kernel,
        grid=(m // tm, nh),
        in_specs=[pl.BlockSpec((tm, d), lambda i, j: (i, 0)),
                  pl.BlockSpec((1, d), lambda i, j: (0, 0)),
                  pl.BlockSpec((1, d, th), lambda i, j: (layer, 0, j)),
                  pl.BlockSpec((1, d, th), lambda i, j: (layer, 0, j + nh)),
                  pl.BlockSpec((1, th, d), lambda i, j: (layer, j, 0))],
        out_specs=pl.BlockSpec((tm, d), lambda i, j: (i, 0)),
        out_shape=jax.ShapeDtypeStruct((m, d), F32),
        scratch_shapes=[pltpu.VMEM((tm, d), BF16)],
        compiler_params=_cparams(("parallel", "arbitrary"), 56),
    )(h, g.reshape(1, d), w_in, w_in, w_out)


def _ple_kernel(x_ref, g_ref, wg_ref, p_ref, wp_ref, gf_ref, o_ref, *, final_norm):
    x = x_ref[...]
    xn = _rms(x, g_ref[...]).astype(BF16)
    gate = jax.nn.sigmoid(_dot(xn, wg_ref[0]))
    y = x + gate * _dot(p_ref[0].astype(BF16), wp_ref[0])
    if final_norm:
        y = _rms(y, gf_ref[...])
    o_ref[...] = y


def _ple(h, g, w_gate, p, layer, w_proj, g_final, final_norm, tm=512):
    m, d = h.shape
    pd = p.shape[2]
    return pl.pallas_call(
        functools.partial(_ple_kernel, final_norm=final_norm),
        grid=(m // tm,),
        in_specs=[pl.BlockSpec((tm, d), lambda i: (i, 0)),
                  pl.BlockSpec((1, d), lambda i: (0, 0)),
                  pl.BlockSpec((1, d, d), lambda i: (layer, 0, 0)),
                  pl.BlockSpec((1, tm, pd), lambda i: (layer, i, 0)),
                  pl.BlockSpec((1, pd, d), lambda i: (layer, 0, 0)),
                  pl.BlockSpec((1, d), lambda i: (0, 0))],
        out_specs=pl.BlockSpec((tm, d), lambda i: (i, 0)),
        out_shape=jax.ShapeDtypeStruct((m, d), F32),
        compiler_params=_cparams(("parallel",), 48),
    )(h, g.reshape(1, d), w_gate, p, w_proj, g_final.reshape(1, d))


def _group_norm_gate(o, gate, gain):
    mu = jnp.mean(o, axis=-1, keepdims=True)
    var = jnp.mean(jnp.square(o - mu), axis=-1, keepdims=True)
    on = (o - mu) * lax.rsqrt(var + EPS)
    return gate * jax.nn.sigmoid(gate) * (on * gain)


RET_CHUNKS_PER_STEP = 8
RET_ROWS = RET_CHUNKS_PER_STEP * RET_CHUNK


def _ret_prompt_kernel(lg_ref, q_ref, k_ref, v_ref, g_ref, cos_ref, sin_ref, gain_ref,
                       y_ref, s_out_ref, s_scr):
    c = pl.program_id(2)
    lg = lg_ref[pl.program_id(1)]
    half = RET_DK // 2

    @pl.when(c == 0)
    def _():
        s_scr[...] = jnp.zeros_like(s_scr)

    n_col = lax.broadcasted_iota(I32, (RET_CHUNK, 1), 0).astype(F32)
    n_row = lax.broadcasted_iota(I32, (1, RET_CHUNK), 1).astype(F32)
    diff = n_col - n_row
    decay = jnp.where(diff >= 0, jnp.exp(lg * jnp.maximum(diff, 0.0)), 0.0)
    q_dec = jnp.exp(lg * (n_col + 1.0))
    k_dec = jnp.exp(lg * (RET_CHUNK - 1.0 - n_col))
    s_dec = jnp.exp(jnp.full((1, 1), lg * RET_CHUNK, F32))
    gain = gain_ref[...]

    def rotate(x, cos, sin):
        x1, x2 = x[:, :half], x[:, half:]
        return jnp.concatenate([x1 * cos - x2 * sin, x1 * sin + x2 * cos], axis=-1)

    for i in range(RET_CHUNKS_PER_STEP):
        rows = pl.ds(i * RET_CHUNK, RET_CHUNK)
        cos, sin = cos_ref[rows, :], sin_ref[rows, :]
        qr = rotate(q_ref[rows, :], cos, sin)
        kr = rotate(k_ref[rows, :], cos, sin) * (RET_DK ** -0.5)
        v = v_ref[rows, :].astype(BF16)
        s = s_scr[...]
        qb = qr.astype(BF16)
        scores = _dot_nt(qb, kr.astype(BF16)) * decay
        o = _dot(scores.astype(BF16), v) + _dot(qb, s.astype(BF16)) * q_dec
        kd_t = (kr * k_dec).T.astype(BF16)
        s_scr[...] = s_dec * s + _dot(kd_t, v)
        y_ref[rows, :] = _group_norm_gate(o, g_ref[rows, :], gain).astype(BF16)

    @pl.when(c == pl.num_programs(2) - 1)
    def _():
        s_out_ref[0, 0] = s_scr[...]


def _ret_prompt(z, lg, cos, sin, gain):
    steps = SEQ // RET_ROWS
    qb, vb = RET_Q_W // RET_DK, (2 * RET_Q_W) // RET_DV
    row = lambda b, h, c: b * steps + c
    return pl.pallas_call(
        _ret_prompt_kernel,
        grid=(BATCH, RET_HEADS, steps),
        in_specs=[pl.BlockSpec(memory_space=pltpu.SMEM),
                  pl.BlockSpec((RET_ROWS, RET_DK), lambda b, h, c: (row(b, h, c), h)),
                  pl.BlockSpec((RET_ROWS, RET_DK), lambda b, h, c: (row(b, h, c), qb + h)),
                  pl.BlockSpec((RET_ROWS, RET_DV), lambda b, h, c: (row(b, h, c), vb + h)),
                  pl.BlockSpec((RET_ROWS, RET_DV), lambda b, h, c: (row(b, h, c), vb + RET_HEADS + h)),
                  pl.BlockSpec((RET_ROWS, RET_DK // 2), lambda b, h, c: (c, 0)),
                  pl.BlockSpec((RET_ROWS, RET_DK // 2), lambda b, h, c: (c, 0)),
                  pl.BlockSpec((1, RET_DV), lambda b, h, c: (0, h))],
        out_specs=[pl.BlockSpec((RET_ROWS, RET_DV), lambda b, h, c: (row(b, h, c), h)),
                   pl.BlockSpec((1, 1, RET_DK, RET_DV), lambda b, h, c: (b, h, 0, 0))],
        out_shape=[jax.ShapeDtypeStruct((M_PROMPT, RET_V_W), BF16),
                   jax.ShapeDtypeStruct((BATCH, RET_HEADS, RET_DK, RET_DV), F32)],
        scratch_shapes=[pltpu.VMEM((RET_DK, RET_DV), F32)],
        compiler_params=_cparams(("parallel", "parallel", "arbitrary"), 32),
    )(lg, z, z, z, z, cos, sin, gain.reshape(1, RET_V_W))


def _ret_sample_kernel(lg_ref, zs_ref, cos_ref, sin_ref, gain_ref, s_ref, y_ref, s_out_ref,
                       q_pad, k_pad, kd_pad, v_pad):
    half = RET_DK // 2
    t = DEC_SEQ
    cos, sin = cos_ref[...], sin_ref[...]
    n_col = lax.broadcasted_iota(I32, (t, 1), 0).astype(F32)
    pad_rows = q_pad.shape[0]
    qi = lax.broadcasted_iota(I32, (pad_rows, LANES), 0)
    kj = lax.broadcasted_iota(I32, (pad_rows, LANES), 1)
    live = (qi >= kj) & (qi < t)
    diff = jnp.maximum(qi - kj, 0).astype(F32)

    def rotate(x):
        x1, x2 = x[:, :half], x[:, half:]
        return jnp.concatenate([x1 * cos - x2 * sin, x1 * sin + x2 * cos], axis=-1)

    q_pad[...] = jnp.zeros_like(q_pad)
    k_pad[...] = jnp.zeros_like(k_pad)
    kd_pad[...] = jnp.zeros_like(kd_pad)
    v_pad[...] = jnp.zeros_like(v_pad)

    for h in range(RET_HEADS):
        lg = lg_ref[h]
        kr = rotate(zs_ref[0, :, RET_Q_W + h * RET_DK:RET_Q_W + (h + 1) * RET_DK]) * (RET_DK ** -0.5)
        k_pad[0:t, :] = kr
        kd_pad[0:t, :] = kr * jnp.exp(lg * (t - 1.0 - n_col))
        q_pad[0:t, :] = rotate(zs_ref[0, :, h * RET_DK:(h + 1) * RET_DK])
        v_pad[0:t, :] = zs_ref[0, :, pl.ds(2 * RET_Q_W + h * RET_DV, RET_DV)]
        gate = zs_ref[0, :, pl.ds(2 * RET_Q_W + RET_V_W + h * RET_DV, RET_DV)]
        s = s_ref[0, 0, h]

        qb = q_pad[...].astype(BF16)
        vb = v_pad[...].astype(BF16)
        scores = _dot_nt(qb, k_pad[...].astype(BF16)) * jnp.where(live, jnp.exp(lg * diff), 0.0)
        o = _dot(scores.astype(BF16), vb)[0:t] + _dot(qb, s.astype(BF16))[0:t] * jnp.exp(lg * (n_col + 1.0))
        s_out_ref[0, 0, h] = (jnp.exp(jnp.full((1, 1), lg * t, F32)) * s
                              + _dot(kd_pad[...].T.astype(BF16), vb))
        y_ref[0, :, pl.ds(h * RET_DV, RET_DV)] = _group_norm_gate(
            o, gate, gain_ref[:, pl.ds(h * RET_DV, RET_DV)])


def _ret_sample(zs, lg, cos, sin, gain, state):
    half = RET_DK // 2
    whole = lambda *shape: pl.BlockSpec(shape, lambda b: (0,) * len(shape))
    return pl.pallas_call(
        _ret_sample_kernel,
        grid=(DEC_BATCH,),
        in_specs=[pl.BlockSpec(memory_space=pltpu.SMEM),
                  pl.BlockSpec((1, DEC_SEQ, RET_IN_WIDTH), lambda b: (b, 0, 0)),
                  whole(DEC_SEQ, half), whole(DEC_SEQ, half),
                  whole(1, RET_V_W),
                  pl.BlockSpec((1, 1, RET_HEADS, RET_DK, RET_DV), lambda b: (0, b, 0, 0, 0))],
        out_specs=[pl.BlockSpec((1, DEC_SEQ, RET_V_W), lambda b: (b, 0, 0)),
                   pl.BlockSpec((1, 1, RET_HEADS, RET_DK, RET_DV), lambda b: (0, b, 0, 0, 0))],
        out_shape=[jax.ShapeDtypeStruct((DEC_BATCH, DEC_SEQ, RET_V_W), F32),
                   jax.ShapeDtypeStruct(state.shape, F32)],
        scratch_shapes=[pltpu.VMEM((2 * SUBLANES, RET_DK), F32),
                        pltpu.VMEM((LANES, RET_DK), F32),
                        pltpu.VMEM((LANES, RET_DK), F32),
                        pltpu.VMEM((LANES, RET_DV), F32)],
        compiler_params=_cparams(("parallel",), 40),
    )(lg, zs, cos, sin, gain.reshape(1, RET_V_W), state)


def _rel_bucket(dist):
    n = jnp.maximum(dist, 0)
    max_exact = REL_BUCKETS // 2
    nf = jnp.maximum(n, max_exact).astype(F32)
    large = max_exact + (jnp.log(nf / max_exact) / math.log(REL_MAX_DIST / max_exact)
                         * (REL_BUCKETS - max_exact)).astype(I32)
    large = jnp.minimum(large, REL_BUCKETS - 1)
    return jnp.where(n < max_exact, n, large)


def _bias_kernel(table_ref, near_ref, samp_ref):
    h = pl.program_id(0)

    def lookup(dist):
        bucket = _rel_bucket(dist)
        acc = jnp.zeros(dist.shape, F32)
        for b in range(REL_BUCKETS):
            acc = jnp.where(bucket == b, table_ref[b, h], acc)
        return acc

    far = table_ref[REL_BUCKETS - 1, h]
    i = lax.broadcasted_iota(I32, (QBLOCK, 2 * QBLOCK), 0)
    j = lax.broadcasted_iota(I32, (QBLOCK, 2 * QBLOCK), 1)
    near = (lookup(QBLOCK + i - j) - far) * LOG2_E
    near_ref[0, :, 0:2 * QBLOCK] = near
    near_ref[0, :, 2 * QBLOCK:] = near[:, 0:QBLOCK]
    t = lax.broadcasted_iota(I32, (SUBLANES, SAMPLE_KEYS), 0)
    s = lax.broadcasted_iota(I32, (SUBLANES, SAMPLE_KEYS), 1)
    samp_ref[0] = lookup(PAST_LEN + t - s)


def _bias_tables(rel_bias):
    return pl.pallas_call(
        _bias_kernel,
        grid=(ATT_HEADS,),
        in_specs=[pl.BlockSpec(memory_space=pltpu.SMEM)],
        out_specs=[pl.BlockSpec((1, QBLOCK, 3 * QBLOCK), lambda h: (h, 0, 0)),
                   pl.BlockSpec((1, SUBLANES, SAMPLE_KEYS), lambda h: (h, 0, 0))],
        out_shape=[jax.ShapeDtypeStruct((ATT_HEADS, QBLOCK, 3 * QBLOCK), F32),
                   jax.ShapeDtypeStruct((ATT_HEADS, SUBLANES, SAMPLE_KEYS), F32)],
        compiler_params=_cparams(("parallel",), 16),
    )(rel_bias)


def _order_key(score):
    bits = lax.bitcast_convert_type(score + 0.0, I32)
    return bits ^ ((bits >> 31) & 0x7FFFFFFF)


def _topk_neg_mask(score, kpos, causal, topk, key_ref, mask_ref, *, key_axis=1):
    keys = score.shape[key_axis]
    key_ref[...] = _order_key(score)
    kf = float(topk)

    def count(mask):
        ind = jnp.where(mask, 1.0, 0.0)
        slab = 8 * SUBLANES
        if key_axis == 0 and keys % slab == 0 and keys > slab:
            ind = jnp.sum(ind.reshape(keys // slab, slab, ind.shape[1]), axis=0)
        return jnp.sum(ind, axis=key_axis, keepdims=True)

    prefix = jnp.where(count(key_ref[...] >= 0) >= kf, 0, INT_MIN).astype(I32)

    def thr_body(i, prefix):
        cand = prefix | jnp.left_shift(1, 30 - i)
        return jnp.where(count(key_ref[...] >= cand) >= kf, cand, prefix)

    thr = lax.fori_loop(0, 31, thr_body, prefix)
    ge = key_ref[...] >= thr
    mask_ref[...] = jnp.where(ge & causal, 0.0, -jnp.inf)

    tied = (count(ge) > kf) & (thr > NEG_INF_KEY)

    @pl.when(jnp.max(jnp.where(tied, 1.0, 0.0)) > 0.0)
    def _():
        need = kf - count(key_ref[...] > thr)
        nbits = (keys - 1).bit_length()

        def tie_body(i, last):
            cand = last | jnp.left_shift(1, nbits - 1 - i)
            before = (key_ref[...] == thr) & (kpos < cand)
            return jnp.where(count(before) < need, cand, last)

        last = lax.fori_loop(0, nbits, tie_body, jnp.zeros_like(thr))
        key = key_ref[...]
        keep = ((key > thr) | ((key == thr) & (kpos <= last))) & causal
        mask_ref[...] = jnp.where(keep, 0.0, -jnp.inf)


PROMPT_SPLITS = 8
SPLIT_QBLOCKS = SEQ // QBLOCK // PROMPT_SPLITS


def _dsa_prompt_kernel(q_ref, iq_lo_ref, iq_hi_ref, iw_ref, k_ref, v_ref, ik_ref, near_ref, prev_ref, o_ref,
                       score_scr, mask_t_scr, key_scr, mask_scr, lg_scr, *, split):
    del prev_ref
    nkeys = (split + 1) * SPLIT_QBLOCKS * QBLOCK
    j = split * SPLIT_QBLOCKS + pl.program_id(1)
    topk = min(TOPK_MAX, SEQ // 4)
    d = ATT_HEAD_DIM

    ikb = ik_ref[0].astype(BF16)
    iw_t = (iw_ref[...] * (IDX_HEADS ** -0.5 * IDX_DIM ** -0.5)).T
    for h in range(IDX_HEADS):
        iq_ref, hh = (iq_lo_ref, h) if h < IDX_HEADS // 2 else (iq_hi_ref, h - IDX_HEADS // 2)
        sc = _dot_nt(ikb, iq_ref[:, hh * IDX_DIM:(hh + 1) * IDX_DIM].astype(BF16))
        term = iw_t[h:h + 1, :] * jnp.maximum(sc, 0.0)
        if h == 0:
            score_scr[...] = term
        else:
            score_scr[...] += term
    qpos = j * QBLOCK + lax.broadcasted_iota(I32, (1, QBLOCK), 1)
    kpos = lax.broadcasted_iota(I32, (nkeys, 1), 0)
    causal = kpos <= qpos
    score = jnp.where(causal, score_scr[...], -jnp.inf)
    _topk_neg_mask(score, kpos, causal, topk, key_scr, mask_t_scr, key_axis=0)
    mask_scr[...] = mask_t_scr[...].T

    scale = d ** -0.5 * LOG2_E
    win_start = pl.multiple_of(jnp.maximum(j - 1, 0) * QBLOCK, QBLOCK)
    near_start = pl.multiple_of(jnp.where(j == 0, QBLOCK, 0), QBLOCK)

    def group(g, carry):
        lanes = lambda i: pl.ds(pl.multiple_of(i * d, d), d)
        kg = k_ref[0, :, lanes(g)].astype(BF16)
        vg = v_ref[0, :, lanes(g)].astype(BF16)
        def logits_and_max(r):
            h = g * ATT_REP + r
            rows = slice(r * QBLOCK, (r + 1) * QBLOCK)
            lg_scr[rows, :] = _dot_nt((q_ref[:, lanes(h)] * scale).astype(BF16), kg)
            lg_scr[rows, pl.ds(win_start, 2 * QBLOCK)] += near_ref[h, :, pl.ds(near_start, 2 * QBLOCK)]
            return jnp.max(lg_scr[rows, :] + mask_scr[...], axis=-1, keepdims=True)

        def weights(r, m):
            rows = slice(r * QBLOCK, (r + 1) * QBLOCK)
            p = jnp.exp2(lg_scr[rows, :] + mask_scr[...] - m)
            return p.astype(BF16), jnp.sum(p, axis=-1, keepdims=True)

        def values(r, p, l):
            o_ref[:, lanes(g * ATT_REP + r)] = (_dot(p, vg) / l).astype(BF16)

        row_max = [None] * ATT_REP
        probs = [None] * ATT_REP
        for step in range(ATT_REP + 2):
            if step < ATT_REP:
                row_max[step] = logits_and_max(step)
            if 0 <= step - 1 < ATT_REP:
                probs[step - 1] = weights(step - 1, row_max[step - 1])
            if 0 <= step - 2 < ATT_REP:
                values(step - 2, *probs[step - 2])
        return carry

    lax.fori_loop(0, ATT_KV_HEADS, group, 0)


def _dsa_prompt(z, near):
    nq = SEQ // QBLOCK
    z_keys = z.reshape(BATCH, SEQ, ATT_IN_PAD)
    out = jnp.zeros((M_PROMPT, ATT_Q_W), BF16)
    for split in range(PROMPT_SPLITS):
        nkeys = (split + 1) * SPLIT_QBLOCKS * QBLOCK
        qrow = lambda b, j, s=split: b * nq + s * SPLIT_QBLOCKS + j
        out = pl.pallas_call(
            functools.partial(_dsa_prompt_kernel, split=split),
            grid=(BATCH, SPLIT_QBLOCKS),
            in_specs=[pl.BlockSpec((QBLOCK, ATT_Q_W), lambda b, j: (qrow(b, j), ATT_COL_Q // ATT_Q_W)),
                      pl.BlockSpec((QBLOCK, IQ_HALF_W), lambda b, j: (qrow(b, j), ATT_COL_IQ // IQ_HALF_W)),
                      pl.BlockSpec((QBLOCK, IQ_HALF_W), lambda b, j: (qrow(b, j), ATT_COL_IQ // IQ_HALF_W + 1)),
                      pl.BlockSpec((QBLOCK, LANES), lambda b, j: (qrow(b, j), ATT_COL_IW // LANES)),
                      pl.BlockSpec((1, nkeys, ATT_KV_W), lambda b, j: (b, 0, ATT_COL_K // ATT_KV_W)),
                      pl.BlockSpec((1, nkeys, ATT_KV_W), lambda b, j: (b, 0, ATT_COL_V // ATT_KV_W)),
                      pl.BlockSpec((1, nkeys, IDX_DIM), lambda b, j: (b, 0, ATT_COL_IK // IDX_DIM)),
                      pl.BlockSpec((ATT_HEADS, QBLOCK, 3 * QBLOCK), lambda b, j: (0, 0, 0)),
                      pl.BlockSpec(memory_space=pl.ANY)],
            out_specs=pl.BlockSpec((QBLOCK, ATT_Q_W), lambda b, j: (qrow(b, j), 0)),
            out_shape=jax.ShapeDtypeStruct((M_PROMPT, ATT_Q_W), BF16),
            input_output_aliases={8: 0},
            scratch_shapes=[pltpu.VMEM((nkeys, QBLOCK), F32), pltpu.VMEM((nkeys, QBLOCK), F32),
                            pltpu.VMEM((nkeys, QBLOCK), I32),
                            pltpu.VMEM((QBLOCK, nkeys), F32),
                            pltpu.VMEM((ATT_REP * QBLOCK, nkeys), F32)],
            compiler_params=_cparams(("parallel", "arbitrary"), 56),
        )(z, z, z, z, z_keys, z_keys, z_keys, near, out)
    return out


def _kv_rows_kernel(k_ref, v_ref, ko_ref, vo_ref):
    tm = k_ref.shape[0]
    for g in range(ATT_KV_HEADS):
        rows = pl.ds(g, tm, stride=ATT_KV_HEADS)
        cols = slice(g * ATT_HEAD_DIM, (g + 1) * ATT_HEAD_DIM)
        ko_ref[rows, :] = k_ref[:, cols]
        vo_ref[rows, :] = v_ref[:, cols]


def _kv_rows(z, tm=512):
    m = z.shape[0]
    out = jax.ShapeDtypeStruct((m * ATT_KV_HEADS, ATT_HEAD_DIM), F32)
    return pl.pallas_call(
        _kv_rows_kernel,
        grid=(m // tm,),
        in_specs=[pl.BlockSpec((tm, ATT_KV_W), lambda i: (i, ATT_COL_K // ATT_KV_W)),
                  pl.BlockSpec((tm, ATT_KV_W), lambda i: (i, ATT_COL_V // ATT_KV_W))],
        out_specs=[pl.BlockSpec((tm * ATT_KV_HEADS, ATT_HEAD_DIM), lambda i: (i, 0))] * 2,
        out_shape=[out, out],
        compiler_params=_cparams(("parallel",), 16),
    )(z, z)


SELECT_BATCHES = 8


def _sample_select_kernel(pt_ref, iq_ref, iw_ref, iknew_ref, cik_ref, mask_ref,
                          ikbuf, iknew_scr, score_scr, key_scr, mask_scr, sem):
    step = pl.program_id(0)
    nsteps = pl.num_programs(0)
    slot = lax.rem(step, 2)
    t = DEC_SEQ
    topk = min(TOPK_MAX, (PAST_LEN + DEC_SEQ) // 4)

    def page_copies(step_, slot_):
        copies = []
        for bb in range(SELECT_BATCHES):
            for p in range(N_PAGES):
                page = pt_ref[(step_ * SELECT_BATCHES + bb) * N_PAGES + p]
                dst = pl.ds(p * PAGE_SIZE, PAGE_SIZE)
                copies.append(pltpu.make_async_copy(cik_ref.at[page], ikbuf.at[slot_, bb, dst], sem.at[slot_]))
        return copies

    @pl.when(step == 0)
    def _():
        iknew_scr[...] = jnp.zeros_like(iknew_scr)
        for cp in page_copies(0, 0):
            cp.start()

    @pl.when(step + 1 < nsteps)
    def _():
        for cp in page_copies(step + 1, 1 - slot):
            cp.start()

    for cp in page_copies(step, slot):
        cp.wait()

    for bb in range(SELECT_BATCHES):
        iknew_scr[bb, 0:t, :] = iknew_ref[bb]
        iq = iq_ref[bb].astype(BF16)
        sc = jnp.concatenate([_dot_nt(iq, ikbuf[slot, bb].astype(BF16)),
                              _dot_nt(iq, iknew_scr[bb].astype(BF16))], axis=-1)
        ws = (iw_ref[bb] * (IDX_HEADS ** -0.5 * IDX_DIM ** -0.5)) * jnp.maximum(sc, 0.0)
        for i in range(t):
            score_scr[bb * t + i:bb * t + i + 1, :] = jnp.sum(
                ws[i * IDX_HEADS:(i + 1) * IDX_HEADS], axis=0, keepdims=True)

    rows = SELECT_BATCHES * t
    assert t & (t - 1) == 0
    tq = lax.broadcasted_iota(I32, (rows, 1), 0) & (t - 1)
    kpos = lax.broadcasted_iota(I32, (1, SAMPLE_KEYS), 1)
    causal = kpos <= PAST_LEN + tq
    score = jnp.where(causal, score_scr[...], -jnp.inf)
    _topk_neg_mask(score, kpos, causal, topk, key_scr, mask_scr)
    for bb in range(SELECT_BATCHES):
        mask_ref[bb] = mask_scr[bb * t:(bb + 1) * t, :]


def _sample_select(iq, iw, iknew, page_table, cik):
    rows = SELECT_BATCHES * DEC_SEQ
    per_step = lambda *shape: pl.BlockSpec((SELECT_BATCHES,) + shape, lambda s, pt: (s,) + (0,) * len(shape))
    grid_spec = pltpu.PrefetchScalarGridSpec(
        num_scalar_prefetch=1,
        grid=(DEC_BATCH // SELECT_BATCHES,),
        in_specs=[per_step(DEC_SEQ * IDX_HEADS, IDX_DIM),
                  per_step(DEC_SEQ * IDX_HEADS, 1),
                  per_step(DEC_SEQ, IDX_DIM),
                  pl.BlockSpec(memory_space=pl.ANY)],
        out_specs=per_step(DEC_SEQ, SAMPLE_KEYS),
        scratch_shapes=[pltpu.VMEM((2, SELECT_BATCHES, PAST_LEN, IDX_DIM), F32),
                        pltpu.VMEM((SELECT_BATCHES, LANES, IDX_DIM), F32),
                        pltpu.VMEM((rows, SAMPLE_KEYS), F32),
                        pltpu.VMEM((rows, SAMPLE_KEYS), I32),
                        pltpu.VMEM((rows, SAMPLE_KEYS), F32),
                        pltpu.SemaphoreType.DMA((2,))])
    return pl.pallas_call(
        _sample_select_kernel,
        grid_spec=grid_spec,
        out_shape=jax.ShapeDtypeStruct((DEC_BATCH, DEC_SEQ, SAMPLE_KEYS), F32),
        compiler_params=_cparams(("arbitrary",), 40),
    )(page_table.reshape(-1), iq, iw, iknew, cik)


def _dsa_sample_kernel(pt_ref, q_ref, mask_ref, knew_ref, vnew_ref, bias_ref,
                       ck_ref, cv_ref, o_ref, kbuf, vbuf, knew_scr, vnew_scr, sem):
    b = pl.program_id(0)
    nb = pl.num_programs(0)
    slot = lax.rem(b, 2)
    t = DEC_SEQ
    d = ATT_HEAD_DIM
    rows16 = ATT_REP * t

    def page_copies(batch, slot_):
        copies = []
        for p in range(N_PAGES):
            page = pt_ref[batch * N_PAGES + p]
            kv_rows = PAGE_SIZE * ATT_KV_HEADS
            kv_src = pl.ds(pl.multiple_of(page * kv_rows, kv_rows), kv_rows)
            kv_dst = pl.ds(p * kv_rows, kv_rows)
            copies.append(pltpu.make_async_copy(ck_ref.at[kv_src], kbuf.at[slot_, kv_dst], sem.at[0, slot_]))
            copies.append(pltpu.make_async_copy(cv_ref.at[kv_src], vbuf.at[slot_, kv_dst], sem.at[1, slot_]))
        return copies

    def kv_head(buf, g):
        return buf[slot, pl.ds(g, PAST_LEN, stride=ATT_KV_HEADS), :]

    @pl.when(b == 0)
    def _():
        knew_scr[...] = jnp.zeros_like(knew_scr)
        vnew_scr[...] = jnp.zeros_like(vnew_scr)
        for cp in page_copies(0, 0):
            cp.start()

    @pl.when(b + 1 < nb)
    def _():
        for cp in page_copies(b + 1, 1 - slot):
            cp.start()

    knew_scr[0:t, :] = knew_ref[0]
    vnew_scr[0:t, :] = vnew_ref[0]

    assert t & (t - 1) == 0
    tq = lax.broadcasted_iota(I32, (rows16, 1), 0) & (t - 1)
    neg_mask = jnp.zeros((rows16, SAMPLE_KEYS), F32)
    for i in range(t):
        neg_mask = jnp.where(tq == i, mask_ref[0, i:i + 1, :], neg_mask)

    for cp in page_copies(b, slot):
        cp.wait()

    scale = d ** -0.5
    for g in range(ATT_KV_HEADS):
        cols = slice(g * d, (g + 1) * d)
        qg = q_ref[0, g].astype(BF16)
        logits = jnp.concatenate([_dot_nt(qg, kv_head(kbuf, g).astype(BF16)),
                                  _dot_nt(qg, knew_scr[:, cols].astype(BF16))], axis=-1)
        x = logits * scale + bias_ref[g] + neg_mask
        p = jnp.exp(x - jnp.max(x, axis=-1, keepdims=True))
        l = jnp.sum(p, axis=-1, keepdims=True)
        pb = p.astype(BF16)
        out = (_dot(pb[:, :PAST_LEN], kv_head(vbuf, g).astype(BF16))
               + _dot(pb[:, PAST_LEN:], vnew_scr[:, cols].astype(BF16))) / l
        for r in range(ATT_REP):
            h = g * ATT_REP + r
            o_ref[0, :, h * d:(h + 1) * d] = out[r * t:(r + 1) * t]


def _dsa_sample(q, mask, knew, vnew, bias, page_table, ck, cv):
    rows16 = ATT_REP * DEC_SEQ
    whole = lambda *shape: pl.BlockSpec(shape, lambda b, pt: (0,) * len(shape))
    per_b = lambda *shape: pl.BlockSpec((1,) + shape, lambda b, pt: (b,) + (0,) * len(shape))
    grid_spec = pltpu.PrefetchScalarGridSpec(
        num_scalar_prefetch=1,
        grid=(DEC_BATCH,),
        in_specs=[per_b(ATT_KV_HEADS, rows16, ATT_HEAD_DIM),
                  per_b(DEC_SEQ, SAMPLE_KEYS),
                  per_b(DEC_SEQ, ATT_KV_W),
                  per_b(DEC_SEQ, ATT_KV_W),
                  whole(ATT_KV_HEADS, rows16, SAMPLE_KEYS),
                  pl.BlockSpec(memory_space=pl.ANY),
                  pl.BlockSpec(memory_space=pl.ANY)],
        out_specs=per_b(DEC_SEQ, ATT_Q_W),
        scratch_shapes=[pltpu.VMEM((2, PAST_LEN * ATT_KV_HEADS, ATT_HEAD_DIM), F32),
                        pltpu.VMEM((2, PAST_LEN * ATT_KV_HEADS, ATT_HEAD_DIM), F32),
                        pltpu.VMEM((LANES, ATT_KV_W), F32),
                        pltpu.VMEM((LANES, ATT_KV_W), F32),
                        pltpu.SemaphoreType.DMA((2, 2))])
    return pl.pallas_call(
        _dsa_sample_kernel,
        grid_spec=grid_spec,
        out_shape=jax.ShapeDtypeStruct((DEC_BATCH, DEC_SEQ, ATT_Q_W), F32),
        compiler_params=_cparams(("arbitrary",), 40),
    )(page_table.reshape(-1), q, mask, knew, vnew, bias, ck, cv)


def _rope_tables(pos):
    half = RET_DK // 2
    freqs = ROPE_BASE ** (-jnp.arange(half, dtype=F32) / half)
    ang = pos.astype(F32)[:, None] * freqs[None, :]
    return jnp.cos(ang), jnp.sin(ang)


def _pad_att_in(w):
    assert w.shape[1] == ATT_IN_WIDTH
    return jnp.pad(w, ((0, 0), (0, ATT_IN_PAD - ATT_IN_WIDTH)))


def _row_tile(m):
    return min(m, 1024)


def kernel(x_prompt, x_sample, p_prompt, p_sample, state_ret, cache_k, cache_v, cache_idx_k, page_table,
           norm_mix, norm_ffn, norm_ple, norm_final, w_ret_in, ret_gn_gain, w_ret_out,
           w_att_in, w_att_out, rel_bias, w_ffn_in, w_ffn_out, w_ple_gate, w_ple_proj):
    bf = lambda w: w.astype(BF16)
    hp = x_prompt.reshape(M_PROMPT, D_MODEL)
    hs = x_sample.reshape(M_SAMPLE, D_MODEL)
    pp = p_prompt.reshape(DEPTH, M_PROMPT, PLE_DIM)
    ps = p_sample.reshape(DEPTH, M_SAMPLE, PLE_DIM)

    def mixer_in(h, g, w, tn):
        return _matmul(_rmsnorm(h, g, BF16), w, tm=_row_tile(h.shape[0]), tn=tn)

    def mixer_out(y, w, h, tm):
        return _matmul(y, w, h, tm=min(tm, h.shape[0]), tn=1024)

    w_ffn_in_b, w_ffn_out_b = bf(w_ffn_in), bf(w_ffn_out)
    w_ple_gate_b, w_ple_proj_b = bf(w_ple_gate), bf(w_ple_proj)

    def tail(h, p, i, final_norm):
        h = _ffn(h, norm_ffn[i], w_ffn_in_b, w_ffn_out_b, i)
        return _ple(h, norm_ple[i], w_ple_gate_b, p, i, w_ple_proj_b, norm_final, final_norm)

    lg = jnp.log1p(-jnp.exp2(-5.0 - jnp.arange(RET_HEADS, dtype=F32)))
    cos_p, sin_p = _rope_tables(jnp.arange(SEQ, dtype=I32))
    cos_s, sin_s = _rope_tables(PAST_LEN + jnp.arange(DEC_SEQ, dtype=I32))
    w_in, w_out = bf(w_ret_in[0]), bf(w_ret_out[0])
    y, ret_state_p = _ret_prompt(mixer_in(hp, norm_mix[0], w_in, 2048), lg, cos_p, sin_p, ret_gn_gain[0])
    zs = mixer_in(hs, norm_mix[0], w_in, 2048).reshape(DEC_BATCH, DEC_SEQ, RET_IN_WIDTH)
    y_s, ret_state_s = _ret_sample(zs, lg, cos_s, sin_s, ret_gn_gain[0], state_ret)
    hp = tail(mixer_out(y, w_out, hp, 512), pp, 0, False)
    hs = tail(mixer_out(bf(y_s.reshape(M_SAMPLE, RET_V_W)), w_out, hs, 512), ps, 0, False)

    near, samp = _bias_tables(rel_bias)
    w_in, w_out = bf(_pad_att_in(w_att_in[0])), bf(w_att_out[0])
    zp = mixer_in(hp, norm_mix[1], w_in, 1792)
    a = _dsa_prompt(zp, near)
    zs = mixer_in(hs, norm_mix[1], w_in, 1792).reshape(DEC_BATCH, DEC_SEQ, ATT_IN_PAD)
    q_s = zs[..., :ATT_Q_W].reshape(DEC_BATCH, DEC_SEQ, ATT_KV_HEADS, ATT_REP, ATT_HEAD_DIM)
    q_s = q_s.transpose(0, 2, 3, 1, 4).reshape(DEC_BATCH, ATT_KV_HEADS, ATT_REP * DEC_SEQ, ATT_HEAD_DIM)
    bias_s = samp[:, :DEC_SEQ].reshape(ATT_KV_HEADS, ATT_REP * DEC_SEQ, SAMPLE_KEYS)
    iq_s = zs[..., ATT_COL_IQ:ATT_COL_IQ + IDX_Q_W].reshape(DEC_BATCH, DEC_SEQ * IDX_HEADS, IDX_DIM)
    iw_s = zs[..., ATT_COL_IW:ATT_COL_IW + IDX_HEADS].reshape(DEC_BATCH, DEC_SEQ * IDX_HEADS, 1)
    k_s = zs[..., ATT_COL_K:ATT_COL_K + ATT_KV_W]
    v_s = zs[..., ATT_COL_V:ATT_COL_V + ATT_KV_W]
    ik_s = zs[..., ATT_COL_IK:ATT_COL_IK + IDX_DIM]
    n_phys = cache_k.shape[1]
    mask_s = _sample_select(iq_s, iw_s, ik_s, page_table, cache_idx_k[0])
    a_s = _dsa_sample(q_s, mask_s, k_s, v_s, bias_s, page_table,
                      cache_k[0].reshape(n_phys * PAGE_SIZE * ATT_KV_HEADS, ATT_HEAD_DIM),
                      cache_v[0].reshape(n_phys * PAGE_SIZE * ATT_KV_HEADS, ATT_HEAD_DIM))
    yp = tail(mixer_out(a, w_out, hp, 1024), pp, 1, True)
    ys = tail(mixer_out(bf(a_s.reshape(M_SAMPLE, ATT_Q_W)), w_out, hs, 1024), ps, 1, True)

    kv_p = lambda x: x.reshape(1, BATCH, SEQ, ATT_KV_HEADS, ATT_HEAD_DIM)
    kv_s = lambda x: x.reshape(1, DEC_BATCH, DEC_SEQ, ATT_KV_HEADS, ATT_HEAD_DIM)
    k_p, v_p = _kv_rows(zp)
    return (yp.reshape(BATCH, SEQ, D_MODEL),
            ys.reshape(DEC_BATCH, DEC_SEQ, D_MODEL),
            ret_state_p[None],
            ret_state_s,
            kv_p(k_p), kv_p(v_p),
            zp[:, ATT_COL_IK:ATT_COL_IK + IDX_DIM].reshape(1, BATCH, SEQ, IDX_DIM),
            kv_s(k_s), kv_s(v_s),
            ik_s.reshape(1, DEC_BATCH, DEC_SEQ, IDX_DIM))
```

```python
import functools
import math

import jax
import jax.numpy as jnp
from jax import lax
from jax.experimental import pallas as pl
from jax.experimental.pallas import tpu as pltpu

F32 = jnp.float32
BF16 = jnp.bfloat16
I32 = jnp.int32

D_MODEL = 2048
BATCH = 4
SEQ = 2048
DEPTH = 2
DEC_BATCH = 128
DEC_SEQ = 4
PAST_LEN = 2048
PAGE_SIZE = 128
N_PAGES = PAST_LEN // PAGE_SIZE

RET_HEADS = 8
RET_DK = D_MODEL // RET_HEADS
RET_DV = 2 * D_MODEL // RET_HEADS
RET_CHUNK = 128
ROPE_BASE = 10000.0
ATT_HEADS = 16
ATT_HEAD_DIM = D_MODEL // ATT_HEADS
ATT_KV_HEADS = 4
ATT_REP = ATT_HEADS // ATT_KV_HEADS
IDX_HEADS = 16
IDX_DIM = 128
TOPK_MAX = 256
REL_BUCKETS = 32
REL_MAX_DIST = 128
FFN_HIDDEN = 5632
PLE_DIM = 256
EPS = 1e-6

RET_Q_W = RET_HEADS * RET_DK
RET_V_W = RET_HEADS * RET_DV
RET_IN_WIDTH = 2 * RET_Q_W + 2 * RET_V_W
ATT_Q_W = ATT_HEADS * ATT_HEAD_DIM
ATT_KV_W = ATT_KV_HEADS * ATT_HEAD_DIM
IDX_Q_W = IDX_HEADS * IDX_DIM

M_PROMPT = BATCH * SEQ
M_SAMPLE = DEC_BATCH * DEC_SEQ

LANES = 128
SUBLANES = 8
V7X_VMEM_BYTES = 64 * 1024 * 1024
MIB = 1024 * 1024

ATT_COL_Q = 0
ATT_COL_K = ATT_Q_W
ATT_COL_V = ATT_COL_K + ATT_KV_W
ATT_COL_IQ = ATT_COL_V + ATT_KV_W
ATT_COL_IK = ATT_COL_IQ + IDX_Q_W
ATT_COL_IW = ATT_COL_IK + IDX_DIM
ATT_IN_WIDTH = ATT_COL_IW + IDX_HEADS
ATT_IN_PAD = ATT_COL_IW + LANES
IQ_HALF_W = IDX_Q_W // 2
assert ATT_COL_IQ % IQ_HALF_W == 0

QBLOCK = 128
SAMPLE_KEYS = PAST_LEN + LANES
LOG2_E = math.log2(math.e)
INT_MIN = -(2 ** 31)
NEG_INF_KEY = -2139095041


def _cparams(semantics, vmem_mib):
    assert vmem_mib * MIB < V7X_VMEM_BYTES
    return pltpu.CompilerParams(dimension_semantics=semantics,
                                vmem_limit_bytes=vmem_mib * MIB)


def _dot(a, b):
    return jnp.dot(a, b, preferred_element_type=F32)


def _dot_nt(a, b):
    return lax.dot_general(a, b, (((1,), (1,)), ((), ())), preferred_element_type=F32)


def _rms(x, g):
    return x * lax.rsqrt(jnp.mean(x * x, axis=-1, keepdims=True) + EPS) * g


def _rmsnorm_kernel(x_ref, g_ref, o_ref):
    o_ref[...] = _rms(x_ref[...], g_ref[...]).astype(o_ref.dtype)


def _rmsnorm(x, g, out_dtype, tm=512):
    m, d = x.shape
    return pl.pallas_call(
        _rmsnorm_kernel,
        grid=(m // tm,),
        in_specs=[pl.BlockSpec((tm, d), lambda i: (i, 0)),
                  pl.BlockSpec((1, d), lambda i: (0, 0))],
        out_specs=pl.BlockSpec((tm, d), lambda i: (i, 0)),
        out_shape=jax.ShapeDtypeStruct((m, d), out_dtype),
        compiler_params=_cparams(("parallel",), 32),
    )(x, g.reshape(1, d))


def _mm_kernel(x_ref, w_ref, o_ref):
    o_ref[...] = _dot(x_ref[...], w_ref[...]).astype(o_ref.dtype)


def _mm_res_kernel(x_ref, w_ref, r_ref, o_ref):
    o_ref[...] = r_ref[...] + _dot(x_ref[...], w_ref[...])


def _matmul(x, w, res=None, *, tm, tn, out_dtype=F32, vmem_mib=48):
    m, k = x.shape
    n = w.shape[1]
    assert m % tm == 0 and n % tn == 0
    in_specs = [pl.BlockSpec((tm, k), lambda j, i: (i, 0)),
                pl.BlockSpec((k, tn), lambda j, i: (0, j))]
    args = [x, w]
    kern = _mm_kernel
    if res is not None:
        in_specs.append(pl.BlockSpec((tm, tn), lambda j, i: (i, j)))
        args.append(res)
        kern = _mm_res_kernel
    return pl.pallas_call(
        kern,
        grid=(n // tn, m // tm),
        in_specs=in_specs,
        out_specs=pl.BlockSpec((tm, tn), lambda j, i: (i, j)),
        out_shape=jax.ShapeDtypeStruct((m, n), out_dtype),
        compiler_params=_cparams(("parallel", "parallel"), vmem_mib),
    )(*args)


def _ffn_kernel(x_ref, g_ref, w1_ref, w2_ref, wo_ref, o_ref, xn_ref):
    @pl.when(pl.program_id(1) == 0)
    def _():
        x = x_ref[...]
        xn_ref[...] = _rms(x, g_ref[...]).astype(BF16)
        o_ref[...] = x

    xn = xn_ref[...]
    a = _dot(xn, w1_ref[0])
    b = _dot(xn, w2_ref[0])
    hid = (a * jax.nn.sigmoid(a) * b).astype(BF16)
    o_ref[...] += _dot(hid, wo_ref[0])


def _ffn(h, g, w_in, w_out, layer, tm=512, th=512):
    m, d = h.shape
    nh = FFN_HIDDEN // th
    return pl.pallas_call(
        _ffn_kernel,
        grid=(m // tm, nh),
        in_specs=[pl.BlockSpec((tm, d), lambda i, j: (i, 0)),
                  pl.BlockSpec((1, d), lambda i, j: (0, 0)),
                  pl.BlockSpec((1, d, th), lambda i, j: (layer, 0, j)),
                  pl.BlockSpec((1, d, th), lambda i, j: (layer, 0, j + nh)),
                  pl.BlockSpec((1, th, d), lambda i, j: (layer, j, 0))],
        out_specs=pl.BlockSpec((tm, d), lambda i, j: (i, 0)),
        out_shape=jax.ShapeDtypeStruct((m, d), F32),
        scratch_shapes=[pltpu.VMEM((tm, d), BF16)],
        compiler_params=_cparams(("parallel", "arbitrary"), 56),
    )(h, g.reshape(1, d), w_in, w_in, w_out)


def _ple_kernel(x_ref, g_ref, wg_ref, p_ref, wp_ref, gn_ref, *o_refs, last):
    x = x_ref[...]
    xn = _rms(x, g_ref[...]).astype(BF16)
    gate = jax.nn.sigmoid(_dot(xn, wg_ref[0]))
    y = x + gate * _dot(p_ref[0].astype(BF16), wp_ref[0])
    if last:
        o_refs[0][...] = _rms(y, gn_ref[...])
    else:
        o_refs[0][...] = y
        o_refs[1][...] = _rms(y, gn_ref[...]).astype(BF16)


def _ple(h, g, w_gate, p, layer, w_proj, g_next, last, tm=512):
    m, d = h.shape
    pd = p.shape[2]
    row_spec = pl.BlockSpec((tm, d), lambda i: (i, 0))
    stream = jax.ShapeDtypeStruct((m, d), F32)
    return pl.pallas_call(
        functools.partial(_ple_kernel, last=last),
        grid=(m // tm,),
        in_specs=[row_spec,
                  pl.BlockSpec((1, d), lambda i: (0, 0)),
                  pl.BlockSpec((1, d, d), lambda i: (layer, 0, 0)),
                  pl.BlockSpec((1, tm, pd), lambda i: (layer, i, 0)),
                  pl.BlockSpec((1, pd, d), lambda i: (layer, 0, 0)),
                  pl.BlockSpec((1, d), lambda i: (0, 0))],
        out_specs=row_spec if last else [row_spec, row_spec],
        out_shape=stream if last else [stream, jax.ShapeDtypeStruct((m, d), BF16)],
        compiler_params=_cparams(("parallel",), 48),
    )(h, g.reshape(1, d), w_gate, p, w_proj, g_next.reshape(1, d))


def _group_norm_gate(o, gate, gain):
    mu = jnp.mean(o, axis=-1, keepdims=True)
    var = jnp.mean(jnp.square(o - mu), axis=-1, keepdims=True)
    on = (o - mu) * lax.rsqrt(var + EPS)
    return gate * jax.nn.sigmoid(gate) * (on * gain)


RET_CHUNKS_PER_STEP = 8
RET_ROWS = RET_CHUNKS_PER_STEP * RET_CHUNK


def _ret_prompt_kernel(lg_ref, q_ref, k_ref, v_ref, g_ref, cos_ref, sin_ref, gain_ref,
                       y_ref, s_out_ref, s_scr):
    c = pl.program_id(2)
    lg = lg_ref[pl.program_id(1)]
    half = RET_DK // 2

    @pl.when(c == 0)
    def _():
        s_scr[...] = jnp.zeros_like(s_scr)

    n_col = lax.broadcasted_iota(I32, (RET_CHUNK, 1), 0).astype(F32)
    n_row = lax.broadcasted_iota(I32, (1, RET_CHUNK), 1).astype(F32)
    diff = n_col - n_row
    decay = jnp.where(diff >= 0, jnp.exp(lg * jnp.maximum(diff, 0.0)), 0.0)
    q_dec = jnp.exp(lg * (n_col + 1.0))
    k_dec = jnp.exp(lg * (RET_CHUNK - 1.0 - n_col))
    s_dec = jnp.exp(jnp.full((1, 1), lg * RET_CHUNK, F32))
    gain = gain_ref[...]

    def rotate(x, cos, sin):
        x1, x2 = x[:, :half], x[:, half:]
        return jnp.concatenate([x1 * cos - x2 * sin, x1 * sin + x2 * cos], axis=-1)

    for i in range(RET_CHUNKS_PER_STEP):
        rows = pl.ds(i * RET_CHUNK, RET_CHUNK)
        cos, sin = cos_ref[rows, :], sin_ref[rows, :]
        qr = rotate(q_ref[rows, :], cos, sin)
        kr = rotate(k_ref[rows, :], cos, sin) * (RET_DK ** -0.5)
        v = v_ref[rows, :].astype(BF16)
        s = s_scr[...]
        qb = qr.astype(BF16)
        scores = _dot_nt(qb, kr.astype(BF16)) * decay
        o = _dot(scores.astype(BF16), v) + _dot(qb, s.astype(BF16)) * q_dec
        kd_t = (kr * k_dec).T.astype(BF16)
        s_scr[...] = s_dec * s + _dot(kd_t, v)
        y_ref[rows, :] = _group_norm_gate(o, g_ref[rows, :], gain).astype(BF16)

    @pl.when(c == pl.num_programs(2) - 1)
    def _():
        s_out_ref[0, 0] = s_scr[...]


def _ret_prompt(z, lg, cos, sin, gain):
    steps = SEQ // RET_ROWS
    qb, vb = RET_Q_W // RET_DK, (2 * RET_Q_W) // RET_DV
    row = lambda b, h, c: b * steps + c
    return pl.pallas_call(
        _ret_prompt_kernel,
        grid=(BATCH, RET_HEADS, steps),
        in_specs=[pl.BlockSpec(memory_space=pltpu.SMEM),
                  pl.BlockSpec((RET_ROWS, RET_DK), lambda b, h, c: (row(b, h, c), h)),
                  pl.BlockSpec((RET_ROWS, RET_DK), lambda b, h, c: (row(b, h, c), qb + h)),
                  pl.BlockSpec((RET_ROWS, RET_DV), lambda b, h, c: (row(b, h, c), vb + h)),
                  pl.BlockSpec((RET_ROWS, RET_DV), lambda b, h, c: (row(b, h, c), vb + RET_HEADS + h)),
                  pl.BlockSpec((RET_ROWS, RET_DK // 2), lambda b, h, c: (c, 0)),
                  pl.BlockSpec((RET_ROWS, RET_DK // 2), lambda b, h, c: (c, 0)),
                  pl.BlockSpec((1, RET_DV), lambda b, h, c: (0, h))],
        out_specs=[pl.BlockSpec((RET_ROWS, RET_DV), lambda b, h, c: (row(b, h, c), h)),
                   pl.BlockSpec((1, 1, RET_DK, RET_DV), lambda b, h, c: (b, h, 0, 0))],
        out_shape=[jax.ShapeDtypeStruct((M_PROMPT, RET_V_W), BF16),
                   jax.ShapeDtypeStruct((BATCH, RET_HEADS, RET_DK, RET_DV), F32)],
        scratch_shapes=[pltpu.VMEM((RET_DK, RET_DV), F32)],
        compiler_params=_cparams(("parallel", "parallel", "arbitrary"), 32),
    )(lg, z, z, z, z, cos, sin, gain.reshape(1, RET_V_W))


def _ret_sample_kernel(lg_ref, zs_ref, cos_ref, sin_ref, gain_ref, s_ref, y_ref, s_out_ref,
                       q_pad, k_pad, kd_pad, v_pad):
    half = RET_DK // 2
    t = DEC_SEQ
    cos, sin = cos_ref[...], sin_ref[...]
    n_col = lax.broadcasted_iota(I32, (t, 1), 0).astype(F32)
    pad_rows = q_pad.shape[0]
    qi = lax.broadcasted_iota(I32, (pad_rows, LANES), 0)
    kj = lax.broadcasted_iota(I32, (pad_rows, LANES), 1)
    live = (qi >= kj) & (qi < t)
    diff = jnp.maximum(qi - kj, 0).astype(F32)

    def rotate(x):
        x1, x2 = x[:, :half], x[:, half:]
        return jnp.concatenate([x1 * cos - x2 * sin, x1 * sin + x2 * cos], axis=-1)

    q_pad[...] = jnp.zeros_like(q_pad)
    k_pad[...] = jnp.zeros_like(k_pad)
    kd_pad[...] = jnp.zeros_like(kd_pad)
    v_pad[...] = jnp.zeros_like(v_pad)

    for h in range(RET_HEADS):
        lg = lg_ref[h]
        kr = rotate(zs_ref[0, :, RET_Q_W + h * RET_DK:RET_Q_W + (h + 1) * RET_DK]) * (RET_DK ** -0.5)
        k_pad[0:t, :] = kr
        kd_pad[0:t, :] = kr * jnp.exp(lg * (t - 1.0 - n_col))
        q_pad[0:t, :] = rotate(zs_ref[0, :, h * RET_DK:(h + 1) * RET_DK])
        v_pad[0:t, :] = zs_ref[0, :, pl.ds(2 * RET_Q_W + h * RET_DV, RET_DV)]
        gate = zs_ref[0, :, pl.ds(2 * RET_Q_W + RET_V_W + h * RET_DV, RET_DV)]
        s = s_ref[0, 0, h]

        qb = q_pad[...].astype(BF16)
        vb = v_pad[...].astype(BF16)
        scores = _dot_nt(qb, k_pad[...].astype(BF16)) * jnp.where(live, jnp.exp(lg * diff), 0.0)
        o = _dot(scores.astype(BF16), vb)[0:t] + _dot(qb, s.astype(BF16))[0:t] * jnp.exp(lg * (n_col + 1.0))
        s_out_ref[0, 0, h] = (jnp.exp(jnp.full((1, 1), lg * t, F32)) * s
                              + _dot(kd_pad[...].T.astype(BF16), vb))
        y_ref[0, :, pl.ds(h * RET_DV, RET_DV)] = _group_norm_gate(
            o, gate, gain_ref[:, pl.ds(h * RET_DV, RET_DV)])


def _ret_sample(zs, lg, cos, sin, gain, state):
    half = RET_DK // 2
    whole = lambda *shape: pl.BlockSpec(shape, lambda b: (0,) * len(shape))
    return pl.pallas_call(
        _ret_sample_kernel,
        grid=(DEC_BATCH,),
        in_specs=[pl.BlockSpec(memory_space=pltpu.SMEM),
                  pl.BlockSpec((1, DEC_SEQ, RET_IN_WIDTH), lambda b: (b, 0, 0)),
                  whole(DEC_SEQ, half), whole(DEC_SEQ, half),
                  whole(1, RET_V_W),
                  pl.BlockSpec((1, 1, RET_HEADS, RET_DK, RET_DV), lambda b: (0, b, 0, 0, 0))],
        out_specs=[pl.BlockSpec((1, DEC_SEQ, RET_V_W), lambda b: (b, 0, 0)),
                   pl.BlockSpec((1, 1, RET_HEADS, RET_DK, RET_DV), lambda b: (0, b, 0, 0, 0))],
        out_shape=[jax.ShapeDtypeStruct((DEC_BATCH, DEC_SEQ, RET_V_W), F32),
                   jax.ShapeDtypeStruct(state.shape, F32)],
        scratch_shapes=[pltpu.VMEM((2 * SUBLANES, RET_DK), F32),
                        pltpu.VMEM((LANES, RET_DK), F32),
                        pltpu.VMEM((LANES, RET_DK), F32),
                        pltpu.VMEM((LANES, RET_DV), F32)],
        compiler_params=_cparams(("parallel",), 40),
    )(lg, zs, cos, sin, gain.reshape(1, RET_V_W), state)


def _rel_bucket(dist):
    n = jnp.maximum(dist, 0)
    max_exact = REL_BUCKETS // 2
    nf = jnp.maximum(n, max_exact).astype(F32)
    large = max_exact + (jnp.log(nf / max_exact) / math.log(REL_MAX_DIST / max_exact)
                         * (REL_BUCKETS - max_exact)).astype(I32)
    large = jnp.minimum(large, REL_BUCKETS - 1)
    return jnp.where(n < max_exact, n, large)


def _bias_kernel(table_ref, near_ref, samp_ref):
    h = pl.program_id(0)

    def lookup(dist):
        bucket = _rel_bucket(dist)
        acc = jnp.zeros(dist.shape, F32)
        for b in range(REL_BUCKETS):
            acc = jnp.where(bucket == b, table_ref[b, h], acc)
        return acc

    far = table_ref[REL_BUCKETS - 1, h]
    i = lax.broadcasted_iota(I32, (QBLOCK, 2 * QBLOCK), 0)
    j = lax.broadcasted_iota(I32, (QBLOCK, 2 * QBLOCK), 1)
    near = (lookup(QBLOCK + i - j) - far) * LOG2_E
    near_ref[0, :, 0:2 * QBLOCK] = near
    near_ref[0, :, 2 * QBLOCK:] = near[:, 0:QBLOCK]
    t = lax.broadcasted_iota(I32, (SUBLANES, SAMPLE_KEYS), 0)
    s = lax.broadcasted_iota(I32, (SUBLANES, SAMPLE_KEYS), 1)
    samp_ref[0] = lookup(PAST_LEN + t - s)


def _bias_tables(rel_bias):
    return pl.pallas_call(
        _bias_kernel,
        grid=(ATT_HEADS,),
        in_specs=[pl.BlockSpec(memory_space=pltpu.SMEM)],
        out_specs=[pl.BlockSpec((1, QBLOCK, 3 * QBLOCK), lambda h: (h, 0, 0)),
                   pl.BlockSpec((1, SUBLANES, SAMPLE_KEYS), lambda h: (h, 0, 0))],
        out_shape=[jax.ShapeDtypeStruct((ATT_HEADS, QBLOCK, 3 * QBLOCK), F32),
                   jax.ShapeDtypeStruct((ATT_HEADS, SUBLANES, SAMPLE_KEYS), F32)],
        compiler_params=_cparams(("parallel",), 16),
    )(rel_bias)


def _order_key(score):
    bits = lax.bitcast_convert_type(score + 0.0, I32)
    return bits ^ ((bits >> 31) & 0x7FFFFFFF)


def _topk_neg_mask(score, kpos, causal, topk, key_ref, mask_ref, *, key_axis=1):
    keys = score.shape[key_axis]
    key_ref[...] = _order_key(score)
    kf = float(topk)

    def count(mask):
        ind = jnp.where(mask, 1.0, 0.0)
        slab = 8 * SUBLANES
        if key_axis == 0 and keys % slab == 0 and keys > slab:
            ind = jnp.sum(ind.reshape(keys // slab, slab, ind.shape[1]), axis=0)
        return jnp.sum(ind, axis=key_axis, keepdims=True)

    prefix = jnp.where(count(key_ref[...] >= 0) >= kf, 0, INT_MIN).astype(I32)

    def thr_body(i, prefix):
        cand = prefix | jnp.left_shift(1, 30 - i)
        return jnp.where(count(key_ref[...] >= cand) >= kf, cand, prefix)

    thr = lax.fori_loop(0, 31, thr_body, prefix)
    ge = key_ref[...] >= thr
    mask_ref[...] = jnp.where(ge & causal, 0.0, -jnp.inf)

    tied = (count(ge) > kf) & (thr > NEG_INF_KEY)

    @pl.when(jnp.max(jnp.where(tied, 1.0, 0.0)) > 0.0)
    def _():
        need = kf - count(key_ref[...] > thr)
        nbits = (keys - 1).bit_length()

        def tie_body(i, last):
            cand = last | jnp.left_shift(1, nbits - 1 - i)
            before = (key_ref[...] == thr) & (kpos < cand)
            return jnp.where(count(before) < need, cand, last)

        last = lax.fori_loop(0, nbits, tie_body, jnp.zeros_like(thr))
        key = key_ref[...]
        keep = ((key > thr) | ((key == thr) & (kpos <= last))) & causal
        mask_ref[...] = jnp.where(keep, 0.0, -jnp.inf)


PROMPT_SPLITS = 8
SPLIT_QBLOCKS = SEQ // QBLOCK // PROMPT_SPLITS


def _dsa_prompt_kernel(q_ref, iq_lo_ref, iq_hi_ref, iw_ref, k_ref, v_ref, ik_ref, near_ref, prev_ref, o_ref,
                       score_scr, mask_t_scr, key_scr, mask_scr, lg_scr, *, split):
    del prev_ref
    nkeys = (split + 1) * SPLIT_QBLOCKS * QBLOCK
    j = split * SPLIT_QBLOCKS + pl.program_id(1)
    topk = min(TOPK_MAX, SEQ // 4)
    d = ATT_HEAD_DIM

    iw_t = (iw_ref[...] * (IDX_HEADS ** -0.5 * IDX_DIM ** -0.5)).T
    iq = [ref[:, hh * IDX_DIM:(hh + 1) * IDX_DIM].astype(BF16)
          for ref in (iq_lo_ref, iq_hi_ref) for hh in range(IDX_HEADS // 2)]
    slab = 2 * QBLOCK
    for s in range(nkeys // slab):
        rows = slice(s * slab, (s + 1) * slab)
        ikb = ik_ref[0, rows, :].astype(BF16)
        acc = None
        for h in range(IDX_HEADS):
            term = iw_t[h:h + 1, :] * jnp.maximum(_dot_nt(ikb, iq[h]), 0.0)
            acc = term if acc is None else acc + term
        score_scr[rows, :] = acc
    qpos = j * QBLOCK + lax.broadcasted_iota(I32, (1, QBLOCK), 1)
    kpos = lax.broadcasted_iota(I32, (nkeys, 1), 0)
    causal = kpos <= qpos
    score = jnp.where(causal, score_scr[...], -jnp.inf)
    _topk_neg_mask(score, kpos, causal, topk, key_scr, mask_t_scr, key_axis=0)
    mask_scr[...] = mask_t_scr[...].T

    scale = d ** -0.5 * LOG2_E
    win_start = pl.multiple_of(jnp.maximum(j - 1, 0) * QBLOCK, QBLOCK)
    near_start = pl.multiple_of(jnp.where(j == 0, QBLOCK, 0), QBLOCK)

    lanes = lambda i: slice(i * d, (i + 1) * d)
    slot_rows = lambda h: slice((h % ATT_REP) * QBLOCK, (h % ATT_REP + 1) * QBLOCK)
    kv_cache = {}

    def group_kv(g):
        if g not in kv_cache:
            kv_cache[g] = (k_ref[0, :, lanes(g)].astype(BF16), v_ref[0, :, lanes(g)].astype(BF16))
        return kv_cache[g]

    def logits_and_max(h):
        rows = slot_rows(h)
        lg_scr[rows, :] = _dot_nt((q_ref[:, lanes(h)] * scale).astype(BF16), group_kv(h // ATT_REP)[0])
        lg_scr[rows, pl.ds(win_start, 2 * QBLOCK)] += near_ref[h, :, pl.ds(near_start, 2 * QBLOCK)]
        return jnp.max(lg_scr[rows, :] + mask_scr[...], axis=-1, keepdims=True)

    def weights(h, m):
        p = jnp.exp2(lg_scr[slot_rows(h), :] + mask_scr[...] - m)
        return p.astype(BF16), jnp.sum(p, axis=-1, keepdims=True)

    def values(h, p, l):
        o_ref[:, lanes(h)] = (_dot(p, group_kv(h // ATT_REP)[1]) / l).astype(BF16)

    row_max = [None] * ATT_HEADS
    probs = [None] * ATT_HEADS
    for step in range(ATT_HEADS + 2):
        if step < ATT_HEADS:
            row_max[step] = logits_and_max(step)
        if 0 <= step - 1 < ATT_HEADS:
            probs[step - 1] = weights(step - 1, row_max[step - 1])
        if 0 <= step - 2 < ATT_HEADS:
            values(step - 2, *probs[step - 2])


def _dsa_prompt(z, near):
    nq = SEQ // QBLOCK
    z_keys = z.reshape(BATCH, SEQ, ATT_IN_PAD)
    out = jnp.zeros((M_PROMPT, ATT_Q_W), BF16)
    for split in range(PROMPT_SPLITS):
        nkeys = (split + 1) * SPLIT_QBLOCKS * QBLOCK
        qrow = lambda b, j, s=split: b * nq + s * SPLIT_QBLOCKS + j
        out = pl.pallas_call(
            functools.partial(_dsa_prompt_kernel, split=split),
            grid=(BATCH, SPLIT_QBLOCKS),
            in_specs=[pl.BlockSpec((QBLOCK, ATT_Q_W), lambda b, j: (qrow(b, j), ATT_COL_Q // ATT_Q_W)),
                      pl.BlockSpec((QBLOCK, IQ_HALF_W), lambda b, j: (qrow(b, j), ATT_COL_IQ // IQ_HALF_W)),
                      pl.BlockSpec((QBLOCK, IQ_HALF_W), lambda b, j: (qrow(b, j), ATT_COL_IQ // IQ_HALF_W + 1)),
                      pl.BlockSpec((QBLOCK, LANES), lambda b, j: (qrow(b, j), ATT_COL_IW // LANES)),
                      pl.BlockSpec((1, nkeys, ATT_KV_W), lambda b, j: (b, 0, ATT_COL_K // ATT_KV_W)),
                      pl.BlockSpec((1, nkeys, ATT_KV_W), lambda b, j: (b, 0, ATT_COL_V // ATT_KV_W)),
                      pl.BlockSpec((1, nkeys, IDX_DIM), lambda b, j: (b, 0, ATT_COL_IK // IDX_DIM)),
                      pl.BlockSpec((ATT_HEADS, QBLOCK, 3 * QBLOCK), lambda b, j: (0, 0, 0)),
                      pl.BlockSpec(memory_space=pl.ANY)],
            out_specs=pl.BlockSpec((QBLOCK, ATT_Q_W), lambda b, j: (qrow(b, j), 0)),
            out_shape=jax.ShapeDtypeStruct((M_PROMPT, ATT_Q_W), BF16),
            input_output_aliases={8: 0},
            scratch_shapes=[pltpu.VMEM((nkeys, QBLOCK), F32), pltpu.VMEM((nkeys, QBLOCK), F32),
                            pltpu.VMEM((nkeys, QBLOCK), I32),
                            pltpu.VMEM((QBLOCK, nkeys), F32),
                            pltpu.VMEM((ATT_REP * QBLOCK, nkeys), F32)],
            compiler_params=_cparams(("parallel", "arbitrary"), 56),
        )(z, z, z, z, z_keys, z_keys, z_keys, near, out)
    return out


def _kv_rows_kernel(k_ref, v_ref, ko_ref, vo_ref):
    tm = k_ref.shape[0]
    for g in range(ATT_KV_HEADS):
        rows = pl.ds(g, tm, stride=ATT_KV_HEADS)
        cols = slice(g * ATT_HEAD_DIM, (g + 1) * ATT_HEAD_DIM)
        ko_ref[rows, :] = k_ref[:, cols]
        vo_ref[rows, :] = v_ref[:, cols]


def _kv_rows(z, tm=512):
    m = z.shape[0]
    out = jax.ShapeDtypeStruct((m * ATT_KV_HEADS, ATT_HEAD_DIM), F32)
    return pl.pallas_call(
        _kv_rows_kernel,
        grid=(m // tm,),
        in_specs=[pl.BlockSpec((tm, ATT_KV_W), lambda i: (i, ATT_COL_K // ATT_KV_W)),
                  pl.BlockSpec((tm, ATT_KV_W), lambda i: (i, ATT_COL_V // ATT_KV_W))],
        out_specs=[pl.BlockSpec((tm * ATT_KV_HEADS, ATT_HEAD_DIM), lambda i: (i, 0))] * 2,
        out_shape=[out, out],
        compiler_params=_cparams(("parallel",), 16),
    )(z, z)


SELECT_BATCHES = 16


def _sample_select_kernel(pt_ref, iq_ref, iw_ref, iknew_ref, cik_ref, mask_ref,
                          ikbuf, iknew_scr, score_scr, key_scr, mask_scr, sem):
    step = pl.program_id(0)
    nsteps = pl.num_programs(0)
    slot = lax.rem(step, 2)
    t = DEC_SEQ
    topk = min(TOPK_MAX, (PAST_LEN + DEC_SEQ) // 4)

    def page_copies(step_, slot_):
        copies = []
        for bb in range(SELECT_BATCHES):
            for p in range(N_PAGES):
                page = pt_ref[(step_ * SELECT_BATCHES + bb) * N_PAGES + p]
                dst = pl.ds(p * PAGE_SIZE, PAGE_SIZE)
                copies.append(pltpu.make_async_copy(cik_ref.at[page], ikbuf.at[slot_, bb, dst], sem.at[slot_]))
        return copies

    @pl.when(step == 0)
    def _():
        iknew_scr[...] = jnp.zeros_like(iknew_scr)
        for cp in page_copies(0, 0):
            cp.start()

    @pl.when(step + 1 < nsteps)
    def _():
        for cp in page_copies(step + 1, 1 - slot):
            cp.start()

    for cp in page_copies(step, slot):
        cp.wait()

    for bb in range(SELECT_BATCHES):
        iknew_scr[bb, 0:t, :] = iknew_ref[bb]
        iq = iq_ref[bb].astype(BF16)
        sc = jnp.concatenate([_dot_nt(iq, ikbuf[slot, bb].astype(BF16)),
                              _dot_nt(iq, iknew_scr[bb].astype(BF16))], axis=-1)
        ws = (iw_ref[bb] * (IDX_HEADS ** -0.5 * IDX_DIM ** -0.5)) * jnp.maximum(sc, 0.0)
        for i in range(t):
            score_scr[bb * t + i:bb * t + i + 1, :] = jnp.sum(
                ws[i * IDX_HEADS:(i + 1) * IDX_HEADS], axis=0, keepdims=True)

    rows = SELECT_BATCHES * t
    assert t & (t - 1) == 0
    tq = lax.broadcasted_iota(I32, (rows, 1), 0) & (t - 1)
    kpos = lax.broadcasted_iota(I32, (1, SAMPLE_KEYS), 1)
    causal = kpos <= PAST_LEN + tq
    score = jnp.where(causal, score_scr[...], -jnp.inf)
    _topk_neg_mask(score, kpos, causal, topk, key_scr, mask_scr)
    for bb in range(SELECT_BATCHES):
        mask_ref[bb] = mask_scr[bb * t:(bb + 1) * t, :]


def _sample_select(iq, iw, iknew, page_table, cik):
    rows = SELECT_BATCHES * DEC_SEQ
    per_step = lambda *shape: pl.BlockSpec((SELECT_BATCHES,) + shape, lambda s, pt: (s,) + (0,) * len(shape))
    grid_spec = pltpu.PrefetchScalarGridSpec(
        num_scalar_prefetch=1,
        grid=(DEC_BATCH // SELECT_BATCHES,),
        in_specs=[per_step(DEC_SEQ * IDX_HEADS, IDX_DIM),
                  per_step(DEC_SEQ * IDX_HEADS, 1),
                  per_step(DEC_SEQ, IDX_DIM),
                  pl.BlockSpec(memory_space=pl.ANY)],
        out_specs=per_step(DEC_SEQ, SAMPLE_KEYS),
        scratch_shapes=[pltpu.VMEM((2, SELECT_BATCHES, PAST_LEN, IDX_DIM), F32),
                        pltpu.VMEM((SELECT_BATCHES, LANES, IDX_DIM), F32),
                        pltpu.VMEM((rows, SAMPLE_KEYS), F32),
                        pltpu.VMEM((rows, SAMPLE_KEYS), I32),
                        pltpu.VMEM((rows, SAMPLE_KEYS), F32),
                        pltpu.SemaphoreType.DMA((2,))])
    return pl.pallas_call(
        _sample_select_kernel,
        grid_spec=grid_spec,
        out_shape=jax.ShapeDtypeStruct((DEC_BATCH, DEC_SEQ, SAMPLE_KEYS), F32),
        compiler_params=_cparams(("arbitrary",), 48),
    )(page_table.reshape(-1), iq, iw, iknew, cik)


def _dsa_sample_kernel(pt_ref, q_ref, mask_ref, knew_ref, vnew_ref, bias_ref,
                       ck_ref, cv_ref, o_ref, kbuf, vbuf, knew_scr, vnew_scr, sem):
    b = pl.program_id(0)
    nb = pl.num_programs(0)
    slot = lax.rem(b, 2)
    t = DEC_SEQ
    d = ATT_HEAD_DIM
    rows16 = ATT_REP * t

    def page_copies(batch, slot_):
        copies = []
        for p in range(N_PAGES):
            page = pt_ref[batch * N_PAGES + p]
            kv_rows = PAGE_SIZE * ATT_KV_HEADS
            kv_src = pl.ds(pl.multiple_of(page * kv_rows, kv_rows), kv_rows)
            kv_dst = pl.ds(p * kv_rows, kv_rows)
            copies.append(pltpu.make_async_copy(ck_ref.at[kv_src], kbuf.at[slot_, kv_dst], sem.at[0, slot_]))
            copies.append(pltpu.make_async_copy(cv_ref.at[kv_src], vbuf.at[slot_, kv_dst], sem.at[1, slot_]))
        return copies

    def kv_head(buf, g):
        return buf[slot, pl.ds(g, PAST_LEN, stride=ATT_KV_HEADS), :]

    @pl.when(b == 0)
    def _():
        knew_scr[...] = jnp.zeros_like(knew_scr)
        vnew_scr[...] = jnp.zeros_like(vnew_scr)
        for cp in page_copies(0, 0):
            cp.start()

    @pl.when(b + 1 < nb)
    def _():
        for cp in page_copies(b + 1, 1 - slot):
            cp.start()

    knew_scr[0:t, :] = knew_ref[0]
    vnew_scr[0:t, :] = vnew_ref[0]

    assert t & (t - 1) == 0
    tq = lax.broadcasted_iota(I32, (rows16, 1), 0) & (t - 1)
    neg_mask = jnp.zeros((rows16, SAMPLE_KEYS), F32)
    for i in range(t):
        neg_mask = jnp.where(tq == i, mask_ref[0, i:i + 1, :], neg_mask)

    for cp in page_copies(b, slot):
        cp.wait()

    scale = d ** -0.5
    for g in range(ATT_KV_HEADS):
        cols = slice(g * d, (g + 1) * d)
        qg = q_ref[0, g].astype(BF16)
        logits = jnp.concatenate([_dot_nt(qg, kv_head(kbuf, g).astype(BF16)),
                                  _dot_nt(qg, knew_scr[:, cols].astype(BF16))], axis=-1)
        x = logits * scale + bias_ref[g] + neg_mask
        p = jnp.exp(x - jnp.max(x, axis=-1, keepdims=True))
        l = jnp.sum(p, axis=-1, keepdims=True)
        pb = p.astype(BF16)
        out = (_dot(pb[:, :PAST_LEN], kv_head(vbuf, g).astype(BF16))
               + _dot(pb[:, PAST_LEN:], vnew_scr[:, cols].astype(BF16))) / l
        for r in range(ATT_REP):
            h = g * ATT_REP + r
            o_ref[0, :, h * d:(h + 1) * d] = out[r * t:(r + 1) * t]


def _dsa_sample(q, mask, knew, vnew, bias, page_table, ck, cv):
    rows16 = ATT_REP * DEC_SEQ
    whole = lambda *shape: pl.BlockSpec(shape, lambda b, pt: (0,) * len(shape))
    per_b = lambda *shape: pl.BlockSpec((1,) + shape, lambda b, pt: (b,) + (0,) * len(shape))
    grid_spec = pltpu.PrefetchScalarGridSpec(
        num_scalar_prefetch=1,
        grid=(DEC_BATCH,),
        in_specs=[per_b(ATT_KV_HEADS, rows16, ATT_HEAD_DIM),
                  per_b(DEC_SEQ, SAMPLE_KEYS),
                  per_b(DEC_SEQ, ATT_KV_W),
                  per_b(DEC_SEQ, ATT_KV_W),
                  whole(ATT_KV_HEADS, rows16, SAMPLE_KEYS),
                  pl.BlockSpec(memory_space=pl.ANY),
                  pl.BlockSpec(memory_space=pl.ANY)],
        out_specs=per_b(DEC_SEQ, ATT_Q_W),
        scratch_shapes=[pltpu.VMEM((2, PAST_LEN * ATT_KV_HEADS, ATT_HEAD_DIM), F32),
                        pltpu.VMEM((2, PAST_LEN * ATT_KV_HEADS, ATT_HEAD_DIM), F32),
                        pltpu.VMEM((LANES, ATT_KV_W), F32),
                        pltpu.VMEM((LANES, ATT_KV_W), F32),
                        pltpu.SemaphoreType.DMA((2, 2))])
    return pl.pallas_call(
        _dsa_sample_kernel,
        grid_spec=grid_spec,
        out_shape=jax.ShapeDtypeStruct((DEC_BATCH, DEC_SEQ, ATT_Q_W), F32),
        compiler_params=_cparams(("arbitrary",), 40),
    )(page_table.reshape(-1), q, mask, knew, vnew, bias, ck, cv)


def _rope_tables(pos):
    half = RET_DK // 2
    freqs = ROPE_BASE ** (-jnp.arange(half, dtype=F32) / half)
    ang = pos.astype(F32)[:, None] * freqs[None, :]
    return jnp.cos(ang), jnp.sin(ang)


def _pad_att_in(w):
    assert w.shape[1] == ATT_IN_WIDTH
    return jnp.pad(w, ((0, 0), (0, ATT_IN_PAD - ATT_IN_WIDTH)))


def _row_tile(m):
    return min(m, 1024)


def kernel(x_prompt, x_sample, p_prompt, p_sample, state_ret, cache_k, cache_v, cache_idx_k, page_table,
           norm_mix, norm_ffn, norm_ple, norm_final, w_ret_in, ret_gn_gain, w_ret_out,
           w_att_in, w_att_out, rel_bias, w_ffn_in, w_ffn_out, w_ple_gate, w_ple_proj):
    bf = lambda w: w.astype(BF16)
    hp = x_prompt.reshape(M_PROMPT, D_MODEL)
    hs = x_sample.reshape(M_SAMPLE, D_MODEL)
    pp = p_prompt.reshape(DEPTH, M_PROMPT, PLE_DIM)
    ps = p_sample.reshape(DEPTH, M_SAMPLE, PLE_DIM)

    def mixer_in(xn, w, tn):
        return _matmul(xn, w, tm=_row_tile(xn.shape[0]), tn=tn)

    def mixer_out(y, w, h, tm):
        return _matmul(y, w, h, tm=min(tm, h.shape[0]), tn=1024)

    w_ffn_in_b, w_ffn_out_b = bf(w_ffn_in), bf(w_ffn_out)
    w_ple_gate_b, w_ple_proj_b = bf(w_ple_gate), bf(w_ple_proj)

    def tail(h, p, i):
        last = i == DEPTH - 1
        h = _ffn(h, norm_ffn[i], w_ffn_in_b, w_ffn_out_b, i)
        g_next = norm_final if last else norm_mix[i + 1]
        return _ple(h, norm_ple[i], w_ple_gate_b, p, i, w_ple_proj_b, g_next, last)

    lg = jnp.log1p(-jnp.exp2(-5.0 - jnp.arange(RET_HEADS, dtype=F32)))
    cos_p, sin_p = _rope_tables(jnp.arange(SEQ, dtype=I32))
    cos_s, sin_s = _rope_tables(PAST_LEN + jnp.arange(DEC_SEQ, dtype=I32))
    w_in, w_out = bf(w_ret_in[0]), bf(w_ret_out[0])
    zp = mixer_in(_rmsnorm(hp, norm_mix[0], BF16), w_in, 2048)
    y, ret_state_p = _ret_prompt(zp, lg, cos_p, sin_p, ret_gn_gain[0])
    zs = mixer_in(_rmsnorm(hs, norm_mix[0], BF16), w_in, 2048).reshape(DEC_BATCH, DEC_SEQ, RET_IN_WIDTH)
    y_s, ret_state_s = _ret_sample(zs, lg, cos_s, sin_s, ret_gn_gain[0], state_ret)
    hp, xn_p = tail(mixer_out(y, w_out, hp, 512), pp, 0)
    hs, xn_s = tail(mixer_out(bf(y_s.reshape(M_SAMPLE, RET_V_W)), w_out, hs, 512), ps, 0)

    near, samp = _bias_tables(rel_bias)
    w_in, w_out = bf(_pad_att_in(w_att_in[0])), bf(w_att_out[0])
    zp = mixer_in(xn_p, w_in, 1792)
    a = _dsa_prompt(zp, near)
    zs = mixer_in(xn_s, w_in, 1792).reshape(DEC_BATCH, DEC_SEQ, ATT_IN_PAD)
    q_s = zs[..., :ATT_Q_W].reshape(DEC_BATCH, DEC_SEQ, ATT_KV_HEADS, ATT_REP, ATT_HEAD_DIM)
    q_s = q_s.transpose(0, 2, 3, 1, 4).reshape(DEC_BATCH, ATT_KV_HEADS, ATT_REP * DEC_SEQ, ATT_HEAD_DIM)
    bias_s = samp[:, :DEC_SEQ].reshape(ATT_KV_HEADS, ATT_REP * DEC_SEQ, SAMPLE_KEYS)
    iq_s = zs[..., ATT_COL_IQ:ATT_COL_IQ + IDX_Q_W].reshape(DEC_BATCH, DEC_SEQ * IDX_HEADS, IDX_DIM)
    iw_s = zs[..., ATT_COL_IW:ATT_COL_IW + IDX_HEADS].reshape(DEC_BATCH, DEC_SEQ * IDX_HEADS, 1)
    k_s = zs[..., ATT_COL_K:ATT_COL_K + ATT_KV_W]
    v_s = zs[..., ATT_COL_V:ATT_COL_V + ATT_KV_W]
    ik_s = zs[..., ATT_COL_IK:ATT_COL_IK + IDX_DIM]
    n_phys = cache_k.shape[1]
    mask_s = _sample_select(iq_s, iw_s, ik_s, page_table, cache_idx_k[0])
    a_s = _dsa_sample(q_s, mask_s, k_s, v_s, bias_s, page_table,
                      cache_k[0].reshape(n_phys * PAGE_SIZE * ATT_KV_HEADS, ATT_HEAD_DIM),
                      cache_v[0].reshape(n_phys * PAGE_SIZE * ATT_KV_HEADS, ATT_HEAD_DIM))
    yp = tail(mixer_out(a, w_out, hp, 1024), pp, 1)
    ys = tail(mixer_out(bf(a_s.reshape(M_SAMPLE, ATT_Q_W)), w_out, hs, 1024), ps, 1)

    kv_p = lambda x: x.reshape(1, BATCH, SEQ, ATT_KV_HEADS, ATT_HEAD_DIM)
    kv_s = lambda x: x.reshape(1, DEC_BATCH, DEC_SEQ, ATT_KV_HEADS, ATT_HEAD_DIM)
    k_p, v_p = _kv_rows(zp)
    return (yp.reshape(BATCH, SEQ, D_MODEL),
            ys.reshape(DEC_BATCH, DEC_SEQ, D_MODEL),
            ret_state_p[None],
            ret_state_s,
            kv_p(k_p), kv_p(v_p),
            zp[:, ATT_COL_IK:ATT_COL_IK + IDX_DIM].reshape(1, BATCH, SEQ, IDX_DIM),
            kv_s(k_s), kv_s(v_s),
            ik_s.reshape(1, DEC_BATCH, DEC_SEQ, IDX_DIM))
```

```python
import functools
import math

import jax
import jax.numpy as jnp
from jax import lax
from jax.experimental import pallas as pl
from jax.experimental.pallas import tpu as pltpu

F32 = jnp.float32
BF16 = jnp.bfloat16
I32 = jnp.int32

D_MODEL = 2048
BATCH = 4
SEQ = 2048
DEPTH = 2
DEC_BATCH = 128
DEC_SEQ = 4
PAST_LEN = 2048
PAGE_SIZE = 128
N_PAGES = PAST_LEN // PAGE_SIZE

RET_HEADS = 8
RET_DK = D_MODEL // RET_HEADS
RET_DV = 2 * D_MODEL // RET_HEADS
RET_CHUNK = 128
ROPE_BASE = 10000.0
ATT_HEADS = 16
ATT_HEAD_DIM = D_MODEL // ATT_HEADS
ATT_KV_HEADS = 4
ATT_REP = ATT_HEADS // ATT_KV_HEADS
IDX_HEADS = 16
IDX_DIM = 128
TOPK_MAX = 256
REL_BUCKETS = 32
REL_MAX_DIST = 128
FFN_HIDDEN = 5632
PLE_DIM = 256
EPS = 1e-6

RET_Q_W = RET_HEADS * RET_DK
RET_V_W = RET_HEADS * RET_DV
RET_IN_WIDTH = 2 * RET_Q_W + 2 * RET_V_W
ATT_Q_W = ATT_HEADS * ATT_HEAD_DIM
ATT_KV_W = ATT_KV_HEADS * ATT_HEAD_DIM
IDX_Q_W = IDX_HEADS * IDX_DIM

M_PROMPT = BATCH * SEQ
M_SAMPLE = DEC_BATCH * DEC_SEQ

LANES = 128
SUBLANES = 8
V7X_VMEM_BYTES = 64 * 1024 * 1024
MIB = 1024 * 1024

ATT_COL_Q = 0
ATT_COL_K = ATT_Q_W
ATT_COL_V = ATT_COL_K + ATT_KV_W
ATT_COL_IQ = ATT_COL_V + ATT_KV_W
ATT_COL_IK = ATT_COL_IQ + IDX_Q_W
ATT_COL_IW = ATT_COL_IK + IDX_DIM
ATT_IN_WIDTH = ATT_COL_IW + IDX_HEADS
ATT_IN_PAD = ATT_COL_IW + LANES
IQ_HALF_W = IDX_Q_W // 2
assert ATT_COL_IQ % IQ_HALF_W == 0

QBLOCK = 128
SAMPLE_KEYS = PAST_LEN + LANES
LOG2_E = math.log2(math.e)
INT_MIN = -(2 ** 31)
NEG_INF_KEY = -2139095041


def _cparams(semantics, vmem_mib):
    assert vmem_mib * MIB < V7X_VMEM_BYTES
    return pltpu.CompilerParams(dimension_semantics=semantics,
                                vmem_limit_bytes=vmem_mib * MIB)


def _dot(a, b):
    return jnp.dot(a, b, preferred_element_type=F32)


def _dot_nt(a, b):
    return lax.dot_general(a, b, (((1,), (1,)), ((), ())), preferred_element_type=F32)


def _rms(x, g):
    return x * lax.rsqrt(jnp.mean(x * x, axis=-1, keepdims=True) + EPS) * g


def _rmsnorm_kernel(x_ref, g_ref, o_ref):
    o_ref[...] = _rms(x_ref[...], g_ref[...]).astype(o_ref.dtype)


def _rmsnorm(x, g, out_dtype, tm=512):
    m, d = x.shape
    return pl.pallas_call(
        _rmsnorm_kernel,
        grid=(m // tm,),
        in_specs=[pl.BlockSpec((tm, d), lambda i: (i, 0)),
                  pl.BlockSpec((1, d), lambda i: (0, 0))],
        out_specs=pl.BlockSpec((tm, d), lambda i: (i, 0)),
        out_shape=jax.ShapeDtypeStruct((m, d), out_dtype),
        compiler_params=_cparams(("parallel",), 32),
    )(x, g.reshape(1, d))


def _mm_kernel(x_ref, w_ref, o_ref):
    o_ref[...] = _dot(x_ref[...], w_ref[...]).astype(o_ref.dtype)


def _mm_res_kernel(x_ref, w_ref, r_ref, o_ref):
    o_ref[...] = r_ref[...] + _dot(x_ref[...], w_ref[...])


def _matmul(x, w, res=None, *, tm, tn, out_dtype=F32, vmem_mib=48):
    m, k = x.shape
    n = w.shape[1]
    assert m % tm == 0 and n % tn == 0
    in_specs = [pl.BlockSpec((tm, k), lambda j, i: (i, 0)),
                pl.BlockSpec((k, tn), lambda j, i: (0, j))]
    args = [x, w]
    kern = _mm_kernel
    if res is not None:
        in_specs.append(pl.BlockSpec((tm, tn), lambda j, i: (i, j)))
        args.append(res)
        kern = _mm_res_kernel
    return pl.pallas_call(
        kern,
        grid=(n // tn, m // tm),
        in_specs=in_specs,
        out_specs=pl.BlockSpec((tm, tn), lambda j, i: (i, j)),
        out_shape=jax.ShapeDtypeStruct((m, n), out_dtype),
        compiler_params=_cparams(("parallel", "parallel"), vmem_mib),
    )(*args)


def _ffn_kernel(x_ref, g_ref, w1_ref, w2_ref, wo_ref, o_ref, xn_ref):
    @pl.when(pl.program_id(1) == 0)
    def _():
        x = x_ref[...]
        xn_ref[...] = _rms(x, g_ref[...]).astype(BF16)
        o_ref[...] = x

    xn = xn_ref[...]
    a = _dot(xn, w1_ref[0])
    b = _dot(xn, w2_ref[0])
    hid = (a * jax.nn.sigmoid(a) * b).astype(BF16)
    o_ref[...] += _dot(hid, wo_ref[0])


def _ffn(h, g, w_in, w_out, layer, tm=512, th=512):
    m, d = h.shape
    nh = FFN_HIDDEN // th
    return pl.pallas_call(
        _ffn_kernel,
        grid=(m // tm, nh),
        in_specs=[pl.BlockSpec((tm, d), lambda i, j: (i, 0)),
                  pl.BlockSpec((1, d), lambda i, j: (0, 0)),
                  pl.BlockSpec((1, d, th), lambda i, j: (layer, 0, j)),
                  pl.BlockSpec((1, d, th), lambda i, j: (layer, 0, j + nh)),
                  pl.BlockSpec((1, th, d), lambda i, j: (layer, j, 0))],
        out_specs=pl.BlockSpec((tm, d), lambda i, j: (i, 0)),
        out_shape=jax.ShapeDtypeStruct((m, d), F32),
        scratch_shapes=[pltpu.VMEM((tm, d), BF16)],
        compiler_params=_cparams(("parallel", "arbitrary"), 48),
    )(h, g.reshape(1, d), w_in, w_in, w_out)


def _ple_kernel(x_ref, g_ref, wg_ref, p_ref, wp_ref, gn_ref, *o_refs, last):
    x = x_ref[...]
    xn = _rms(x, g_ref[...]).astype(BF16)
    gate = jax.nn.sigmoid(_dot(xn, wg_ref[0]))
    y = x + gate * _dot(p_ref[0].astype(BF16), wp_ref[0])
    if last:
        o_refs[0][...] = _rms(y, gn_ref[...])
    else:
        o_refs[0][...] = y
        o_refs[1][...] = _rms(y, gn_ref[...]).astype(BF16)


def _ple(h, g, w_gate, p, layer, w_proj, g_next, last, tm=512):
    m, d = h.shape
    pd = p.shape[2]
    row_spec = pl.BlockSpec((tm, d), lambda i: (i, 0))
    stream = jax.ShapeDtypeStruct((m, d), F32)
    return pl.pallas_call(
        functools.partial(_ple_kernel, last=last),
        grid=(m // tm,),
        in_specs=[row_spec,
                  pl.BlockSpec((1, d), lambda i: (0, 0)),
                  pl.BlockSpec((1, d, d), lambda i: (layer, 0, 0)),
                  pl.BlockSpec((1, tm, pd), lambda i: (layer, i, 0)),
                  pl.BlockSpec((1, pd, d), lambda i: (layer, 0, 0)),
                  pl.BlockSpec((1, d), lambda i: (0, 0))],
        out_specs=row_spec if last else [row_spec, row_spec],
        out_shape=stream if last else [stream, jax.ShapeDtypeStruct((m, d), BF16)],
        compiler_params=_cparams(("parallel",), 48),
    )(h, g.reshape(1, d), w_gate, p, w_proj, g_next.reshape(1, d))


def _group_norm_gate(o, gate, gain):
    mu = jnp.mean(o, axis=-1, keepdims=True)
    var = jnp.mean(jnp.square(o - mu), axis=-1, keepdims=True)
    on = (o - mu) * lax.rsqrt(var + EPS)
    return gate * jax.nn.sigmoid(gate) * (on * gain)


RET_CHUNKS_PER_STEP = 8
RET_ROWS = RET_CHUNKS_PER_STEP * RET_CHUNK


def _ret_prompt_kernel(lg_ref, q_ref, k_ref, v_ref, g_ref, cos_ref, sin_ref, gain_ref,
                       y_ref, s_out_ref, s_scr):
    c = pl.program_id(2)
    lg = lg_ref[pl.program_id(1)]
    half = RET_DK // 2

    @pl.when(c == 0)
    def _():
        s_scr[...] = jnp.zeros_like(s_scr)

    n_col = lax.broadcasted_iota(I32, (RET_CHUNK, 1), 0).astype(F32)
    n_row = lax.broadcasted_iota(I32, (1, RET_CHUNK), 1).astype(F32)
    diff = n_col - n_row
    decay = jnp.where(diff >= 0, jnp.exp(lg * jnp.maximum(diff, 0.0)), 0.0)
    q_dec = jnp.exp(lg * (n_col + 1.0))
    k_dec = jnp.exp(lg * (RET_CHUNK - 1.0 - n_col))
    s_dec = jnp.exp(jnp.full((1, 1), lg * RET_CHUNK, F32))
    gain = gain_ref[...]

    def rotate(x, cos, sin):
        x1, x2 = x[:, :half], x[:, half:]
        return jnp.concatenate([x1 * cos - x2 * sin, x1 * sin + x2 * cos], axis=-1)

    for i in range(RET_CHUNKS_PER_STEP):
        rows = pl.ds(i * RET_CHUNK, RET_CHUNK)
        cos, sin = cos_ref[rows, :], sin_ref[rows, :]
        qr = rotate(q_ref[rows, :], cos, sin)
        kr = rotate(k_ref[rows, :], cos, sin) * (RET_DK ** -0.5)
        v = v_ref[rows, :].astype(BF16)
        s = s_scr[...]
        qb = qr.astype(BF16)
        scores = _dot_nt(qb, kr.astype(BF16)) * decay
        o = _dot(scores.astype(BF16), v) + _dot(qb, s.astype(BF16)) * q_dec
        kd_t = (kr * k_dec).T.astype(BF16)
        s_scr[...] = s_dec * s + _dot(kd_t, v)
        y_ref[rows, :] = _group_norm_gate(o, g_ref[rows, :], gain).astype(BF16)

    @pl.when(c == pl.num_programs(2) - 1)
    def _():
        s_out_ref[0, 0] = s_scr[...]


def _ret_prompt(z, lg, cos, sin, gain):
    steps = SEQ // RET_ROWS
    qb, vb = RET_Q_W // RET_DK, (2 * RET_Q_W) // RET_DV
    row = lambda b, h, c: b * steps + c
    return pl.pallas_call(
        _ret_prompt_kernel,
        grid=(BATCH, RET_HEADS, steps),
        in_specs=[pl.BlockSpec(memory_space=pltpu.SMEM),
                  pl.BlockSpec((RET_ROWS, RET_DK), lambda b, h, c: (row(b, h, c), h)),
                  pl.BlockSpec((RET_ROWS, RET_DK), lambda b, h, c: (row(b, h, c), qb + h)),
                  pl.BlockSpec((RET_ROWS, RET_DV), lambda b, h, c: (row(b, h, c), vb + h)),
                  pl.BlockSpec((RET_ROWS, RET_DV), lambda b, h, c: (row(b, h, c), vb + RET_HEADS + h)),
                  pl.BlockSpec((RET_ROWS, RET_DK // 2), lambda b, h, c: (c, 0)),
                  pl.BlockSpec((RET_ROWS, RET_DK // 2), lambda b, h, c: (c, 0)),
                  pl.BlockSpec((1, RET_DV), lambda b, h, c: (0, h))],
        out_specs=[pl.BlockSpec((RET_ROWS, RET_DV), lambda b, h, c: (row(b, h, c), h)),
                   pl.BlockSpec((1, 1, RET_DK, RET_DV), lambda b, h, c: (b, h, 0, 0))],
        out_shape=[jax.ShapeDtypeStruct((M_PROMPT, RET_V_W), BF16),
                   jax.ShapeDtypeStruct((BATCH, RET_HEADS, RET_DK, RET_DV), F32)],
        scratch_shapes=[pltpu.VMEM((RET_DK, RET_DV), F32)],
        compiler_params=_cparams(("parallel", "parallel", "arbitrary"), 32),
    )(lg, z, z, z, z, cos, sin, gain.reshape(1, RET_V_W))


def _ret_sample_kernel(lg_ref, zs_ref, cos_ref, sin_ref, gain_ref, s_ref, y_ref, s_out_ref,
                       q_pad, k_pad, kd_pad, v_pad):
    half = RET_DK // 2
    t = DEC_SEQ
    cos, sin = cos_ref[...], sin_ref[...]
    n_col = lax.broadcasted_iota(I32, (t, 1), 0).astype(F32)
    pad_rows = q_pad.shape[0]
    qi = lax.broadcasted_iota(I32, (pad_rows, LANES), 0)
    kj = lax.broadcasted_iota(I32, (pad_rows, LANES), 1)
    live = (qi >= kj) & (qi < t)
    diff = jnp.maximum(qi - kj, 0).astype(F32)

    def rotate(x):
        x1, x2 = x[:, :half], x[:, half:]
        return jnp.concatenate([x1 * cos - x2 * sin, x1 * sin + x2 * cos], axis=-1)

    q_pad[...] = jnp.zeros_like(q_pad)
    k_pad[...] = jnp.zeros_like(k_pad)
    kd_pad[...] = jnp.zeros_like(kd_pad)
    v_pad[...] = jnp.zeros_like(v_pad)

    for h in range(RET_HEADS):
        lg = lg_ref[h]
        kr = rotate(zs_ref[0, :, RET_Q_W + h * RET_DK:RET_Q_W + (h + 1) * RET_DK]) * (RET_DK ** -0.5)
        k_pad[0:t, :] = kr
        kd_pad[0:t, :] = kr * jnp.exp(lg * (t - 1.0 - n_col))
        q_pad[0:t, :] = rotate(zs_ref[0, :, h * RET_DK:(h + 1) * RET_DK])
        v_pad[0:t, :] = zs_ref[0, :, pl.ds(2 * RET_Q_W + h * RET_DV, RET_DV)]
        gate = zs_ref[0, :, pl.ds(2 * RET_Q_W + RET_V_W + h * RET_DV, RET_DV)]
        s = s_ref[0, 0, h]

        qb = q_pad[...].astype(BF16)
        vb = v_pad[...].astype(BF16)
        scores = _dot_nt(qb, k_pad[...].astype(BF16)) * jnp.where(live, jnp.exp(lg * diff), 0.0)
        o = _dot(scores.astype(BF16), vb)[0:t] + _dot(qb, s.astype(BF16))[0:t] * jnp.exp(lg * (n_col + 1.0))
        s_out_ref[0, 0, h] = (jnp.exp(jnp.full((1, 1), lg * t, F32)) * s
                              + _dot(kd_pad[...].T.astype(BF16), vb))
        y_ref[0, :, pl.ds(h * RET_DV, RET_DV)] = _group_norm_gate(
            o, gate, gain_ref[:, pl.ds(h * RET_DV, RET_DV)])


def _ret_sample(zs, lg, cos, sin, gain, state):
    half = RET_DK // 2
    whole = lambda *shape: pl.BlockSpec(shape, lambda b: (0,) * len(shape))
    return pl.pallas_call(
        _ret_sample_kernel,
        grid=(DEC_BATCH,),
        in_specs=[pl.BlockSpec(memory_space=pltpu.SMEM),
                  pl.BlockSpec((1, DEC_SEQ, RET_IN_WIDTH), lambda b: (b, 0, 0)),
                  whole(DEC_SEQ, half), whole(DEC_SEQ, half),
                  whole(1, RET_V_W),
                  pl.BlockSpec((1, 1, RET_HEADS, RET_DK, RET_DV), lambda b: (0, b, 0, 0, 0))],
        out_specs=[pl.BlockSpec((1, DEC_SEQ, RET_V_W), lambda b: (b, 0, 0)),
                   pl.BlockSpec((1, 1, RET_HEADS, RET_DK, RET_DV), lambda b: (0, b, 0, 0, 0))],
        out_shape=[jax.ShapeDtypeStruct((DEC_BATCH, DEC_SEQ, RET_V_W), F32),
                   jax.ShapeDtypeStruct(state.shape, F32)],
        scratch_shapes=[pltpu.VMEM((2 * SUBLANES, RET_DK), F32),
                        pltpu.VMEM((LANES, RET_DK), F32),
                        pltpu.VMEM((LANES, RET_DK), F32),
                        pltpu.VMEM((LANES, RET_DV), F32)],
        compiler_params=_cparams(("parallel",), 40),
    )(lg, zs, cos, sin, gain.reshape(1, RET_V_W), state)


def _rel_bucket(dist):
    n = jnp.maximum(dist, 0)
    max_exact = REL_BUCKETS // 2
    nf = jnp.maximum(n, max_exact).astype(F32)
    large = max_exact + (jnp.log(nf / max_exact) / math.log(REL_MAX_DIST / max_exact)
                         * (REL_BUCKETS - max_exact)).astype(I32)
    large = jnp.minimum(large, REL_BUCKETS - 1)
    return jnp.where(n < max_exact, n, large)


def _bias_kernel(table_ref, near_ref, samp_ref):
    h = pl.program_id(0)

    def lookup(dist):
        bucket = _rel_bucket(dist)
        acc = jnp.zeros(dist.shape, F32)
        for b in range(REL_BUCKETS):
            acc = jnp.where(bucket == b, table_ref[b, h], acc)
        return acc

    far = table_ref[REL_BUCKETS - 1, h]
    i = lax.broadcasted_iota(I32, (QBLOCK, 2 * QBLOCK), 0)
    j = lax.broadcasted_iota(I32, (QBLOCK, 2 * QBLOCK), 1)
    near = (lookup(QBLOCK + i - j) - far) * LOG2_E
    near_ref[0, :, 0:2 * QBLOCK] = near
    near_ref[0, :, 2 * QBLOCK:] = near[:, 0:QBLOCK]
    t = lax.broadcasted_iota(I32, (SUBLANES, SAMPLE_KEYS), 0)
    s = lax.broadcasted_iota(I32, (SUBLANES, SAMPLE_KEYS), 1)
    samp_ref[0] = lookup(PAST_LEN + t - s)


def _bias_tables(rel_bias):
    return pl.pallas_call(
        _bias_kernel,
        grid=(ATT_HEADS,),
        in_specs=[pl.BlockSpec(memory_space=pltpu.SMEM)],
        out_specs=[pl.BlockSpec((1, QBLOCK, 3 * QBLOCK), lambda h: (h, 0, 0)),
                   pl.BlockSpec((1, SUBLANES, SAMPLE_KEYS), lambda h: (h, 0, 0))],
        out_shape=[jax.ShapeDtypeStruct((ATT_HEADS, QBLOCK, 3 * QBLOCK), F32),
                   jax.ShapeDtypeStruct((ATT_HEADS, SUBLANES, SAMPLE_KEYS), F32)],
        compiler_params=_cparams(("parallel",), 16),
    )(rel_bias)


def _order_key(score):
    bits = lax.bitcast_convert_type(score + 0.0, I32)
    return bits ^ ((bits >> 31) & 0x7FFFFFFF)


def _topk_neg_mask(score, kpos, causal, topk, key_ref, mask_ref, *, key_axis=1):
    keys = score.shape[key_axis]
    key_ref[...] = _order_key(score)
    kf = float(topk)

    def count(mask):
        ind = jnp.where(mask, 1.0, 0.0)
        slab = 8 * SUBLANES
        if key_axis == 0 and keys % slab == 0 and keys > slab:
            ind = jnp.sum(ind.reshape(keys // slab, slab, ind.shape[1]), axis=0)
        return jnp.sum(ind, axis=key_axis, keepdims=True)

    prefix = jnp.where(count(key_ref[...] >= 0) >= kf, 0, INT_MIN).astype(I32)

    def thr_body(i, prefix):
        cand = prefix | jnp.left_shift(1, 30 - i)
        return jnp.where(count(key_ref[...] >= cand) >= kf, cand, prefix)

    thr = lax.fori_loop(0, 31, thr_body, prefix)
    ge = key_ref[...] >= thr
    mask_ref[...] = jnp.where(ge & causal, 0.0, -jnp.inf)

    tied = (count(ge) > kf) & (thr > NEG_INF_KEY)

    @pl.when(jnp.max(jnp.where(tied, 1.0, 0.0)) > 0.0)
    def _():
        need = kf - count(key_ref[...] > thr)
        nbits = (keys - 1).bit_length()

        def tie_body(i, last):
            cand = last | jnp.left_shift(1, nbits - 1 - i)
            before = (key_ref[...] == thr) & (kpos < cand)
            return jnp.where(count(before) < need, cand, last)

        last = lax.fori_loop(0, nbits, tie_body, jnp.zeros_like(thr))
        key = key_ref[...]
        keep = ((key > thr) | ((key == thr) & (kpos <= last))) & causal
        mask_ref[...] = jnp.where(keep, 0.0, -jnp.inf)


PROMPT_SPLITS = 8
SPLIT_QBLOCKS = SEQ // QBLOCK // PROMPT_SPLITS


def _dsa_prompt_kernel(q_ref, iq_lo_ref, iq_hi_ref, iw_ref, k_ref, v_ref, ik_ref, near_ref, prev_ref, o_ref,
                       score_scr, mask_t_scr, key_scr, mask_scr, lg_scr, *, split):
    del prev_ref
    nkeys = (split + 1) * SPLIT_QBLOCKS * QBLOCK
    j = split * SPLIT_QBLOCKS + pl.program_id(1)
    topk = min(TOPK_MAX, SEQ // 4)
    d = ATT_HEAD_DIM

    iw_t = (iw_ref[...] * (IDX_HEADS ** -0.5 * IDX_DIM ** -0.5)).T
    iq = [ref[:, hh * IDX_DIM:(hh + 1) * IDX_DIM].astype(BF16)
          for ref in (iq_lo_ref, iq_hi_ref) for hh in range(IDX_HEADS // 2)]
    slab = 2 * QBLOCK
    for s in range(nkeys // slab):
        rows = slice(s * slab, (s + 1) * slab)
        ikb = ik_ref[0, rows, :].astype(BF16)
        acc = None
        for h in range(IDX_HEADS):
            term = iw_t[h:h + 1, :] * jnp.maximum(_dot_nt(ikb, iq[h]), 0.0)
            acc = term if acc is None else acc + term
        score_scr[rows, :] = acc
    qpos = j * QBLOCK + lax.broadcasted_iota(I32, (1, QBLOCK), 1)
    kpos = lax.broadcasted_iota(I32, (nkeys, 1), 0)
    causal = kpos <= qpos
    score = jnp.where(causal, score_scr[...], -jnp.inf)
    _topk_neg_mask(score, kpos, causal, topk, key_scr, mask_t_scr, key_axis=0)
    mask_scr[...] = mask_t_scr[...].T

    scale = d ** -0.5 * LOG2_E
    win_start = pl.multiple_of(jnp.maximum(j - 1, 0) * QBLOCK, QBLOCK)
    near_start = pl.multiple_of(jnp.where(j == 0, QBLOCK, 0), QBLOCK)

    lanes = lambda i: slice(i * d, (i + 1) * d)
    slot_rows = lambda h: slice((h % ATT_REP) * QBLOCK, (h % ATT_REP + 1) * QBLOCK)
    kv_cache = {}

    def group_kv(g):
        if g not in kv_cache:
            kv_cache[g] = (k_ref[0, :, lanes(g)].astype(BF16), v_ref[0, :, lanes(g)].astype(BF16))
        return kv_cache[g]

    def logits_and_max(h):
        rows = slot_rows(h)
        qh = (q_ref[:, lanes(h)] * scale).astype(BF16)
        lg_scr[rows, :] = _dot_nt(qh, group_kv(h // ATT_REP)[0]) + mask_scr[...]
        lg_scr[rows, pl.ds(win_start, 2 * QBLOCK)] += near_ref[h, :, pl.ds(near_start, 2 * QBLOCK)]
        return jnp.max(lg_scr[rows, :], axis=-1, keepdims=True)

    def weights(h, m):
        p = jnp.exp2(lg_scr[slot_rows(h), :] - m)
        return p.astype(BF16), jnp.sum(p, axis=-1, keepdims=True)

    def values(h, p, l):
        o_ref[:, lanes(h)] = (_dot(p, group_kv(h // ATT_REP)[1]) / l).astype(BF16)

    row_max = [None] * ATT_HEADS
    probs = [None] * ATT_HEADS
    for step in range(ATT_HEADS + 2):
        if step < ATT_HEADS:
            row_max[step] = logits_and_max(step)
        if 0 <= step - 1 < ATT_HEADS:
            probs[step - 1] = weights(step - 1, row_max[step - 1])
        if 0 <= step - 2 < ATT_HEADS:
            values(step - 2, *probs[step - 2])


def _dsa_prompt(z, near):
    nq = SEQ // QBLOCK
    z_keys = z.reshape(BATCH, SEQ, ATT_IN_PAD)
    out = jnp.zeros((M_PROMPT, ATT_Q_W), BF16)
    for split in range(PROMPT_SPLITS):
        nkeys = (split + 1) * SPLIT_QBLOCKS * QBLOCK
        qrow = lambda b, j, s=split: b * nq + s * SPLIT_QBLOCKS + j
        out = pl.pallas_call(
            functools.partial(_dsa_prompt_kernel, split=split),
            grid=(BATCH, SPLIT_QBLOCKS),
            in_specs=[pl.BlockSpec((QBLOCK, ATT_Q_W), lambda b, j: (qrow(b, j), ATT_COL_Q // ATT_Q_W)),
                      pl.BlockSpec((QBLOCK, IQ_HALF_W), lambda b, j: (qrow(b, j), ATT_COL_IQ // IQ_HALF_W)),
                      pl.BlockSpec((QBLOCK, IQ_HALF_W), lambda b, j: (qrow(b, j), ATT_COL_IQ // IQ_HALF_W + 1)),
                      pl.BlockSpec((QBLOCK, LANES), lambda b, j: (qrow(b, j), ATT_COL_IW // LANES)),
                      pl.BlockSpec((1, nkeys, ATT_KV_W), lambda b, j: (b, 0, ATT_COL_K // ATT_KV_W)),
                      pl.BlockSpec((1, nkeys, ATT_KV_W), lambda b, j: (b, 0, ATT_COL_V // ATT_KV_W)),
                      pl.BlockSpec((1, nkeys, IDX_DIM), lambda b, j: (b, 0, ATT_COL_IK // IDX_DIM)),
                      pl.BlockSpec((ATT_HEADS, QBLOCK, 3 * QBLOCK), lambda b, j: (0, 0, 0)),
                      pl.BlockSpec(memory_space=pl.ANY)],
            out_specs=pl.BlockSpec((QBLOCK, ATT_Q_W), lambda b, j: (qrow(b, j), 0)),
            out_shape=jax.ShapeDtypeStruct((M_PROMPT, ATT_Q_W), BF16),
            input_output_aliases={8: 0},
            scratch_shapes=[pltpu.VMEM((nkeys, QBLOCK), F32), pltpu.VMEM((nkeys, QBLOCK), F32),
                            pltpu.VMEM((nkeys, QBLOCK), I32),
                            pltpu.VMEM((QBLOCK, nkeys), F32),
                            pltpu.VMEM((ATT_REP * QBLOCK, nkeys), F32)],
            compiler_params=_cparams(("parallel", "arbitrary"), 56),
        )(z, z, z, z, z_keys, z_keys, z_keys, near, out)
    return out


def _kv_rows_kernel(k_ref, v_ref, ko_ref, vo_ref):
    tm = k_ref.shape[0]
    for g in range(ATT_KV_HEADS):
        rows = pl.ds(g, tm, stride=ATT_KV_HEADS)
        cols = slice(g * ATT_HEAD_DIM, (g + 1) * ATT_HEAD_DIM)
        ko_ref[rows, :] = k_ref[:, cols]
        vo_ref[rows, :] = v_ref[:, cols]


def _kv_rows(z, tm=512):
    m = z.shape[0]
    out = jax.ShapeDtypeStruct((m * ATT_KV_HEADS, ATT_HEAD_DIM), F32)
    return pl.pallas_call(
        _kv_rows_kernel,
        grid=(m // tm,),
        in_specs=[pl.BlockSpec((tm, ATT_KV_W), lambda i: (i, ATT_COL_K // ATT_KV_W)),
                  pl.BlockSpec((tm, ATT_KV_W), lambda i: (i, ATT_COL_V // ATT_KV_W))],
        out_specs=[pl.BlockSpec((tm * ATT_KV_HEADS, ATT_HEAD_DIM), lambda i: (i, 0))] * 2,
        out_shape=[out, out],
        compiler_params=_cparams(("parallel",), 16),
    )(z, z)


SELECT_BATCHES = 16


def _sample_select_kernel(pt_ref, iq_ref, iw_ref, iknew_ref, cik_ref, mask_ref,
                          ikbuf, iknew_scr, score_scr, key_scr, mask_scr, sem):
    step = pl.program_id(0)
    nsteps = pl.num_programs(0)
    slot = lax.rem(step, 2)
    t = DEC_SEQ
    topk = min(TOPK_MAX, (PAST_LEN + DEC_SEQ) // 4)

    def page_copies(step_, slot_):
        copies = []
        for bb in range(SELECT_BATCHES):
            for p in range(N_PAGES):
                page = pt_ref[(step_ * SELECT_BATCHES + bb) * N_PAGES + p]
                dst = pl.ds(p * PAGE_SIZE, PAGE_SIZE)
                copies.append(pltpu.make_async_copy(cik_ref.at[page], ikbuf.at[slot_, bb, dst], sem.at[slot_]))
        return copies

    @pl.when(step == 0)
    def _():
        iknew_scr[...] = jnp.zeros_like(iknew_scr)
        for cp in page_copies(0, 0):
            cp.start()

    @pl.when(step + 1 < nsteps)
    def _():
        for cp in page_copies(step + 1, 1 - slot):
            cp.start()

    for cp in page_copies(step, slot):
        cp.wait()

    for bb in range(SELECT_BATCHES):
        iknew_scr[bb, 0:t, :] = iknew_ref[bb]
        iq = iq_ref[bb].astype(BF16)
        sc = jnp.concatenate([_dot_nt(iq, ikbuf[slot, bb].astype(BF16)),
                              _dot_nt(iq, iknew_scr[bb].astype(BF16))], axis=-1)
        ws = (iw_ref[bb] * (IDX_HEADS ** -0.5 * IDX_DIM ** -0.5)) * jnp.maximum(sc, 0.0)
        for i in range(t):
            score_scr[bb * t + i:bb * t + i + 1, :] = jnp.sum(
                ws[i * IDX_HEADS:(i + 1) * IDX_HEADS], axis=0, keepdims=True)

    rows = SELECT_BATCHES * t
    assert t & (t - 1) == 0
    tq = lax.broadcasted_iota(I32, (rows, 1), 0) & (t - 1)
    kpos = lax.broadcasted_iota(I32, (1, SAMPLE_KEYS), 1)
    causal = kpos <= PAST_LEN + tq
    score = jnp.where(causal, score_scr[...], -jnp.inf)
    _topk_neg_mask(score, kpos, causal, topk, key_scr, mask_scr)
    for bb in range(SELECT_BATCHES):
        mask_ref[bb] = mask_scr[bb * t:(bb + 1) * t, :]


def _sample_select(iq, iw, iknew, page_table, cik):
    rows = SELECT_BATCHES * DEC_SEQ
    per_step = lambda *shape: pl.BlockSpec((SELECT_BATCHES,) + shape, lambda s, pt: (s,) + (0,) * len(shape))
    grid_spec = pltpu.PrefetchScalarGridSpec(
        num_scalar_prefetch=1,
        grid=(DEC_BATCH // SELECT_BATCHES,),
        in_specs=[per_step(DEC_SEQ * IDX_HEADS, IDX_DIM),
                  per_step(DEC_SEQ * IDX_HEADS, 1),
                  per_step(DEC_SEQ, IDX_DIM),
                  pl.BlockSpec(memory_space=pl.ANY)],
        out_specs=per_step(DEC_SEQ, SAMPLE_KEYS),
        scratch_shapes=[pltpu.VMEM((2, SELECT_BATCHES, PAST_LEN, IDX_DIM), F32),
                        pltpu.VMEM((SELECT_BATCHES, LANES, IDX_DIM), F32),
                        pltpu.VMEM((rows, SAMPLE_KEYS), F32),
                        pltpu.VMEM((rows, SAMPLE_KEYS), I32),
                        pltpu.VMEM((rows, SAMPLE_KEYS), F32),
                        pltpu.SemaphoreType.DMA((2,))])
    return pl.pallas_call(
        _sample_select_kernel,
        grid_spec=grid_spec,
        out_shape=jax.ShapeDtypeStruct((DEC_BATCH, DEC_SEQ, SAMPLE_KEYS), F32),
        compiler_params=_cparams(("arbitrary",), 48),
    )(page_table.reshape(-1), iq, iw, iknew, cik)


def _dsa_sample_kernel(pt_ref, q_ref, mask_ref, knew_ref, vnew_ref, bias_ref,
                       ck_ref, cv_ref, o_ref, kbuf, vbuf, knew_scr, vnew_scr, sem):
    b = pl.program_id(0)
    nb = pl.num_programs(0)
    slot = lax.rem(b, 2)
    t = DEC_SEQ
    d = ATT_HEAD_DIM
    rows16 = ATT_REP * t

    def page_copies(batch, slot_):
        copies = []
        for p in range(N_PAGES):
            page = pt_ref[batch * N_PAGES + p]
            kv_rows = PAGE_SIZE * ATT_KV_HEADS
            kv_src = pl.ds(pl.multiple_of(page * kv_rows, kv_rows), kv_rows)
            kv_dst = pl.ds(p * kv_rows, kv_rows)
            copies.append(pltpu.make_async_copy(ck_ref.at[kv_src], kbuf.at[slot_, kv_dst], sem.at[0, slot_]))
            copies.append(pltpu.make_async_copy(cv_ref.at[kv_src], vbuf.at[slot_, kv_dst], sem.at[1, slot_]))
        return copies

    def kv_head(buf, g):
        return buf[slot, pl.ds(g, PAST_LEN, stride=ATT_KV_HEADS), :]

    @pl.when(b == 0)
    def _():
        knew_scr[...] = jnp.zeros_like(knew_scr)
        vnew_scr[...] = jnp.zeros_like(vnew_scr)
        for cp in page_copies(0, 0):
            cp.start()

    @pl.when(b + 1 < nb)
    def _():
        for cp in page_copies(b + 1, 1 - slot):
            cp.start()

    knew_scr[0:t, :] = knew_ref[0]
    vnew_scr[0:t, :] = vnew_ref[0]

    assert t & (t - 1) == 0
    tq = lax.broadcasted_iota(I32, (rows16, 1), 0) & (t - 1)
    neg_mask = jnp.zeros((rows16, SAMPLE_KEYS), F32)
    for i in range(t):
        neg_mask = jnp.where(tq == i, mask_ref[0, i:i + 1, :], neg_mask)

    for cp in page_copies(b, slot):
        cp.wait()

    scale = d ** -0.5
    for g in range(ATT_KV_HEADS):
        cols = slice(g * d, (g + 1) * d)
        qg = q_ref[0, g].astype(BF16)
        logits = jnp.concatenate([_dot_nt(qg, kv_head(kbuf, g).astype(BF16)),
                                  _dot_nt(qg, knew_scr[:, cols].astype(BF16))], axis=-1)
        x = logits * scale + bias_ref[g] + neg_mask
        p = jnp.exp(x - jnp.max(x, axis=-1, keepdims=True))
        l = jnp.sum(p, axis=-1, keepdims=True)
        pb = p.astype(BF16)
        out = (_dot(pb[:, :PAST_LEN], kv_head(vbuf, g).astype(BF16))
               + _dot(pb[:, PAST_LEN:], vnew_scr[:, cols].astype(BF16))) / l
        for r in range(ATT_REP):
            h = g * ATT_REP + r
            o_ref[0, :, h * d:(h + 1) * d] = out[r * t:(r + 1) * t]


def _dsa_sample(q, mask, knew, vnew, bias, page_table, ck, cv):
    rows16 = ATT_REP * DEC_SEQ
    whole = lambda *shape: pl.BlockSpec(shape, lambda b, pt: (0,) * len(shape))
    per_b = lambda *shape: pl.BlockSpec((1,) + shape, lambda b, pt: (b,) + (0,) * len(shape))
    grid_spec = pltpu.PrefetchScalarGridSpec(
        num_scalar_prefetch=1,
        grid=(DEC_BATCH,),
        in_specs=[per_b(ATT_KV_HEADS, rows16, ATT_HEAD_DIM),
                  per_b(DEC_SEQ, SAMPLE_KEYS),
                  per_b(DEC_SEQ, ATT_KV_W),
                  per_b(DEC_SEQ, ATT_KV_W),
                  whole(ATT_KV_HEADS, rows16, SAMPLE_KEYS),
                  pl.BlockSpec(memory_space=pl.ANY),
                  pl.BlockSpec(memory_space=pl.ANY)],
        out_specs=per_b(DEC_SEQ, ATT_Q_W),
        scratch_shapes=[pltpu.VMEM((2, PAST_LEN * ATT_KV_HEADS, ATT_HEAD_DIM), F32),
                        pltpu.VMEM((2, PAST_LEN * ATT_KV_HEADS, ATT_HEAD_DIM), F32),
                        pltpu.VMEM((LANES, ATT_KV_W), F32),
                        pltpu.VMEM((LANES, ATT_KV_W), F32),
                        pltpu.SemaphoreType.DMA((2, 2))])
    return pl.pallas_call(
        _dsa_sample_kernel,
        grid_spec=grid_spec,
        out_shape=jax.ShapeDtypeStruct((DEC_BATCH, DEC_SEQ, ATT_Q_W), F32),
        compiler_params=_cparams(("arbitrary",), 40),
    )(page_table.reshape(-1), q, mask, knew, vnew, bias, ck, cv)


def _rope_tables(pos):
    half = RET_DK // 2
    freqs = ROPE_BASE ** (-jnp.arange(half, dtype=F32) / half)
    ang = pos.astype(F32)[:, None] * freqs[None, :]
    return jnp.cos(ang), jnp.sin(ang)


def _pad_att_in(w):
    assert w.shape[1] == ATT_IN_WIDTH
    return jnp.pad(w, ((0, 0), (0, ATT_IN_PAD - ATT_IN_WIDTH)))


def _row_tile(m):
    return min(m, 1024)


def kernel(x_prompt, x_sample, p_prompt, p_sample, state_ret, cache_k, cache_v, cache_idx_k, page_table,
           norm_mix, norm_ffn, norm_ple, norm_final, w_ret_in, ret_gn_gain, w_ret_out,
           w_att_in, w_att_out, rel_bias, w_ffn_in, w_ffn_out, w_ple_gate, w_ple_proj):
    bf = lambda w: w.astype(BF16)
    hp = x_prompt.reshape(M_PROMPT, D_MODEL)
    hs = x_sample.reshape(M_SAMPLE, D_MODEL)
    pp = p_prompt.reshape(DEPTH, M_PROMPT, PLE_DIM)
    ps = p_sample.reshape(DEPTH, M_SAMPLE, PLE_DIM)

    def mixer_in(xn, w, tn):
        return _matmul(xn, w, tm=_row_tile(xn.shape[0]), tn=tn)

    def mixer_out(y, w, h, tm):
        return _matmul(y, w, h, tm=min(tm, h.shape[0]), tn=1024)

    w_ffn_in_b, w_ffn_out_b = bf(w_ffn_in), bf(w_ffn_out)
    w_ple_gate_b, w_ple_proj_b = bf(w_ple_gate), bf(w_ple_proj)

    def tail(h, p, i):
        last = i == DEPTH - 1
        h = _ffn(h, norm_ffn[i], w_ffn_in_b, w_ffn_out_b, i)
        g_next = norm_final if last else norm_mix[i + 1]
        return _ple(h, norm_ple[i], w_ple_gate_b, p, i, w_ple_proj_b, g_next, last)

    lg = jnp.log1p(-jnp.exp2(-5.0 - jnp.arange(RET_HEADS, dtype=F32)))
    cos_p, sin_p = _rope_tables(jnp.arange(SEQ, dtype=I32))
    cos_s, sin_s = _rope_tables(PAST_LEN + jnp.arange(DEC_SEQ, dtype=I32))
    w_in, w_out = bf(w_ret_in[0]), bf(w_ret_out[0])
    zp = mixer_in(_rmsnorm(hp, norm_mix[0], BF16), w_in, 2048)
    y, ret_state_p = _ret_prompt(zp, lg, cos_p, sin_p, ret_gn_gain[0])
    zs = mixer_in(_rmsnorm(hs, norm_mix[0], BF16), w_in, 2048).reshape(DEC_BATCH, DEC_SEQ, RET_IN_WIDTH)
    y_s, ret_state_s = _ret_sample(zs, lg, cos_s, sin_s, ret_gn_gain[0], state_ret)
    hp, xn_p = tail(mixer_out(y, w_out, hp, 512), pp, 0)
    hs, xn_s = tail(mixer_out(bf(y_s.reshape(M_SAMPLE, RET_V_W)), w_out, hs, 512), ps, 0)

    near, samp = _bias_tables(rel_bias)
    w_in, w_out = bf(_pad_att_in(w_att_in[0])), bf(w_att_out[0])
    zp = mixer_in(xn_p, w_in, 1792)
    a = _dsa_prompt(zp, near)
    zs = mixer_in(xn_s, w_in, 1792).reshape(DEC_BATCH, DEC_SEQ, ATT_IN_PAD)
    q_s = zs[..., :ATT_Q_W].reshape(DEC_BATCH, DEC_SEQ, ATT_KV_HEADS, ATT_REP, ATT_HEAD_DIM)
    q_s = q_s.transpose(0, 2, 3, 1, 4).reshape(DEC_BATCH, ATT_KV_HEADS, ATT_REP * DEC_SEQ, ATT_HEAD_DIM)
    bias_s = samp[:, :DEC_SEQ].reshape(ATT_KV_HEADS, ATT_REP * DEC_SEQ, SAMPLE_KEYS)
    iq_s = zs[..., ATT_COL_IQ:ATT_COL_IQ + IDX_Q_W].reshape(DEC_BATCH, DEC_SEQ * IDX_HEADS, IDX_DIM)
    iw_s = zs[..., ATT_COL_IW:ATT_COL_IW + IDX_HEADS].reshape(DEC_BATCH, DEC_SEQ * IDX_HEADS, 1)
    k_s = zs[..., ATT_COL_K:ATT_COL_K + ATT_KV_W]
    v_s = zs[..., ATT_COL_V:ATT_COL_V + ATT_KV_W]
    ik_s = zs[..., ATT_COL_IK:ATT_COL_IK + IDX_DIM]
    n_phys = cache_k.shape[1]
    mask_s = _sample_select(iq_s, iw_s, ik_s, page_table, cache_idx_k[0])
    a_s = _dsa_sample(q_s, mask_s, k_s, v_s, bias_s, page_table,
                      cache_k[0].reshape(n_phys * PAGE_SIZE * ATT_KV_HEADS, ATT_HEAD_DIM),
                      cache_v[0].reshape(n_phys * PAGE_SIZE * ATT_KV_HEADS, ATT_HEAD_DIM))
    yp = tail(mixer_out(a, w_out, hp, 1024), pp, 1)
    ys = tail(mixer_out(bf(a_s.reshape(M_SAMPLE, ATT_Q_W)), w_out, hs, 1024), ps, 1)

    kv_p = lambda x: x.reshape(1, BATCH, SEQ, ATT_KV_HEADS, ATT_HEAD_DIM)
    kv_s = lambda x: x.reshape(1, DEC_BATCH, DEC_SEQ, ATT_KV_HEADS, ATT_HEAD_DIM)
    k_p, v_p = _kv_rows(zp)
    return (yp.reshape(BATCH, SEQ, D_MODEL),
            ys.reshape(DEC_BATCH, DEC_SEQ, D_MODEL),
            ret_state_p[None],
            ret_state_s,
            kv_p(k_p), kv_p(v_p),
            zp[:, ATT_COL_IK:ATT_COL_IK + IDX_DIM].reshape(1, BATCH, SEQ, IDX_DIM),
            kv_s(k_s), kv_s(v_s),
            ik_s.reshape(1, DEC_BATCH, DEC_SEQ, IDX_DIM))
```

```python
import functools
import math

import jax
import jax.numpy as jnp
from jax import lax
from jax.experimental import pallas as pl
from jax.experimental.pallas import tpu as pltpu

F32 = jnp.float32
BF16 = jnp.bfloat16
I32 = jnp.int32

D_MODEL = 2048
BATCH = 4
SEQ = 2048
DEPTH = 2
DEC_BATCH = 128
DEC_SEQ = 4
PAST_LEN = 2048
PAGE_SIZE = 128
N_PAGES = PAST_LEN // PAGE_SIZE

RET_HEADS = 8
RET_DK = D_MODEL // RET_HEADS
RET_DV = 2 * D_MODEL // RET_HEADS
RET_CHUNK = 128
ROPE_BASE = 10000.0
ATT_HEADS = 16
ATT_HEAD_DIM = D_MODEL // ATT_HEADS
ATT_KV_HEADS = 4
ATT_REP = ATT_HEADS // ATT_KV_HEADS
IDX_HEADS = 16
IDX_DIM = 128
TOPK_MAX = 256
REL_BUCKETS = 32
REL_MAX_DIST = 128
FFN_HIDDEN = 5632
PLE_DIM = 256
EPS = 1e-6

RET_Q_W = RET_HEADS * RET_DK
RET_V_W = RET_HEADS * RET_DV
RET_IN_WIDTH = 2 * RET_Q_W + 2 * RET_V_W
ATT_Q_W = ATT_HEADS * ATT_HEAD_DIM
ATT_KV_W = ATT_KV_HEADS * ATT_HEAD_DIM
IDX_Q_W = IDX_HEADS * IDX_DIM

M_PROMPT = BATCH * SEQ
M_SAMPLE = DEC_BATCH * DEC_SEQ

LANES = 128
SUBLANES = 8
V7X_VMEM_BYTES = 64 * 1024 * 1024
MIB = 1024 * 1024

ATT_COL_Q = 0
ATT_COL_K = ATT_Q_W
ATT_COL_V = ATT_COL_K + ATT_KV_W
ATT_COL_IQ = ATT_COL_V + ATT_KV_W
ATT_COL_IK = ATT_COL_IQ + IDX_Q_W
ATT_COL_IW = ATT_COL_IK + IDX_DIM
ATT_IN_WIDTH = ATT_COL_IW + IDX_HEADS
ATT_IN_PAD = ATT_COL_IW + LANES
IQ_HALF_W = IDX_Q_W // 2
assert ATT_COL_IQ % IQ_HALF_W == 0

QBLOCK = 128
SAMPLE_KEYS = PAST_LEN + LANES
LOG2_E = math.log2(math.e)
INT_MIN = -(2 ** 31)
NEG_INF_KEY = -2139095041


def _cparams(semantics, vmem_mib):
    assert vmem_mib * MIB < V7X_VMEM_BYTES
    return pltpu.CompilerParams(dimension_semantics=semantics,
                                vmem_limit_bytes=vmem_mib * MIB)


def _dot(a, b):
    return jnp.dot(a, b, preferred_element_type=F32)


def _dot_nt(a, b):
    return lax.dot_general(a, b, (((1,), (1,)), ((), ())), preferred_element_type=F32)


def _rms(x, g):
    return x * lax.rsqrt(jnp.mean(x * x, axis=-1, keepdims=True) + EPS) * g


def _rmsnorm_kernel(x_ref, g_ref, o_ref):
    o_ref[...] = _rms(x_ref[...], g_ref[...]).astype(o_ref.dtype)


def _rmsnorm(x, g, out_dtype, tm=512):
    m, d = x.shape
    return pl.pallas_call(
        _rmsnorm_kernel,
        grid=(m // tm,),
        in_specs=[pl.BlockSpec((tm, d), lambda i: (i, 0)),
                  pl.BlockSpec((1, d), lambda i: (0, 0))],
        out_specs=pl.BlockSpec((tm, d), lambda i: (i, 0)),
        out_shape=jax.ShapeDtypeStruct((m, d), out_dtype),
        compiler_params=_cparams(("parallel",), 32),
    )(x, g.reshape(1, d))


def _mm_kernel(x_ref, w_ref, o_ref):
    o_ref[...] = _dot(x_ref[...], w_ref[...]).astype(o_ref.dtype)


def _mm_res_kernel(x_ref, w_ref, r_ref, o_ref):
    o_ref[...] = r_ref[...] + _dot(x_ref[...], w_ref[...])


def _matmul(x, w, res=None, *, tm, tn, out_dtype=F32, vmem_mib=48):
    m, k = x.shape
    n = w.shape[1]
    assert m % tm == 0 and n % tn == 0
    in_specs = [pl.BlockSpec((tm, k), lambda j, i: (i, 0)),
                pl.BlockSpec((k, tn), lambda j, i: (0, j))]
    args = [x, w]
    kern = _mm_kernel
    if res is not None:
        in_specs.append(pl.BlockSpec((tm, tn), lambda j, i: (i, j)))
        args.append(res)
        kern = _mm_res_kernel
    return pl.pallas_call(
        kern,
        grid=(n // tn, m // tm),
        in_specs=in_specs,
        out_specs=pl.BlockSpec((tm, tn), lambda j, i: (i, j)),
        out_shape=jax.ShapeDtypeStruct((m, n), out_dtype),
        compiler_params=_cparams(("parallel", "parallel"), vmem_mib),
    )(*args)


def _ffn_kernel(x_ref, g_ref, w1_ref, w2_ref, wo_ref, o_ref, xn_ref):
    @pl.when(pl.program_id(1) == 0)
    def _():
        x = x_ref[...]
        xn_ref[...] = _rms(x, g_ref[...]).astype(BF16)
        o_ref[...] = x

    xn = xn_ref[...]
    a = _dot(xn, w1_ref[0])
    b = _dot(xn, w2_ref[0])
    hid = (a * jax.nn.sigmoid(a) * b).astype(BF16)
    o_ref[...] += _dot(hid, wo_ref[0])


def _ffn(h, g, w_in, w_out, layer, tm=512, th=512):
    m, d = h.shape
    nh = FFN_HIDDEN // th
    return pl.pallas_call(
        _ffn_kernel,
        grid=(m // tm, nh),
        in_specs=[pl.BlockSpec((tm, d), lambda i, j: (i, 0)),
                  pl.BlockSpec((1, d), lambda i, j: (0, 0)),
                  pl.BlockSpec((1, d, th), lambda i, j: (layer, 0, j)),
                  pl.BlockSpec((1, d, th), lambda i, j: (layer, 0, j + nh)),
                  pl.BlockSpec((1, th, d), lambda i, j: (layer, j, 0))],
        out_specs=pl.BlockSpec((tm, d), lambda i, j: (i, 0)),
        out_shape=jax.ShapeDtypeStruct((m, d), F32),
        scratch_shapes=[pltpu.VMEM((tm, d), BF16)],
        compiler_params=_cparams(("parallel", "arbitrary"), 48),
    )(h, g.reshape(1, d), w_in, w_in, w_out)


def _ple_kernel(x_ref, g_ref, wg_ref, p_ref, wp_ref, gn_ref, *o_refs, last):
    x = x_ref[...]
    xn = _rms(x, g_ref[...]).astype(BF16)
    gate = jax.nn.sigmoid(_dot(xn, wg_ref[0]))
    y = x + gate * _dot(p_ref[0].astype(BF16), wp_ref[0])
    if last:
        o_refs[0][...] = _rms(y, gn_ref[...])
    else:
        o_refs[0][...] = y
        o_refs[1][...] = _rms(y, gn_ref[...]).astype(BF16)


def _ple(h, g, w_gate, p, layer, w_proj, g_next, last, tm=512):
    m, d = h.shape
    pd = p.shape[2]
    row_spec = pl.BlockSpec((tm, d), lambda i: (i, 0))
    stream = jax.ShapeDtypeStruct((m, d), F32)
    return pl.pallas_call(
        functools.partial(_ple_kernel, last=last),
        grid=(m // tm,),
        in_specs=[row_spec,
                  pl.BlockSpec((1, d), lambda i: (0, 0)),
                  pl.BlockSpec((1, d, d), lambda i: (layer, 0, 0)),
                  pl.BlockSpec((1, tm, pd), lambda i: (layer, i, 0)),
                  pl.BlockSpec((1, pd, d), lambda i: (layer, 0, 0)),
                  pl.BlockSpec((1, d), lambda i: (0, 0))],
        out_specs=row_spec if last else [row_spec, row_spec],
        out_shape=stream if last else [stream, jax.ShapeDtypeStruct((m, d), BF16)],
        compiler_params=_cparams(("parallel",), 48),
    )(h, g.reshape(1, d), w_gate, p, w_proj, g_next.reshape(1, d))


def _group_norm_gate(o, gate, gain):
    mu = jnp.mean(o, axis=-1, keepdims=True)
    var = jnp.mean(jnp.square(o - mu), axis=-1, keepdims=True)
    on = (o - mu) * lax.rsqrt(var + EPS)
    return gate * jax.nn.sigmoid(gate) * (on * gain)


RET_CHUNKS_PER_STEP = 16
RET_ROWS = RET_CHUNKS_PER_STEP * RET_CHUNK


def _ret_prompt_kernel(lg_ref, q_ref, k_ref, v_ref, g_ref, cos_ref, sin_ref, gain_ref,
                       y_ref, s_out_ref, s_scr):
    c = pl.program_id(2)
    lg = lg_ref[pl.program_id(1)]
    half = RET_DK // 2

    @pl.when(c == 0)
    def _():
        s_scr[...] = jnp.zeros_like(s_scr)

    n_col = lax.broadcasted_iota(I32, (RET_CHUNK, 1), 0).astype(F32)
    n_row = lax.broadcasted_iota(I32, (1, RET_CHUNK), 1).astype(F32)
    diff = n_col - n_row
    decay = jnp.where(diff >= 0, jnp.exp(lg * jnp.maximum(diff, 0.0)), 0.0)
    q_dec = jnp.exp(lg * (n_col + 1.0))
    k_dec = jnp.exp(lg * (RET_CHUNK - 1.0 - n_col))
    s_dec = jnp.exp(jnp.full((1, 1), lg * RET_CHUNK, F32))
    gain = gain_ref[...]

    def rotate(x, cos, sin):
        x1, x2 = x[:, :half], x[:, half:]
        return jnp.concatenate([x1 * cos - x2 * sin, x1 * sin + x2 * cos], axis=-1)

    for i in range(RET_CHUNKS_PER_STEP):
        rows = pl.ds(i * RET_CHUNK, RET_CHUNK)
        cos, sin = cos_ref[rows, :], sin_ref[rows, :]
        qr = rotate(q_ref[rows, :], cos, sin)
        kr = rotate(k_ref[rows, :], cos, sin) * (RET_DK ** -0.5)
        v = v_ref[rows, :].astype(BF16)
        s = s_scr[...]
        qb = qr.astype(BF16)
        scores = _dot_nt(qb, kr.astype(BF16)) * decay
        o = _dot(scores.astype(BF16), v) + _dot(qb, s.astype(BF16)) * q_dec
        kd_t = (kr * k_dec).T.astype(BF16)
        s_scr[...] = s_dec * s + _dot(kd_t, v)
        y_ref[rows, :] = _group_norm_gate(o, g_ref[rows, :], gain).astype(BF16)

    @pl.when(c == pl.num_programs(2) - 1)
    def _():
        s_out_ref[0, 0] = s_scr[...]


def _ret_prompt(z, lg, cos, sin, gain):
    steps = SEQ // RET_ROWS
    qb, vb = RET_Q_W // RET_DK, (2 * RET_Q_W) // RET_DV
    row = lambda b, h, c: b * steps + c
    return pl.pallas_call(
        _ret_prompt_kernel,
        grid=(BATCH, RET_HEADS, steps),
        in_specs=[pl.BlockSpec(memory_space=pltpu.SMEM),
                  pl.BlockSpec((RET_ROWS, RET_DK), lambda b, h, c: (row(b, h, c), h)),
                  pl.BlockSpec((RET_ROWS, RET_DK), lambda b, h, c: (row(b, h, c), qb + h)),
                  pl.BlockSpec((RET_ROWS, RET_DV), lambda b, h, c: (row(b, h, c), vb + h)),
                  pl.BlockSpec((RET_ROWS, RET_DV), lambda b, h, c: (row(b, h, c), vb + RET_HEADS + h)),
                  pl.BlockSpec((RET_ROWS, RET_DK // 2), lambda b, h, c: (c, 0)),
                  pl.BlockSpec((RET_ROWS, RET_DK // 2), lambda b, h, c: (c, 0)),
                  pl.BlockSpec((1, RET_DV), lambda b, h, c: (0, h))],
        out_specs=[pl.BlockSpec((RET_ROWS, RET_DV), lambda b, h, c: (row(b, h, c), h)),
                   pl.BlockSpec((1, 1, RET_DK, RET_DV), lambda b, h, c: (b, h, 0, 0))],
        out_shape=[jax.ShapeDtypeStruct((M_PROMPT, RET_V_W), BF16),
                   jax.ShapeDtypeStruct((BATCH, RET_HEADS, RET_DK, RET_DV), F32)],
        scratch_shapes=[pltpu.VMEM((RET_DK, RET_DV), F32)],
        compiler_params=_cparams(("parallel", "parallel", "arbitrary"), 48),
    )(lg, z, z, z, z, cos, sin, gain.reshape(1, RET_V_W))


RET_SAMPLE_BATCHES = 2


def _ret_sample_kernel(lg_ref, zs_ref, cos_ref, sin_ref, gain_ref, s_ref, y_ref, s_out_ref,
                       q_pad, k_pad, kd_pad, v_pad):
    half = RET_DK // 2
    t = DEC_SEQ
    cos, sin = cos_ref[...], sin_ref[...]
    n_col = lax.broadcasted_iota(I32, (t, 1), 0).astype(F32)
    pad_rows = q_pad.shape[0]
    qi = lax.broadcasted_iota(I32, (pad_rows, LANES), 0)
    kj = lax.broadcasted_iota(I32, (pad_rows, LANES), 1)
    live = (qi >= kj) & (qi < t)
    diff = jnp.maximum(qi - kj, 0).astype(F32)

    def rotate(x):
        x1, x2 = x[:, :half], x[:, half:]
        return jnp.concatenate([x1 * cos - x2 * sin, x1 * sin + x2 * cos], axis=-1)

    q_pad[...] = jnp.zeros_like(q_pad)
    k_pad[...] = jnp.zeros_like(k_pad)
    kd_pad[...] = jnp.zeros_like(kd_pad)
    v_pad[...] = jnp.zeros_like(v_pad)

    for bb, h in [(bb, h) for bb in range(RET_SAMPLE_BATCHES) for h in range(RET_HEADS)]:
        lg = lg_ref[h]
        kr = rotate(zs_ref[bb, :, RET_Q_W + h * RET_DK:RET_Q_W + (h + 1) * RET_DK]) * (RET_DK ** -0.5)
        k_pad[0:t, :] = kr
        kd_pad[0:t, :] = kr * jnp.exp(lg * (t - 1.0 - n_col))
        q_pad[0:t, :] = rotate(zs_ref[bb, :, h * RET_DK:(h + 1) * RET_DK])
        v_pad[0:t, :] = zs_ref[bb, :, pl.ds(2 * RET_Q_W + h * RET_DV, RET_DV)]
        gate = zs_ref[bb, :, pl.ds(2 * RET_Q_W + RET_V_W + h * RET_DV, RET_DV)]
        s = s_ref[0, bb, h]

        qb = q_pad[...].astype(BF16)
        vb = v_pad[...].astype(BF16)
        scores = _dot_nt(qb, k_pad[...].astype(BF16)) * jnp.where(live, jnp.exp(lg * diff), 0.0)
        o = _dot(scores.astype(BF16), vb)[0:t] + _dot(qb, s.astype(BF16))[0:t] * jnp.exp(lg * (n_col + 1.0))
        s_out_ref[0, bb, h] = (jnp.exp(jnp.full((1, 1), lg * t, F32)) * s
                               + _dot(kd_pad[...].T.astype(BF16), vb))
        y_ref[bb, :, pl.ds(h * RET_DV, RET_DV)] = _group_norm_gate(
            o, gate, gain_ref[:, pl.ds(h * RET_DV, RET_DV)])


def _ret_sample(zs, lg, cos, sin, gain, state):
    half = RET_DK // 2
    nb = RET_SAMPLE_BATCHES
    whole = lambda *shape: pl.BlockSpec(shape, lambda b: (0,) * len(shape))
    state_spec = pl.BlockSpec((1, nb, RET_HEADS, RET_DK, RET_DV), lambda b: (0, b, 0, 0, 0))
    return pl.pallas_call(
        _ret_sample_kernel,
        grid=(DEC_BATCH // nb,),
        in_specs=[pl.BlockSpec(memory_space=pltpu.SMEM),
                  pl.BlockSpec((nb, DEC_SEQ, RET_IN_WIDTH), lambda b: (b, 0, 0)),
                  whole(DEC_SEQ, half), whole(DEC_SEQ, half),
                  whole(1, RET_V_W),
                  state_spec],
        out_specs=[pl.BlockSpec((nb, DEC_SEQ, RET_V_W), lambda b: (b, 0, 0)), state_spec],
        out_shape=[jax.ShapeDtypeStruct((DEC_BATCH, DEC_SEQ, RET_V_W), F32),
                   jax.ShapeDtypeStruct(state.shape, F32)],
        scratch_shapes=[pltpu.VMEM((2 * SUBLANES, RET_DK), F32),
                        pltpu.VMEM((LANES, RET_DK), F32),
                        pltpu.VMEM((LANES, RET_DK), F32),
                        pltpu.VMEM((LANES, RET_DV), F32)],
        compiler_params=_cparams(("parallel",), 48),
    )(lg, zs, cos, sin, gain.reshape(1, RET_V_W), state)


def _rel_bucket(dist):
    n = jnp.maximum(dist, 0)
    max_exact = REL_BUCKETS // 2
    nf = jnp.maximum(n, max_exact).astype(F32)
    large = max_exact + (jnp.log(nf / max_exact) / math.log(REL_MAX_DIST / max_exact)
                         * (REL_BUCKETS - max_exact)).astype(I32)
    large = jnp.minimum(large, REL_BUCKETS - 1)
    return jnp.where(n < max_exact, n, large)


def _bias_kernel(table_ref, near_ref, samp_ref):
    h = pl.program_id(0)

    def lookup(dist):
        bucket = _rel_bucket(dist)
        acc = jnp.zeros(dist.shape, F32)
        for b in range(REL_BUCKETS):
            acc = jnp.where(bucket == b, table_ref[b, h], acc)
        return acc

    far = table_ref[REL_BUCKETS - 1, h]
    i = lax.broadcasted_iota(I32, (QBLOCK, 2 * QBLOCK), 0)
    j = lax.broadcasted_iota(I32, (QBLOCK, 2 * QBLOCK), 1)
    near = (lookup(QBLOCK + i - j) - far) * LOG2_E
    near_ref[0, :, 0:2 * QBLOCK] = near
    near_ref[0, :, 2 * QBLOCK:] = near[:, 0:QBLOCK]
    t = lax.broadcasted_iota(I32, (SUBLANES, SAMPLE_KEYS), 0)
    s = lax.broadcasted_iota(I32, (SUBLANES, SAMPLE_KEYS), 1)
    samp_ref[0] = lookup(PAST_LEN + t - s)


def _bias_tables(rel_bias):
    return pl.pallas_call(
        _bias_kernel,
        grid=(ATT_HEADS,),
        in_specs=[pl.BlockSpec(memory_space=pltpu.SMEM)],
        out_specs=[pl.BlockSpec((1, QBLOCK, 3 * QBLOCK), lambda h: (h, 0, 0)),
                   pl.BlockSpec((1, SUBLANES, SAMPLE_KEYS), lambda h: (h, 0, 0))],
        out_shape=[jax.ShapeDtypeStruct((ATT_HEADS, QBLOCK, 3 * QBLOCK), F32),
                   jax.ShapeDtypeStruct((ATT_HEADS, SUBLANES, SAMPLE_KEYS), F32)],
        compiler_params=_cparams(("parallel",), 16),
    )(rel_bias)


def _order_key(score):
    bits = lax.bitcast_convert_type(score + 0.0, I32)
    return bits ^ ((bits >> 31) & 0x7FFFFFFF)


def _topk_neg_mask(score, kpos, causal, topk, key_ref, mask_ref, *, key_axis=1):
    keys = score.shape[key_axis]
    key_ref[...] = _order_key(score)
    kf = float(topk)

    def count(mask):
        ind = jnp.where(mask, 1.0, 0.0)
        slab = 8 * SUBLANES
        if key_axis == 0 and keys % slab == 0 and keys > slab:
            ind = jnp.sum(ind.reshape(keys // slab, slab, ind.shape[1]), axis=0)
        return jnp.sum(ind, axis=key_axis, keepdims=True)

    prefix = jnp.where(count(key_ref[...] >= 0) >= kf, 0, INT_MIN).astype(I32)

    def thr_body(i, prefix):
        cand = prefix | jnp.left_shift(1, 30 - i)
        return jnp.where(count(key_ref[...] >= cand) >= kf, cand, prefix)

    thr = lax.fori_loop(0, 31, thr_body, prefix)
    ge = key_ref[...] >= thr
    mask_ref[...] = jnp.where(ge & causal, 0.0, -jnp.inf)

    tied = (count(ge) > kf) & (thr > NEG_INF_KEY)

    @pl.when(jnp.max(jnp.where(tied, 1.0, 0.0)) > 0.0)
    def _():
        need = kf - count(key_ref[...] > thr)
        nbits = (keys - 1).bit_length()

        def tie_body(i, last):
            cand = last | jnp.left_shift(1, nbits - 1 - i)
            before = (key_ref[...] == thr) & (kpos < cand)
            return jnp.where(count(before) < need, cand, last)

        last = lax.fori_loop(0, nbits, tie_body, jnp.zeros_like(thr))
        key = key_ref[...]
        keep = ((key > thr) | ((key == thr) & (kpos <= last))) & causal
        mask_ref[...] = jnp.where(keep, 0.0, -jnp.inf)


PROMPT_SPLITS = 8
SPLIT_QBLOCKS = SEQ // QBLOCK // PROMPT_SPLITS


def _dsa_prompt_kernel(q_ref, iq_lo_ref, iq_hi_ref, iw_ref, k_ref, v_ref, ik_ref, near_ref, prev_ref, o_ref,
                       score_scr, mask_t_scr, key_scr, mask_scr, lg_scr, *, split):
    del prev_ref
    nkeys = (split + 1) * SPLIT_QBLOCKS * QBLOCK
    j = split * SPLIT_QBLOCKS + pl.program_id(1)
    topk = min(TOPK_MAX, SEQ // 4)
    d = ATT_HEAD_DIM

    iw_t = (iw_ref[...] * (IDX_HEADS ** -0.5 * IDX_DIM ** -0.5)).T
    iq = [ref[:, hh * IDX_DIM:(hh + 1) * IDX_DIM].astype(BF16)
          for ref in (iq_lo_ref, iq_hi_ref) for hh in range(IDX_HEADS // 2)]
    slab = 2 * QBLOCK
    for s in range(nkeys // slab):
        rows = slice(s * slab, (s + 1) * slab)
        ikb = ik_ref[0, rows, :].astype(BF16)
        acc = None
        for h in range(IDX_HEADS):
            term = iw_t[h:h + 1, :] * jnp.maximum(_dot_nt(ikb, iq[h]), 0.0)
            acc = term if acc is None else acc + term
        score_scr[rows, :] = acc
    qpos = j * QBLOCK + lax.broadcasted_iota(I32, (1, QBLOCK), 1)
    kpos = lax.broadcasted_iota(I32, (nkeys, 1), 0)
    causal = kpos <= qpos
    score = jnp.where(causal, score_scr[...], -jnp.inf)
    _topk_neg_mask(score, kpos, causal, topk, key_scr, mask_t_scr, key_axis=0)
    mask_scr[...] = mask_t_scr[...].T

    scale = d ** -0.5 * LOG2_E
    win_start = pl.multiple_of(jnp.maximum(j - 1, 0) * QBLOCK, QBLOCK)
    near_start = pl.multiple_of(jnp.where(j == 0, QBLOCK, 0), QBLOCK)

    lanes = lambda i: slice(i * d, (i + 1) * d)
    slot_rows = lambda h: slice((h % ATT_REP) * QBLOCK, (h % ATT_REP + 1) * QBLOCK)
    kv_cache = {}

    def group_kv(g):
        if g not in kv_cache:
            kv_cache[g] = (k_ref[0, :, lanes(g)].astype(BF16), v_ref[0, :, lanes(g)].astype(BF16))
        return kv_cache[g]

    def logits_and_max(h):
        rows = slot_rows(h)
        qh = (q_ref[:, lanes(h)] * scale).astype(BF16)
        lg_scr[rows, :] = _dot_nt(qh, group_kv(h // ATT_REP)[0]) + mask_scr[...]
        lg_scr[rows, pl.ds(win_start, 2 * QBLOCK)] += near_ref[h, :, pl.ds(near_start, 2 * QBLOCK)]
        return jnp.max(lg_scr[rows, :], axis=-1, keepdims=True)

    def weights(h, m):
        p = jnp.exp2(lg_scr[slot_rows(h), :] - m)
        return p.astype(BF16), jnp.sum(p, axis=-1, keepdims=True)

    def values(h, p, l):
        o_ref[:, lanes(h)] = (_dot(p, group_kv(h // ATT_REP)[1]) / l).astype(BF16)

    row_max = [None] * ATT_HEADS
    probs = [None] * ATT_HEADS
    for step in range(ATT_HEADS + 2):
        if step < ATT_HEADS:
            row_max[step] = logits_and_max(step)
        if 0 <= step - 1 < ATT_HEADS:
            probs[step - 1] = weights(step - 1, row_max[step - 1])
        if 0 <= step - 2 < ATT_HEADS:
            values(step - 2, *probs[step - 2])


def _dsa_prompt(z, near):
    nq = SEQ // QBLOCK
    z_keys = z.reshape(BATCH, SEQ, ATT_IN_PAD)
    out = jnp.zeros((M_PROMPT, ATT_Q_W), BF16)
    for split in range(PROMPT_SPLITS):
        nkeys = (split + 1) * SPLIT_QBLOCKS * QBLOCK
        qrow = lambda b, j, s=split: b * nq + s * SPLIT_QBLOCKS + j
        out = pl.pallas_call(
            functools.partial(_dsa_prompt_kernel, split=split),
            grid=(BATCH, SPLIT_QBLOCKS),
            in_specs=[pl.BlockSpec((QBLOCK, ATT_Q_W), lambda b, j: (qrow(b, j), ATT_COL_Q // ATT_Q_W)),
                      pl.BlockSpec((QBLOCK, IQ_HALF_W), lambda b, j: (qrow(b, j), ATT_COL_IQ // IQ_HALF_W)),
                      pl.BlockSpec((QBLOCK, IQ_HALF_W), lambda b, j: (qrow(b, j), ATT_COL_IQ // IQ_HALF_W + 1)),
                      pl.BlockSpec((QBLOCK, LANES), lambda b, j: (qrow(b, j), ATT_COL_IW // LANES)),
                      pl.BlockSpec((1, nkeys, ATT_KV_W), lambda b, j: (b, 0, ATT_COL_K // ATT_KV_W)),
                      pl.BlockSpec((1, nkeys, ATT_KV_W), lambda b, j: (b, 0, ATT_COL_V // ATT_KV_W)),
                      pl.BlockSpec((1, nkeys, IDX_DIM), lambda b, j: (b, 0, ATT_COL_IK // IDX_DIM)),
                      pl.BlockSpec((ATT_HEADS, QBLOCK, 3 * QBLOCK), lambda b, j: (0, 0, 0)),
                      pl.BlockSpec(memory_space=pl.ANY)],
            out_specs=pl.BlockSpec((QBLOCK, ATT_Q_W), lambda b, j: (qrow(b, j), 0)),
            out_shape=jax.ShapeDtypeStruct((M_PROMPT, ATT_Q_W), BF16),
            input_output_aliases={8: 0},
            scratch_shapes=[pltpu.VMEM((nkeys, QBLOCK), F32), pltpu.VMEM((nkeys, QBLOCK), F32),
                            pltpu.VMEM((nkeys, QBLOCK), I32),
                            pltpu.VMEM((QBLOCK, nkeys), F32),
                            pltpu.VMEM((ATT_REP * QBLOCK, nkeys), F32)],
            compiler_params=_cparams(("parallel", "arbitrary"), 56),
        )(z, z, z, z, z_keys, z_keys, z_keys, near, out)
    return out


def _kv_rows_kernel(k_ref, v_ref, ko_ref, vo_ref):
    tm = k_ref.shape[0]
    for g in range(ATT_KV_HEADS):
        rows = pl.ds(g, tm, stride=ATT_KV_HEADS)
        cols = slice(g * ATT_HEAD_DIM, (g + 1) * ATT_HEAD_DIM)
        ko_ref[rows, :] = k_ref[:, cols]
        vo_ref[rows, :] = v_ref[:, cols]


def _kv_rows(z, tm=512):
    m = z.shape[0]
    out = jax.ShapeDtypeStruct((m * ATT_KV_HEADS, ATT_HEAD_DIM), F32)
    return pl.pallas_call(
        _kv_rows_kernel,
        grid=(m // tm,),
        in_specs=[pl.BlockSpec((tm, ATT_KV_W), lambda i: (i, ATT_COL_K // ATT_KV_W)),
                  pl.BlockSpec((tm, ATT_KV_W), lambda i: (i, ATT_COL_V // ATT_KV_W))],
        out_specs=[pl.BlockSpec((tm * ATT_KV_HEADS, ATT_HEAD_DIM), lambda i: (i, 0))] * 2,
        out_shape=[out, out],
        compiler_params=_cparams(("parallel",), 16),
    )(z, z)


SELECT_BATCHES = 16


def _sample_select_kernel(pt_ref, iq_ref, iw_ref, iknew_ref, cik_ref, mask_ref,
                          ikbuf, iknew_scr, score_scr, key_scr, mask_scr, sem):
    step = pl.program_id(0)
    nsteps = pl.num_programs(0)
    slot = lax.rem(step, 2)
    t = DEC_SEQ
    topk = min(TOPK_MAX, (PAST_LEN + DEC_SEQ) // 4)

    def page_copies(step_, slot_):
        copies = []
        for bb in range(SELECT_BATCHES):
            for p in range(N_PAGES):
                page = pt_ref[(step_ * SELECT_BATCHES + bb) * N_PAGES + p]
                dst = pl.ds(p * PAGE_SIZE, PAGE_SIZE)
                copies.append(pltpu.make_async_copy(cik_ref.at[page], ikbuf.at[slot_, bb, dst], sem.at[slot_]))
        return copies

    @pl.when(step == 0)
    def _():
        iknew_scr[...] = jnp.zeros_like(iknew_scr)
        for cp in page_copies(0, 0):
            cp.start()

    @pl.when(step + 1 < nsteps)
    def _():
        for cp in page_copies(step + 1, 1 - slot):
            cp.start()

    for cp in page_copies(step, slot):
        cp.wait()

    for bb in range(SELECT_BATCHES):
        iknew_scr[bb, 0:t, :] = iknew_ref[bb]
        iq = iq_ref[bb].astype(BF16)
        sc = jnp.concatenate([_dot_nt(iq, ikbuf[slot, bb].astype(BF16)),
                              _dot_nt(iq, iknew_scr[bb].astype(BF16))], axis=-1)
        ws = (iw_ref[bb] * (IDX_HEADS ** -0.5 * IDX_DIM ** -0.5)) * jnp.maximum(sc, 0.0)
        for i in range(t):
            score_scr[bb * t + i:bb * t + i + 1, :] = jnp.sum(
                ws[i * IDX_HEADS:(i + 1) * IDX_HEADS], axis=0, keepdims=True)

    rows = SELECT_BATCHES * t
    assert t & (t - 1) == 0
    tq = lax.broadcasted_iota(I32, (rows, 1), 0) & (t - 1)
    kpos = lax.broadcasted_iota(I32, (1, SAMPLE_KEYS), 1)
    causal = kpos <= PAST_LEN + tq
    score = jnp.where(causal, score_scr[...], -jnp.inf)
    _topk_neg_mask(score, kpos, causal, topk, key_scr, mask_scr)
    for bb in range(SELECT_BATCHES):
        mask_ref[bb] = mask_scr[bb * t:(bb + 1) * t, :]


def _sample_select(iq, iw, iknew, page_table, cik):
    rows = SELECT_BATCHES * DEC_SEQ
    per_step = lambda *shape: pl.BlockSpec((SELECT_BATCHES,) + shape, lambda s, pt: (s,) + (0,) * len(shape))
    grid_spec = pltpu.PrefetchScalarGridSpec(
        num_scalar_prefetch=1,
        grid=(DEC_BATCH // SELECT_BATCHES,),
        in_specs=[per_step(DEC_SEQ * IDX_HEADS, IDX_DIM),
                  per_step(DEC_SEQ * IDX_HEADS, 1),
                  per_step(DEC_SEQ, IDX_DIM),
                  pl.BlockSpec(memory_space=pl.ANY)],
        out_specs=per_step(DEC_SEQ, SAMPLE_KEYS),
        scratch_shapes=[pltpu.VMEM((2, SELECT_BATCHES, PAST_LEN, IDX_DIM), F32),
                        pltpu.VMEM((SELECT_BATCHES, LANES, IDX_DIM), F32),
                        pltpu.VMEM((rows, SAMPLE_KEYS), F32),
                        pltpu.VMEM((rows, SAMPLE_KEYS), I32),
                        pltpu.VMEM((rows, SAMPLE_KEYS), F32),
                        pltpu.SemaphoreType.DMA((2,))])
    return pl.pallas_call(
        _sample_select_kernel,
        grid_spec=grid_spec,
        out_shape=jax.ShapeDtypeStruct((DEC_BATCH, DEC_SEQ, SAMPLE_KEYS), F32),
        compiler_params=_cparams(("arbitrary",), 48),
    )(page_table.reshape(-1), iq, iw, iknew, cik)


def _dsa_sample_kernel(pt_ref, q_ref, mask_ref, knew_ref, vnew_ref, bias_ref,
                       ck_ref, cv_ref, o_ref, kbuf, vbuf, knew_scr, vnew_scr, sem):
    b = pl.program_id(0)
    nb = pl.num_programs(0)
    slot = lax.rem(b, 2)
    t = DEC_SEQ
    d = ATT_HEAD_DIM
    rows16 = ATT_REP * t

    def page_copies(batch, slot_):
        copies = []
        for p in range(N_PAGES):
            page = pt_ref[batch * N_PAGES + p]
            kv_rows = PAGE_SIZE * ATT_KV_HEADS
            kv_src = pl.ds(pl.multiple_of(page * kv_rows, kv_rows), kv_rows)
            kv_dst = pl.ds(p * kv_rows, kv_rows)
            copies.append(pltpu.make_async_copy(ck_ref.at[kv_src], kbuf.at[slot_, kv_dst], sem.at[0, slot_]))
            copies.append(pltpu.make_async_copy(cv_ref.at[kv_src], vbuf.at[slot_, kv_dst], sem.at[1, slot_]))
        return copies

    def kv_head(buf, g):
        return buf[slot, pl.ds(g, PAST_LEN, stride=ATT_KV_HEADS), :]

    @pl.when(b == 0)
    def _():
        knew_scr[...] = jnp.zeros_like(knew_scr)
        vnew_scr[...] = jnp.zeros_like(vnew_scr)
        for cp in page_copies(0, 0):
            cp.start()

    @pl.when(b + 1 < nb)
    def _():
        for cp in page_copies(b + 1, 1 - slot):
            cp.start()

    knew_scr[0:t, :] = knew_ref[0]
    vnew_scr[0:t, :] = vnew_ref[0]

    assert t & (t - 1) == 0
    tq = lax.broadcasted_iota(I32, (rows16, 1), 0) & (t - 1)
    neg_mask = jnp.zeros((rows16, SAMPLE_KEYS), F32)
    for i in range(t):
        neg_mask = jnp.where(tq == i, mask_ref[0, i:i + 1, :], neg_mask)

    for cp in page_copies(b, slot):
        cp.wait()

    scale = d ** -0.5
    for g in range(ATT_KV_HEADS):
        cols = slice(g * d, (g + 1) * d)
        qg = q_ref[0, g].astype(BF16)
        logits = jnp.concatenate([_dot_nt(qg, kv_head(kbuf, g).astype(BF16)),
                                  _dot_nt(qg, knew_scr[:, cols].astype(BF16))], axis=-1)
        x = logits * scale + bias_ref[g] + neg_mask
        p = jnp.exp(x - jnp.max(x, axis=-1, keepdims=True))
        l = jnp.sum(p, axis=-1, keepdims=True)
        pb = p.astype(BF16)
        out = (_dot(pb[:, :PAST_LEN], kv_head(vbuf, g).astype(BF16))
               + _dot(pb[:, PAST_LEN:], vnew_scr[:, cols].astype(BF16))) / l
        for r in range(ATT_REP):
            h = g * ATT_REP + r
            o_ref[0, :, h * d:(h + 1) * d] = out[r * t:(r + 1) * t]


def _dsa_sample(q, mask, knew, vnew, bias, page_table, ck, cv):
    rows16 = ATT_REP * DEC_SEQ
    whole = lambda *shape: pl.BlockSpec(shape, lambda b, pt: (0,) * len(shape))
    per_b = lambda *shape: pl.BlockSpec((1,) + shape, lambda b, pt: (b,) + (0,) * len(shape))
    grid_spec = pltpu.PrefetchScalarGridSpec(
        num_scalar_prefetch=1,
        grid=(DEC_BATCH,),
        in_specs=[per_b(ATT_KV_HEADS, rows16, ATT_HEAD_DIM),
                  per_b(DEC_SEQ, SAMPLE_KEYS),
                  per_b(DEC_SEQ, ATT_KV_W),
                  per_b(DEC_SEQ, ATT_KV_W),
                  whole(ATT_KV_HEADS, rows16, SAMPLE_KEYS),
                  pl.BlockSpec(memory_space=pl.ANY),
                  pl.BlockSpec(memory_space=pl.ANY)],
        out_specs=per_b(DEC_SEQ, ATT_Q_W),
        scratch_shapes=[pltpu.VMEM((2, PAST_LEN * ATT_KV_HEADS, ATT_HEAD_DIM), F32),
                        pltpu.VMEM((2, PAST_LEN * ATT_KV_HEADS, ATT_HEAD_DIM), F32),
                        pltpu.VMEM((LANES, ATT_KV_W), F32),
                        pltpu.VMEM((LANES, ATT_KV_W), F32),
                        pltpu.SemaphoreType.DMA((2, 2))])
    return pl.pallas_call(
        _dsa_sample_kernel,
        grid_spec=grid_spec,
        out_shape=jax.ShapeDtypeStruct((DEC_BATCH, DEC_SEQ, ATT_Q_W), F32),
        compiler_params=_cparams(("arbitrary",), 40),
    )(page_table.reshape(-1), q, mask, knew, vnew, bias, ck, cv)


def _rope_tables(pos):
    half = RET_DK // 2
    freqs = ROPE_BASE ** (-jnp.arange(half, dtype=F32) / half)
    ang = pos.astype(F32)[:, None] * freqs[None, :]
    return jnp.cos(ang), jnp.sin(ang)


def _pad_att_in(w):
    assert w.shape[1] == ATT_IN_WIDTH
    return jnp.pad(w, ((0, 0), (0, ATT_IN_PAD - ATT_IN_WIDTH)))


def _row_tile(m):
    return min(m, 1024)


def kernel(x_prompt, x_sample, p_prompt, p_sample, state_ret, cache_k, cache_v, cache_idx_k, page_table,
           norm_mix, norm_ffn, norm_ple, norm_final, w_ret_in, ret_gn_gain, w_ret_out,
           w_att_in, w_att_out, rel_bias, w_ffn_in, w_ffn_out, w_ple_gate, w_ple_proj):
    bf = lambda w: w.astype(BF16)
    hp = x_prompt.reshape(M_PROMPT, D_MODEL)
    hs = x_sample.reshape(M_SAMPLE, D_MODEL)
    pp = p_prompt.reshape(DEPTH, M_PROMPT, PLE_DIM)
    ps = p_sample.reshape(DEPTH, M_SAMPLE, PLE_DIM)

    def mixer_in(xn, w, tn):
        return _matmul(xn, w, tm=_row_tile(xn.shape[0]), tn=tn)

    def mixer_out(y, w, h, tm):
        return _matmul(y, w, h, tm=min(tm, h.shape[0]), tn=1024)

    w_ffn_in_b, w_ffn_out_b = bf(w_ffn_in), bf(w_ffn_out)
    w_ple_gate_b, w_ple_proj_b = bf(w_ple_gate), bf(w_ple_proj)

    def tail(h, p, i):
        last = i == DEPTH - 1
        h = _ffn(h, norm_ffn[i], w_ffn_in_b, w_ffn_out_b, i)
        g_next = norm_final if last else norm_mix[i + 1]
        return _ple(h, norm_ple[i], w_ple_gate_b, p, i, w_ple_proj_b, g_next, last)

    lg = jnp.log1p(-jnp.exp2(-5.0 - jnp.arange(RET_HEADS, dtype=F32)))
    cos_p, sin_p = _rope_tables(jnp.arange(SEQ, dtype=I32))
    cos_s, sin_s = _rope_tables(PAST_LEN + jnp.arange(DEC_SEQ, dtype=I32))
    w_in, w_out = bf(w_ret_in[0]), bf(w_ret_out[0])
    zp = mixer_in(_rmsnorm(hp, norm_mix[0], BF16), w_in, 2048)
    y, ret_state_p = _ret_prompt(zp, lg, cos_p, sin_p, ret_gn_gain[0])
    zs = mixer_in(_rmsnorm(hs, norm_mix[0], BF16), w_in, 2048).reshape(DEC_BATCH, DEC_SEQ, RET_IN_WIDTH)
    y_s, ret_state_s = _ret_sample(zs, lg, cos_s, sin_s, ret_gn_gain[0], state_ret)
    hp, xn_p = tail(mixer_out(y, w_out, hp, 512), pp, 0)
    hs, xn_s = tail(mixer_out(bf(y_s.reshape(M_SAMPLE, RET_V_W)), w_out, hs, 512), ps, 0)

    near, samp = _bias_tables(rel_bias)
    w_in, w_out = bf(_pad_att_in(w_att_in[0])), bf(w_att_out[0])
    zp = mixer_in(xn_p, w_in, 1792)
    a = _dsa_prompt(zp, near)
    zs = mixer_in(xn_s, w_in, 1792).reshape(DEC_BATCH, DEC_SEQ, ATT_IN_PAD)
    q_s = zs[..., :ATT_Q_W].reshape(DEC_BATCH, DEC_SEQ, ATT_KV_HEADS, ATT_REP, ATT_HEAD_DIM)
    q_s = q_s.transpose(0, 2, 3, 1, 4).reshape(DEC_BATCH, ATT_KV_HEADS, ATT_REP * DEC_SEQ, ATT_HEAD_DIM)
    bias_s = samp[:, :DEC_SEQ].reshape(ATT_KV_HEADS, ATT_REP * DEC_SEQ, SAMPLE_KEYS)
    iq_s = zs[..., ATT_COL_IQ:ATT_COL_IQ + IDX_Q_W].reshape(DEC_BATCH, DEC_SEQ * IDX_HEADS, IDX_DIM)
    iw_s = zs[..., ATT_COL_IW:ATT_COL_IW + IDX_HEADS].reshape(DEC_BATCH, DEC_SEQ * IDX_HEADS, 1)
    k_s = zs[..., ATT_COL_K:ATT_COL_K + ATT_KV_W]
    v_s = zs[..., ATT_COL_V:ATT_COL_V + ATT_KV_W]
    ik_s = zs[..., ATT_COL_IK:ATT_COL_IK + IDX_DIM]
    n_phys = cache_k.shape[1]
    mask_s = _sample_select(iq_s, iw_s, ik_s, page_table, cache_idx_k[0])
    a_s = _dsa_sample(q_s, mask_s, k_s, v_s, bias_s, page_table,
                      cache_k[0].reshape(n_phys * PAGE_SIZE * ATT_KV_HEADS, ATT_HEAD_DIM),
                      cache_v[0].reshape(n_phys * PAGE_SIZE * ATT_KV_HEADS, ATT_HEAD_DIM))
    yp = tail(mixer_out(a, w_out, hp, 1024), pp, 1)
    ys = tail(mixer_out(bf(a_s.reshape(M_SAMPLE, ATT_Q_W)), w_out, hs, 1024), ps, 1)

    kv_p = lambda x: x.reshape(1, BATCH, SEQ, ATT_KV_HEADS, ATT_HEAD_DIM)
    kv_s = lambda x: x.reshape(1, DEC_BATCH, DEC_SEQ, ATT_KV_HEADS, ATT_HEAD_DIM)
    k_p, v_p = _kv_rows(zp)
    return (yp.reshape(BATCH, SEQ, D_MODEL),
            ys.reshape(DEC_BATCH, DEC_SEQ, D_MODEL),
            ret_state_p[None],
            ret_state_s,
            kv_p(k_p), kv_p(v_p),
            zp[:, ATT_COL_IK:ATT_COL_IK + IDX_DIM].reshape(1, BATCH, SEQ, IDX_DIM),
            kv_s(k_s), kv_s(v_s),
            ik_s.reshape(1, DEC_BATCH, DEC_SEQ, IDX_DIM))
```

```python
import functools
import math

import jax
import jax.numpy as jnp
from jax import lax
from jax.experimental import pallas as pl
from jax.experimental.pallas import tpu as pltpu

F32 = jnp.float32
BF16 = jnp.bfloat16
I32 = jnp.int32

D_MODEL = 2048
BATCH = 4
SEQ = 2048
DEPTH = 2
DEC_BATCH = 128
DEC_SEQ = 4
PAST_LEN = 2048
PAGE_SIZE = 128
N_PAGES = PAST_LEN // PAGE_SIZE

RET_HEADS = 8
RET_DK = D_MODEL // RET_HEADS
RET_DV = 2 * D_MODEL // RET_HEADS
RET_CHUNK = 128
ROPE_BASE = 10000.0
ATT_HEADS = 16
ATT_HEAD_DIM = D_MODEL // ATT_HEADS
ATT_KV_HEADS = 4
ATT_REP = ATT_HEADS // ATT_KV_HEADS
IDX_HEADS = 16
IDX_DIM = 128
TOPK_MAX = 256
REL_BUCKETS = 32
REL_MAX_DIST = 128
FFN_HIDDEN = 5632
PLE_DIM = 256
EPS = 1e-6

RET_Q_W = RET_HEADS * RET_DK
RET_V_W = RET_HEADS * RET_DV
RET_IN_WIDTH = 2 * RET_Q_W + 2 * RET_V_W
ATT_Q_W = ATT_HEADS * ATT_HEAD_DIM
ATT_KV_W = ATT_KV_HEADS * ATT_HEAD_DIM
IDX_Q_W = IDX_HEADS * IDX_DIM

M_PROMPT = BATCH * SEQ
M_SAMPLE = DEC_BATCH * DEC_SEQ

LANES = 128
SUBLANES = 8
V7X_VMEM_BYTES = 64 * 1024 * 1024
MIB = 1024 * 1024

ATT_COL_Q = 0
ATT_COL_K = ATT_Q_W
ATT_COL_V = ATT_COL_K + ATT_KV_W
ATT_COL_IQ = ATT_COL_V + ATT_KV_W
ATT_COL_IK = ATT_COL_IQ + IDX_Q_W
ATT_COL_IW = ATT_COL_IK + IDX_DIM
ATT_IN_WIDTH = ATT_COL_IW + IDX_HEADS
ATT_IN_PAD = ATT_COL_IW + LANES
IQ_HALF_W = IDX_Q_W // 2
assert ATT_COL_IQ % IQ_HALF_W == 0

QBLOCK = 128
SAMPLE_KEYS = PAST_LEN + LANES
LOG2_E = math.log2(math.e)
INT_MIN = -(2 ** 31)
NEG_INF_KEY = -2139095041


def _cparams(semantics, vmem_mib):
    assert vmem_mib * MIB < V7X_VMEM_BYTES
    return pltpu.CompilerParams(dimension_semantics=semantics,
                                vmem_limit_bytes=vmem_mib * MIB)


def _dot(a, b):
    return jnp.dot(a, b, preferred_element_type=F32)


def _dot_nt(a, b):
    return lax.dot_general(a, b, (((1,), (1,)), ((), ())), preferred_element_type=F32)


def _rms(x, g):
    return x * lax.rsqrt(jnp.mean(x * x, axis=-1, keepdims=True) + EPS) * g


def _rmsnorm_kernel(x_ref, g_ref, o_ref):
    o_ref[...] = _rms(x_ref[...], g_ref[...]).astype(o_ref.dtype)


def _rmsnorm(x, g, out_dtype, tm=512):
    m, d = x.shape
    return pl.pallas_call(
        _rmsnorm_kernel,
        grid=(m // tm,),
        in_specs=[pl.BlockSpec((tm, d), lambda i: (i, 0)),
                  pl.BlockSpec((1, d), lambda i: (0, 0))],
        out_specs=pl.BlockSpec((tm, d), lambda i: (i, 0)),
        out_shape=jax.ShapeDtypeStruct((m, d), out_dtype),
        compiler_params=_cparams(("parallel",), 32),
    )(x, g.reshape(1, d))


def _mm_kernel(x_ref, w_ref, o_ref):
    o_ref[...] = _dot(x_ref[...], w_ref[...]).astype(o_ref.dtype)


def _mm_res_kernel(x_ref, w_ref, r_ref, o_ref):
    o_ref[...] = r_ref[...] + _dot(x_ref[...], w_ref[...])


def _matmul(x, w, res=None, *, tm, tn, out_dtype=F32, vmem_mib=48):
    m, k = x.shape
    n = w.shape[1]
    assert m % tm == 0 and n % tn == 0
    in_specs = [pl.BlockSpec((tm, k), lambda j, i: (i, 0)),
                pl.BlockSpec((k, tn), lambda j, i: (0, j))]
    args = [x, w]
    kern = _mm_kernel
    if res is not None:
        in_specs.append(pl.BlockSpec((tm, tn), lambda j, i: (i, j)))
        args.append(res)
        kern = _mm_res_kernel
    return pl.pallas_call(
        kern,
        grid=(n // tn, m // tm),
        in_specs=in_specs,
        out_specs=pl.BlockSpec((tm, tn), lambda j, i: (i, j)),
        out_shape=jax.ShapeDtypeStruct((m, n), out_dtype),
        compiler_params=_cparams(("parallel", "parallel"), vmem_mib),
    )(*args)


def _ffn_kernel(x_ref, g_ref, w1_ref, w2_ref, wo_ref, o_ref, xn_ref):
    @pl.when(pl.program_id(1) == 0)
    def _():
        x = x_ref[...]
        xn_ref[...] = _rms(x, g_ref[...]).astype(BF16)
        o_ref[...] = x

    xn = xn_ref[...]
    a = _dot(xn, w1_ref[0])
    b = _dot(xn, w2_ref[0])
    hid = (a * jax.nn.sigmoid(a) * b).astype(BF16)
    o_ref[...] += _dot(hid, wo_ref[0])


def _ffn(h, g, w_in, w_out, layer, tm=512, th=512):
    m, d = h.shape
    nh = FFN_HIDDEN // th
    return pl.pallas_call(
        _ffn_kernel,
        grid=(m // tm, nh),
        in_specs=[pl.BlockSpec((tm, d), lambda i, j: (i, 0)),
                  pl.BlockSpec((1, d), lambda i, j: (0, 0)),
                  pl.BlockSpec((1, d, th), lambda i, j: (layer, 0, j)),
                  pl.BlockSpec((1, d, th), lambda i, j: (layer, 0, j + nh)),
                  pl.BlockSpec((1, th, d), lambda i, j: (layer, j, 0))],
        out_specs=pl.BlockSpec((tm, d), lambda i, j: (i, 0)),
        out_shape=jax.ShapeDtypeStruct((m, d), F32),
        scratch_shapes=[pltpu.VMEM((tm, d), BF16)],
        compiler_params=_cparams(("parallel", "arbitrary"), 48),
    )(h, g.reshape(1, d), w_in, w_in, w_out)


def _ple_kernel(x_ref, g_ref, wg_ref, p_ref, wp_ref, gn_ref, *o_refs, last):
    x = x_ref[...]
    xn = _rms(x, g_ref[...]).astype(BF16)
    gate = jax.nn.sigmoid(_dot(xn, wg_ref[0]))
    y = x + gate * _dot(p_ref[0].astype(BF16), wp_ref[0])
    if last:
        o_refs[0][...] = _rms(y, gn_ref[...])
    else:
        o_refs[0][...] = y
        o_refs[1][...] = _rms(y, gn_ref[...]).astype(BF16)


def _ple(h, g, w_gate, p, layer, w_proj, g_next, last, tm=512):
    m, d = h.shape
    pd = p.shape[2]
    row_spec = pl.BlockSpec((tm, d), lambda i: (i, 0))
    stream = jax.ShapeDtypeStruct((m, d), F32)
    return pl.pallas_call(
        functools.partial(_ple_kernel, last=last),
        grid=(m // tm,),
        in_specs=[row_spec,
                  pl.BlockSpec((1, d), lambda i: (0, 0)),
                  pl.BlockSpec((1, d, d), lambda i: (layer, 0, 0)),
                  pl.BlockSpec((1, tm, pd), lambda i: (layer, i, 0)),
                  pl.BlockSpec((1, pd, d), lambda i: (layer, 0, 0)),
                  pl.BlockSpec((1, d), lambda i: (0, 0))],
        out_specs=row_spec if last else [row_spec, row_spec],
        out_shape=stream if last else [stream, jax.ShapeDtypeStruct((m, d), BF16)],
        compiler_params=_cparams(("parallel",), 48),
    )(h, g.reshape(1, d), w_gate, p, w_proj, g_next.reshape(1, d))


def _group_norm_gate(o, gate, gain):
    mu = jnp.mean(o, axis=-1, keepdims=True)
    var = jnp.mean(jnp.square(o - mu), axis=-1, keepdims=True)
    on = (o - mu) * lax.rsqrt(var + EPS)
    return gate * jax.nn.sigmoid(gate) * (on * gain)


RET_CHUNKS_PER_STEP = 16
RET_ROWS = RET_CHUNKS_PER_STEP * RET_CHUNK


def _ret_prompt_kernel(lg_ref, q_ref, k_ref, v_ref, g_ref, cos_ref, sin_ref, gain_ref,
                       y_ref, s_out_ref, s_scr):
    c = pl.program_id(2)
    lg = lg_ref[pl.program_id(1)]
    half = RET_DK // 2

    @pl.when(c == 0)
    def _():
        s_scr[...] = jnp.zeros_like(s_scr)

    n_col = lax.broadcasted_iota(I32, (RET_CHUNK, 1), 0).astype(F32)
    n_row = lax.broadcasted_iota(I32, (1, RET_CHUNK), 1).astype(F32)
    diff = n_col - n_row
    decay = jnp.where(diff >= 0, jnp.exp(lg * jnp.maximum(diff, 0.0)), 0.0)
    q_dec = jnp.exp(lg * (n_col + 1.0))
    k_dec = jnp.exp(lg * (RET_CHUNK - 1.0 - n_col))
    s_dec = jnp.exp(jnp.full((1, 1), lg * RET_CHUNK, F32))
    gain = gain_ref[...]

    def rotate(x, cos, sin):
        x1, x2 = x[:, :half], x[:, half:]
        return jnp.concatenate([x1 * cos - x2 * sin, x1 * sin + x2 * cos], axis=-1)

    for i in range(RET_CHUNKS_PER_STEP):
        rows = pl.ds(i * RET_CHUNK, RET_CHUNK)
        cos, sin = cos_ref[rows, :], sin_ref[rows, :]
        qr = rotate(q_ref[rows, :], cos, sin)
        kr = rotate(k_ref[rows, :], cos, sin) * (RET_DK ** -0.5)
        v = v_ref[rows, :].astype(BF16)
        s = s_scr[...]
        qb = qr.astype(BF16)
        scores = _dot_nt(qb, kr.astype(BF16)) * decay
        o = _dot(scores.astype(BF16), v) + _dot(qb, s.astype(BF16)) * q_dec
        kd_t = (kr * k_dec).T.astype(BF16)
        s_scr[...] = s_dec * s + _dot(kd_t, v)
        y_ref[rows, :] = _group_norm_gate(o, g_ref[rows, :], gain).astype(BF16)

    @pl.when(c == pl.num_programs(2) - 1)
    def _():
        s_out_ref[0, 0] = s_scr[...]


def _ret_prompt(z, lg, cos, sin, gain):
    steps = SEQ // RET_ROWS
    qb, vb = RET_Q_W // RET_DK, (2 * RET_Q_W) // RET_DV
    row = lambda b, h, c: b * steps + c
    return pl.pallas_call(
        _ret_prompt_kernel,
        grid=(BATCH, RET_HEADS, steps),
        in_specs=[pl.BlockSpec(memory_space=pltpu.SMEM),
                  pl.BlockSpec((RET_ROWS, RET_DK), lambda b, h, c: (row(b, h, c), h)),
                  pl.BlockSpec((RET_ROWS, RET_DK), lambda b, h, c: (row(b, h, c), qb + h)),
                  pl.BlockSpec((RET_ROWS, RET_DV), lambda b, h, c: (row(b, h, c), vb + h)),
                  pl.BlockSpec((RET_ROWS, RET_DV), lambda b, h, c: (row(b, h, c), vb + RET_HEADS + h)),
                  pl.BlockSpec((RET_ROWS, RET_DK // 2), lambda b, h, c: (c, 0)),
                  pl.BlockSpec((RET_ROWS, RET_DK // 2), lambda b, h, c: (c, 0)),
                  pl.BlockSpec((1, RET_DV), lambda b, h, c: (0, h))],
        out_specs=[pl.BlockSpec((RET_ROWS, RET_DV), lambda b, h, c: (row(b, h, c), h)),
                   pl.BlockSpec((1, 1, RET_DK, RET_DV), lambda b, h, c: (b, h, 0, 0))],
        out_shape=[jax.ShapeDtypeStruct((M_PROMPT, RET_V_W), BF16),
                   jax.ShapeDtypeStruct((BATCH, RET_HEADS, RET_DK, RET_DV), F32)],
        scratch_shapes=[pltpu.VMEM((RET_DK, RET_DV), F32)],
        compiler_params=_cparams(("parallel", "parallel", "arbitrary"), 48),
    )(lg, z, z, z, z, cos, sin, gain.reshape(1, RET_V_W))


RET_SAMPLE_BATCHES = 2


def _ret_sample_kernel(lg_ref, zs_ref, cos_ref, sin_ref, gain_ref, s_ref, y_ref, s_out_ref,
                       q_pad, k_pad, kd_pad, v_pad):
    half = RET_DK // 2
    t = DEC_SEQ
    cos, sin = cos_ref[...], sin_ref[...]
    n_col = lax.broadcasted_iota(I32, (t, 1), 0).astype(F32)
    pad_rows = q_pad.shape[0]
    qi = lax.broadcasted_iota(I32, (pad_rows, LANES), 0)
    kj = lax.broadcasted_iota(I32, (pad_rows, LANES), 1)
    live = (qi >= kj) & (qi < t)
    diff = jnp.maximum(qi - kj, 0).astype(F32)

    def rotate(x):
        x1, x2 = x[:, :half], x[:, half:]
        return jnp.concatenate([x1 * cos - x2 * sin, x1 * sin + x2 * cos], axis=-1)

    q_pad[...] = jnp.zeros_like(q_pad)
    k_pad[...] = jnp.zeros_like(k_pad)
    kd_pad[...] = jnp.zeros_like(kd_pad)
    v_pad[...] = jnp.zeros_like(v_pad)

    for bb, h in [(bb, h) for bb in range(RET_SAMPLE_BATCHES) for h in range(RET_HEADS)]:
        lg = lg_ref[h]
        kr = rotate(zs_ref[bb, :, RET_Q_W + h * RET_DK:RET_Q_W + (h + 1) * RET_DK]) * (RET_DK ** -0.5)
        k_pad[0:t, :] = kr
        kd_pad[0:t, :] = kr * jnp.exp(lg * (t - 1.0 - n_col))
        q_pad[0:t, :] = rotate(zs_ref[bb, :, h * RET_DK:(h + 1) * RET_DK])
        v_pad[0:t, :] = zs_ref[bb, :, pl.ds(2 * RET_Q_W + h * RET_DV, RET_DV)]
        gate = zs_ref[bb, :, pl.ds(2 * RET_Q_W + RET_V_W + h * RET_DV, RET_DV)]
        s = s_ref[0, bb, h]

        qb = q_pad[...].astype(BF16)
        vb = v_pad[...].astype(BF16)
        scores = _dot_nt(qb, k_pad[...].astype(BF16)) * jnp.where(live, jnp.exp(lg * diff), 0.0)
        o = _dot(scores.astype(BF16), vb)[0:t] + _dot(qb, s.astype(BF16))[0:t] * jnp.exp(lg * (n_col + 1.0))
        s_out_ref[0, bb, h] = (jnp.exp(jnp.full((1, 1), lg * t, F32)) * s
                               + _dot(kd_pad[...].T.astype(BF16), vb))
        y_ref[bb, :, pl.ds(h * RET_DV, RET_DV)] = _group_norm_gate(
            o, gate, gain_ref[:, pl.ds(h * RET_DV, RET_DV)])


def _ret_sample(zs, lg, cos, sin, gain, state):
    half = RET_DK // 2
    nb = RET_SAMPLE_BATCHES
    whole = lambda *shape: pl.BlockSpec(shape, lambda b: (0,) * len(shape))
    state_spec = pl.BlockSpec((1, nb, RET_HEADS, RET_DK, RET_DV), lambda b: (0, b, 0, 0, 0))
    return pl.pallas_call(
        _ret_sample_kernel,
        grid=(DEC_BATCH // nb,),
        in_specs=[pl.BlockSpec(memory_space=pltpu.SMEM),
                  pl.BlockSpec((nb, DEC_SEQ, RET_IN_WIDTH), lambda b: (b, 0, 0)),
                  whole(DEC_SEQ, half), whole(DEC_SEQ, half),
                  whole(1, RET_V_W),
                  state_spec],
        out_specs=[pl.BlockSpec((nb, DEC_SEQ, RET_V_W), lambda b: (b, 0, 0)), state_spec],
        out_shape=[jax.ShapeDtypeStruct((DEC_BATCH, DEC_SEQ, RET_V_W), F32),
                   jax.ShapeDtypeStruct(state.shape, F32)],
        scratch_shapes=[pltpu.VMEM((2 * SUBLANES, RET_DK), F32),
                        pltpu.VMEM((LANES, RET_DK), F32),
                        pltpu.VMEM((LANES, RET_DK), F32),
                        pltpu.VMEM((LANES, RET_DV), F32)],
        compiler_params=_cparams(("parallel",), 48),
    )(lg, zs, cos, sin, gain.reshape(1, RET_V_W), state)


def _rel_bucket(dist):
    n = jnp.maximum(dist, 0)
    max_exact = REL_BUCKETS // 2
    nf = jnp.maximum(n, max_exact).astype(F32)
    large = max_exact + (jnp.log(nf / max_exact) / math.log(REL_MAX_DIST / max_exact)
                         * (REL_BUCKETS - max_exact)).astype(I32)
    large = jnp.minimum(large, REL_BUCKETS - 1)
    return jnp.where(n < max_exact, n, large)


def _bias_kernel(table_ref, near_ref, samp_ref):
    h = pl.program_id(0)

    def lookup(dist):
        bucket = _rel_bucket(dist)
        acc = jnp.zeros(dist.shape, F32)
        for b in range(REL_BUCKETS):
            acc = jnp.where(bucket == b, table_ref[b, h], acc)
        return acc

    far = table_ref[REL_BUCKETS - 1, h]
    i = lax.broadcasted_iota(I32, (QBLOCK, 2 * QBLOCK), 0)
    j = lax.broadcasted_iota(I32, (QBLOCK, 2 * QBLOCK), 1)
    near = (lookup(QBLOCK + i - j) - far) * LOG2_E
    near_ref[0, :, 0:2 * QBLOCK] = near
    near_ref[0, :, 2 * QBLOCK:] = near[:, 0:QBLOCK]
    t = lax.broadcasted_iota(I32, (SUBLANES, SAMPLE_KEYS), 0)
    s = lax.broadcasted_iota(I32, (SUBLANES, SAMPLE_KEYS), 1)
    samp_ref[0] = lookup(PAST_LEN + t - s)


def _bias_tables(rel_bias):
    return pl.pallas_call(
        _bias_kernel,
        grid=(ATT_HEADS,),
        in_specs=[pl.BlockSpec(memory_space=pltpu.SMEM)],
        out_specs=[pl.BlockSpec((1, QBLOCK, 3 * QBLOCK), lambda h: (h, 0, 0)),
                   pl.BlockSpec((1, SUBLANES, SAMPLE_KEYS), lambda h: (h, 0, 0))],
        out_shape=[jax.ShapeDtypeStruct((ATT_HEADS, QBLOCK, 3 * QBLOCK), F32),
                   jax.ShapeDtypeStruct((ATT_HEADS, SUBLANES, SAMPLE_KEYS), F32)],
        compiler_params=_cparams(("parallel",), 16),
    )(rel_bias)


def _order_key(score):
    bits = lax.bitcast_convert_type(score + 0.0, I32)
    return bits ^ ((bits >> 31) & 0x7FFFFFFF)


def _topk_neg_mask(score, kpos, causal, topk, key_ref, mask_ref, *, key_axis=1):
    keys = score.shape[key_axis]
    key_ref[...] = _order_key(score)
    kf = float(topk)

    def count(mask):
        ind = jnp.where(mask, 1.0, 0.0)
        slab = 8 * SUBLANES
        if key_axis == 0 and keys % slab == 0 and keys > slab:
            ind = jnp.sum(ind.reshape(keys // slab, slab, ind.shape[1]), axis=0)
        return jnp.sum(ind, axis=key_axis, keepdims=True)

    prefix = jnp.where(count(key_ref[...] >= 0) >= kf, 0, INT_MIN).astype(I32)

    def thr_body(i, prefix):
        cand = prefix | jnp.left_shift(1, 30 - i)
        return jnp.where(count(key_ref[...] >= cand) >= kf, cand, prefix)

    thr = lax.fori_loop(0, 31, thr_body, prefix)
    ge = key_ref[...] >= thr
    mask_ref[...] = jnp.where(ge & causal, 0.0, -jnp.inf)

    tied = (count(ge) > kf) & (thr > NEG_INF_KEY)

    @pl.when(jnp.max(jnp.where(tied, 1.0, 0.0)) > 0.0)
    def _():
        need = kf - count(key_ref[...] > thr)
        nbits = (keys - 1).bit_length()

        def tie_body(i, last):
            cand = last | jnp.left_shift(1, nbits - 1 - i)
            before = (key_ref[...] == thr) & (kpos < cand)
            return jnp.where(count(before) < need, cand, last)

        last = lax.fori_loop(0, nbits, tie_body, jnp.zeros_like(thr))
        key = key_ref[...]
        keep = ((key > thr) | ((key == thr) & (kpos <= last))) & causal
        mask_ref[...] = jnp.where(keep, 0.0, -jnp.inf)


PROMPT_SPLITS = 8
SPLIT_QBLOCKS = SEQ // QBLOCK // PROMPT_SPLITS


def _dsa_prompt_kernel(q_ref, iq_lo_ref, iq_hi_ref, iw_ref, k_ref, v_ref, ik_ref, near_ref, prev_ref, o_ref,
                       score_scr, mask_t_scr, key_scr, mask_scr, lg_scr, *, split):
    del prev_ref
    nkeys = (split + 1) * SPLIT_QBLOCKS * QBLOCK
    j = split * SPLIT_QBLOCKS + pl.program_id(1)
    topk = min(TOPK_MAX, SEQ // 4)
    d = ATT_HEAD_DIM

    iw_t = (iw_ref[...] * (IDX_HEADS ** -0.5 * IDX_DIM ** -0.5)).T
    iq = [ref[:, hh * IDX_DIM:(hh + 1) * IDX_DIM].astype(BF16)
          for ref in (iq_lo_ref, iq_hi_ref) for hh in range(IDX_HEADS // 2)]
    slab = 2 * QBLOCK
    for s in range(nkeys // slab):
        rows = slice(s * slab, (s + 1) * slab)
        ikb = ik_ref[0, rows, :].astype(BF16)
        acc = None
        for h in range(IDX_HEADS):
            term = iw_t[h:h + 1, :] * jnp.maximum(_dot_nt(ikb, iq[h]), 0.0)
            acc = term if acc is None else acc + term
        score_scr[rows, :] = acc
    qpos = j * QBLOCK + lax.broadcasted_iota(I32, (1, QBLOCK), 1)
    kpos = lax.broadcasted_iota(I32, (nkeys, 1), 0)
    causal = kpos <= qpos
    score = jnp.where(causal, score_scr[...], -jnp.inf)
    _topk_neg_mask(score, kpos, causal, topk, key_scr, mask_t_scr, key_axis=0)
    mask_scr[...] = mask_t_scr[...].T

    scale = d ** -0.5 * LOG2_E
    win_start = pl.multiple_of(jnp.maximum(j - 1, 0) * QBLOCK, QBLOCK)
    near_start = pl.multiple_of(jnp.where(j == 0, QBLOCK, 0), QBLOCK)

    lanes = lambda i: slice(i * d, (i + 1) * d)
    slot_rows = lambda h: slice((h % ATT_REP) * QBLOCK, (h % ATT_REP + 1) * QBLOCK)
    kv_cache = {}

    def group_kv(g):
        if g not in kv_cache:
            kv_cache[g] = (k_ref[0, :, lanes(g)].astype(BF16), v_ref[0, :, lanes(g)].astype(BF16))
        return kv_cache[g]

    def logits_and_max(h):
        rows = slot_rows(h)
        qh = (q_ref[:, lanes(h)] * scale).astype(BF16)
        lg_scr[rows, :] = _dot_nt(qh, group_kv(h // ATT_REP)[0]) + mask_scr[...]
        lg_scr[rows, pl.ds(win_start, 2 * QBLOCK)] += near_ref[h, :, pl.ds(near_start, 2 * QBLOCK)]
        return jnp.max(lg_scr[rows, :], axis=-1, keepdims=True)

    def weights(h, m):
        p = jnp.exp2(lg_scr[slot_rows(h), :] - m)
        return p.astype(BF16), jnp.sum(p, axis=-1, keepdims=True)

    def values(h, p, l):
        o_ref[:, lanes(h)] = (_dot(p, group_kv(h // ATT_REP)[1]) / l).astype(BF16)

    row_max = [None] * ATT_HEADS
    probs = [None] * ATT_HEADS
    for step in range(ATT_HEADS + 2):
        if step < ATT_HEADS:
            row_max[step] = logits_and_max(step)
        if 0 <= step - 1 < ATT_HEADS:
            probs[step - 1] = weights(step - 1, row_max[step - 1])
        if 0 <= step - 2 < ATT_HEADS:
            values(step - 2, *probs[step - 2])


def _dsa_prompt(z, near):
    nq = SEQ // QBLOCK
    z_keys = z.reshape(BATCH, SEQ, ATT_IN_PAD)
    out = jnp.zeros((M_PROMPT, ATT_Q_W), BF16)
    for split in range(PROMPT_SPLITS):
        nkeys = (split + 1) * SPLIT_QBLOCKS * QBLOCK
        qrow = lambda b, j, s=split: b * nq + s * SPLIT_QBLOCKS + j
        out = pl.pallas_call(
            functools.partial(_dsa_prompt_kernel, split=split),
            grid=(BATCH, SPLIT_QBLOCKS),
            in_specs=[pl.BlockSpec((QBLOCK, ATT_Q_W), lambda b, j: (qrow(b, j), ATT_COL_Q // ATT_Q_W)),
                      pl.BlockSpec((QBLOCK, IQ_HALF_W), lambda b, j: (qrow(b, j), ATT_COL_IQ // IQ_HALF_W)),
                      pl.BlockSpec((QBLOCK, IQ_HALF_W), lambda b, j: (qrow(b, j), ATT_COL_IQ // IQ_HALF_W + 1)),
                      pl.BlockSpec((QBLOCK, LANES), lambda b, j: (qrow(b, j), ATT_COL_IW // LANES)),
                      pl.BlockSpec((1, nkeys, ATT_KV_W), lambda b, j: (b, 0, ATT_COL_K // ATT_KV_W)),
                      pl.BlockSpec((1, nkeys, ATT_KV_W), lambda b, j: (b, 0, ATT_COL_V // ATT_KV_W)),
                      pl.BlockSpec((1, nkeys, IDX_DIM), lambda b, j: (b, 0, ATT_COL_IK // IDX_DIM)),
                      pl.BlockSpec((ATT_HEADS, QBLOCK, 3 * QBLOCK), lambda b, j: (0, 0, 0)),
                      pl.BlockSpec(memory_space=pl.ANY)],
            out_specs=pl.BlockSpec((QBLOCK, ATT_Q_W), lambda b, j: (qrow(b, j), 0)),
            out_shape=jax.ShapeDtypeStruct((M_PROMPT, ATT_Q_W), BF16),
            input_output_aliases={8: 0},
            scratch_shapes=[pltpu.VMEM((nkeys, QBLOCK), F32), pltpu.VMEM((nkeys, QBLOCK), F32),
                            pltpu.VMEM((nkeys, QBLOCK), I32),
                            pltpu.VMEM((QBLOCK, nkeys), F32),
                            pltpu.VMEM((ATT_REP * QBLOCK, nkeys), F32)],
            compiler_params=_cparams(("parallel", "arbitrary"), 56),
        )(z, z, z, z, z_keys, z_keys, z_keys, near, out)
    return out


def _kv_rows_kernel(k_ref, v_ref, ko_ref, vo_ref):
    tm = k_ref.shape[0]
    for g in range(ATT_KV_HEADS):
        rows = pl.ds(g, tm, stride=ATT_KV_HEADS)
        cols = slice(g * ATT_HEAD_DIM, (g + 1) * ATT_HEAD_DIM)
        ko_ref[rows, :] = k_ref[:, cols]
        vo_ref[rows, :] = v_ref[:, cols]


def _kv_rows(z, tm=512):
    m = z.shape[0]
    out = jax.ShapeDtypeStruct((m * ATT_KV_HEADS, ATT_HEAD_DIM), F32)
    return pl.pallas_call(
        _kv_rows_kernel,
        grid=(m // tm,),
        in_specs=[pl.BlockSpec((tm, ATT_KV_W), lambda i: (i, ATT_COL_K // ATT_KV_W)),
                  pl.BlockSpec((tm, ATT_KV_W), lambda i: (i, ATT_COL_V // ATT_KV_W))],
        out_specs=[pl.BlockSpec((tm * ATT_KV_HEADS, ATT_HEAD_DIM), lambda i: (i, 0))] * 2,
        out_shape=[out, out],
        compiler_params=_cparams(("parallel",), 16),
    )(z, z)


SELECT_BATCHES = 16


def _sample_select_kernel(pt_ref, iq_ref, iw_ref, iknew_ref, cik_ref, mask_ref,
                          ikbuf, iknew_scr, score_scr, key_scr, mask_scr, sem):
    step = pl.program_id(0)
    nsteps = pl.num_programs(0)
    slot = lax.rem(step, 2)
    t = DEC_SEQ
    topk = min(TOPK_MAX, (PAST_LEN + DEC_SEQ) // 4)

    def page_copies(step_, slot_):
        copies = []
        for bb in range(SELECT_BATCHES):
            for p in range(N_PAGES):
                page = pt_ref[(step_ * SELECT_BATCHES + bb) * N_PAGES + p]
                dst = pl.ds(p * PAGE_SIZE, PAGE_SIZE)
                copies.append(pltpu.make_async_copy(cik_ref.at[page], ikbuf.at[slot_, bb, dst], sem.at[slot_]))
        return copies

    @pl.when(step == 0)
    def _():
        iknew_scr[...] = jnp.zeros_like(iknew_scr)
        for cp in page_copies(0, 0):
            cp.start()

    @pl.when(step + 1 < nsteps)
    def _():
        for cp in page_copies(step + 1, 1 - slot):
            cp.start()

    for cp in page_copies(step, slot):
        cp.wait()

    for bb in range(SELECT_BATCHES):
        iknew_scr[bb, 0:t, :] = iknew_ref[bb]
        iq = iq_ref[bb].astype(BF16)
        sc = jnp.concatenate([_dot_nt(iq, ikbuf[slot, bb].astype(BF16)),
                              _dot_nt(iq, iknew_scr[bb].astype(BF16))], axis=-1)
        ws = (iw_ref[bb] * (IDX_HEADS ** -0.5 * IDX_DIM ** -0.5)) * jnp.maximum(sc, 0.0)
        for i in range(t):
            score_scr[bb * t + i:bb * t + i + 1, :] = jnp.sum(
                ws[i * IDX_HEADS:(i + 1) * IDX_HEADS], axis=0, keepdims=True)

    rows = SELECT_BATCHES * t
    assert t & (t - 1) == 0
    tq = lax.broadcasted_iota(I32, (rows, 1), 0) & (t - 1)
    kpos = lax.broadcasted_iota(I32, (1, SAMPLE_KEYS), 1)
    causal = kpos <= PAST_LEN + tq
    score = jnp.where(causal, score_scr[...], -jnp.inf)
    _topk_neg_mask(score, kpos, causal, topk, key_scr, mask_scr)
    for bb in range(SELECT_BATCHES):
        mask_ref[bb] = mask_scr[bb * t:(bb + 1) * t, :]


def _sample_select(iq, iw, iknew, page_table, cik):
    rows = SELECT_BATCHES * DEC_SEQ
    per_step = lambda *shape: pl.BlockSpec((SELECT_BATCHES,) + shape, lambda s, pt: (s,) + (0,) * len(shape))
    grid_spec = pltpu.PrefetchScalarGridSpec(
        num_scalar_prefetch=1,
        grid=(DEC_BATCH // SELECT_BATCHES,),
        in_specs=[per_step(DEC_SEQ * IDX_HEADS, IDX_DIM),
                  per_step(DEC_SEQ * IDX_HEADS, 1),
                  per_step(DEC_SEQ, IDX_DIM),
                  pl.BlockSpec(memory_space=pl.ANY)],
        out_specs=per_step(DEC_SEQ, SAMPLE_KEYS),
        scratch_shapes=[pltpu.VMEM((2, SELECT_BATCHES, PAST_LEN, IDX_DIM), F32),
                        pltpu.VMEM((SELECT_BATCHES, LANES, IDX_DIM), F32),
                        pltpu.VMEM((rows, SAMPLE_KEYS), F32),
                        pltpu.VMEM((rows, SAMPLE_KEYS), I32),
                        pltpu.VMEM((rows, SAMPLE_KEYS), F32),
                        pltpu.SemaphoreType.DMA((2,))])
    return pl.pallas_call(
        _sample_select_kernel,
        grid_spec=grid_spec,
        out_shape=jax.ShapeDtypeStruct((DEC_BATCH, DEC_SEQ, SAMPLE_KEYS), F32),
        compiler_params=_cparams(("arbitrary",), 48),
    )(page_table.reshape(-1), iq, iw, iknew, cik)


ATTEND_BATCHES = 2


def _dsa_sample_kernel(pt_ref, q_ref, mask_ref, knew_ref, vnew_ref, bias_ref,
                       ck_ref, cv_ref, o_ref, kbuf, vbuf, knew_scr, vnew_scr, sem):
    step = pl.program_id(0)
    nsteps = pl.num_programs(0)
    slot = lax.rem(step, 2)
    t = DEC_SEQ
    d = ATT_HEAD_DIM
    rows16 = ATT_REP * t

    def page_copies(step_, slot_):
        copies = []
        for bb in range(ATTEND_BATCHES):
            for p in range(N_PAGES):
                page = pt_ref[(step_ * ATTEND_BATCHES + bb) * N_PAGES + p]
                kv_rows = PAGE_SIZE * ATT_KV_HEADS
                kv_src = pl.ds(pl.multiple_of(page * kv_rows, kv_rows), kv_rows)
                kv_dst = pl.ds(p * kv_rows, kv_rows)
                copies.append(pltpu.make_async_copy(ck_ref.at[kv_src], kbuf.at[slot_, bb, kv_dst], sem.at[0, slot_]))
                copies.append(pltpu.make_async_copy(cv_ref.at[kv_src], vbuf.at[slot_, bb, kv_dst], sem.at[1, slot_]))
        return copies

    def kv_head(buf, bb, g):
        return buf[slot, bb, pl.ds(g, PAST_LEN, stride=ATT_KV_HEADS), :]

    @pl.when(step == 0)
    def _():
        knew_scr[...] = jnp.zeros_like(knew_scr)
        vnew_scr[...] = jnp.zeros_like(vnew_scr)
        for cp in page_copies(0, 0):
            cp.start()

    @pl.when(step + 1 < nsteps)
    def _():
        for cp in page_copies(step + 1, 1 - slot):
            cp.start()

    for cp in page_copies(step, slot):
        cp.wait()

    assert t & (t - 1) == 0
    tq = lax.broadcasted_iota(I32, (rows16, 1), 0) & (t - 1)
    scale = d ** -0.5
    for bb in range(ATTEND_BATCHES):
        knew_scr[0:t, :] = knew_ref[bb]
        vnew_scr[0:t, :] = vnew_ref[bb]
        neg_mask = jnp.zeros((rows16, SAMPLE_KEYS), F32)
        for i in range(t):
            neg_mask = jnp.where(tq == i, mask_ref[bb, i:i + 1, :], neg_mask)
        for g in range(ATT_KV_HEADS):
            cols = slice(g * d, (g + 1) * d)
            qg = q_ref[bb, g].astype(BF16)
            logits = jnp.concatenate([_dot_nt(qg, kv_head(kbuf, bb, g).astype(BF16)),
                                      _dot_nt(qg, knew_scr[:, cols].astype(BF16))], axis=-1)
            x = logits * scale + bias_ref[g] + neg_mask
            p = jnp.exp(x - jnp.max(x, axis=-1, keepdims=True))
            l = jnp.sum(p, axis=-1, keepdims=True)
            pb = p.astype(BF16)
            out = (_dot(pb[:, :PAST_LEN], kv_head(vbuf, bb, g).astype(BF16))
                   + _dot(pb[:, PAST_LEN:], vnew_scr[:, cols].astype(BF16))) / l
            for r in range(ATT_REP):
                h = g * ATT_REP + r
                o_ref[bb, :, h * d:(h + 1) * d] = out[r * t:(r + 1) * t]


def _dsa_sample(q, mask, knew, vnew, bias, page_table, ck, cv):
    rows16 = ATT_REP * DEC_SEQ
    whole = lambda *shape: pl.BlockSpec(shape, lambda b, pt: (0,) * len(shape))
    nb = ATTEND_BATCHES
    per_b = lambda *shape: pl.BlockSpec((nb,) + shape, lambda b, pt: (b,) + (0,) * len(shape))
    kv_buf = pltpu.VMEM((2, nb, PAST_LEN * ATT_KV_HEADS, ATT_HEAD_DIM), F32)
    grid_spec = pltpu.PrefetchScalarGridSpec(
        num_scalar_prefetch=1,
        grid=(DEC_BATCH // nb,),
        in_specs=[per_b(ATT_KV_HEADS, rows16, ATT_HEAD_DIM),
                  per_b(DEC_SEQ, SAMPLE_KEYS),
                  per_b(DEC_SEQ, ATT_KV_W),
                  per_b(DEC_SEQ, ATT_KV_W),
                  whole(ATT_KV_HEADS, rows16, SAMPLE_KEYS),
                  pl.BlockSpec(memory_space=pl.ANY),
                  pl.BlockSpec(memory_space=pl.ANY)],
        out_specs=per_b(DEC_SEQ, ATT_Q_W),
        scratch_shapes=[kv_buf, kv_buf,
                        pltpu.VMEM((LANES, ATT_KV_W), F32),
                        pltpu.VMEM((LANES, ATT_KV_W), F32),
                        pltpu.SemaphoreType.DMA((2, 2))])
    return pl.pallas_call(
        _dsa_sample_kernel,
        grid_spec=grid_spec,
        out_shape=jax.ShapeDtypeStruct((DEC_BATCH, DEC_SEQ, ATT_Q_W), F32),
        compiler_params=_cparams(("arbitrary",), 48),
    )(page_table.reshape(-1), q, mask, knew, vnew, bias, ck, cv)


def _rope_tables(pos):
    half = RET_DK // 2
    freqs = ROPE_BASE ** (-jnp.arange(half, dtype=F32) / half)
    ang = pos.astype(F32)[:, None] * freqs[None, :]
    return jnp.cos(ang), jnp.sin(ang)


def _pad_att_in(w):
    assert w.shape[1] == ATT_IN_WIDTH
    return jnp.pad(w, ((0, 0), (0, ATT_IN_PAD - ATT_IN_WIDTH)))


def _row_tile(m):
    return min(m, 1024)


def kernel(x_prompt, x_sample, p_prompt, p_sample, state_ret, cache_k, cache_v, cache_idx_k, page_table,
           norm_mix, norm_ffn, norm_ple, norm_final, w_ret_in, ret_gn_gain, w_ret_out,
           w_att_in, w_att_out, rel_bias, w_ffn_in, w_ffn_out, w_ple_gate, w_ple_proj):
    bf = lambda w: w.astype(BF16)
    hp = x_prompt.reshape(M_PROMPT, D_MODEL)
    hs = x_sample.reshape(M_SAMPLE, D_MODEL)
    pp = p_prompt.reshape(DEPTH, M_PROMPT, PLE_DIM)
    ps = p_sample.reshape(DEPTH, M_SAMPLE, PLE_DIM)

    def mixer_in(xn, w, tn):
        return _matmul(xn, w, tm=_row_tile(xn.shape[0]), tn=tn)

    def mixer_out(y, w, h, tm):
        return _matmul(y, w, h, tm=min(tm, h.shape[0]), tn=1024)

    w_ffn_in_b, w_ffn_out_b = bf(w_ffn_in), bf(w_ffn_out)
    w_ple_gate_b, w_ple_proj_b = bf(w_ple_gate), bf(w_ple_proj)

    def tail(h, p, i):
        last = i == DEPTH - 1
        h = _ffn(h, norm_ffn[i], w_ffn_in_b, w_ffn_out_b, i)
        g_next = norm_final if last else norm_mix[i + 1]
        return _ple(h, norm_ple[i], w_ple_gate_b, p, i, w_ple_proj_b, g_next, last)

    lg = jnp.log1p(-jnp.exp2(-5.0 - jnp.arange(RET_HEADS, dtype=F32)))
    cos_p, sin_p = _rope_tables(jnp.arange(SEQ, dtype=I32))
    cos_s, sin_s = _rope_tables(PAST_LEN + jnp.arange(DEC_SEQ, dtype=I32))
    w_in, w_out = bf(w_ret_in[0]), bf(w_ret_out[0])
    zp = mixer_in(_rmsnorm(hp, norm_mix[0], BF16), w_in, 2048)
    y, ret_state_p = _ret_prompt(zp, lg, cos_p, sin_p, ret_gn_gain[0])
    zs = mixer_in(_rmsnorm(hs, norm_mix[0], BF16), w_in, 2048).reshape(DEC_BATCH, DEC_SEQ, RET_IN_WIDTH)
    y_s, ret_state_s = _ret_sample(zs, lg, cos_s, sin_s, ret_gn_gain[0], state_ret)
    hp, xn_p = tail(mixer_out(y, w_out, hp, 512), pp, 0)
    hs, xn_s = tail(mixer_out(bf(y_s.reshape(M_SAMPLE, RET_V_W)), w_out, hs, 512), ps, 0)

    near, samp = _bias_tables(rel_bias)
    w_in, w_out = bf(_pad_att_in(w_att_in[0])), bf(w_att_out[0])
    zp = mixer_in(xn_p, w_in, 1792)
    a = _dsa_prompt(zp, near)
    zs = mixer_in(xn_s, w_in, 1792).reshape(DEC_BATCH, DEC_SEQ, ATT_IN_PAD)
    q_s = zs[..., :ATT_Q_W].reshape(DEC_BATCH, DEC_SEQ, ATT_KV_HEADS, ATT_REP, ATT_HEAD_DIM)
    q_s = q_s.transpose(0, 2, 3, 1, 4).reshape(DEC_BATCH, ATT_KV_HEADS, ATT_REP * DEC_SEQ, ATT_HEAD_DIM)
    bias_s = samp[:, :DEC_SEQ].reshape(ATT_KV_HEADS, ATT_REP * DEC_SEQ, SAMPLE_KEYS)
    iq_s = zs[..., ATT_COL_IQ:ATT_COL_IQ + IDX_Q_W].reshape(DEC_BATCH, DEC_SEQ * IDX_HEADS, IDX_DIM)
    iw_s = zs[..., ATT_COL_IW:ATT_COL_IW + IDX_HEADS].reshape(DEC_BATCH, DEC_SEQ * IDX_HEADS, 1)
    k_s = zs[..., ATT_COL_K:ATT_COL_K + ATT_KV_W]
    v_s = zs[..., ATT_COL_V:ATT_COL_V + ATT_KV_W]
    ik_s = zs[..., ATT_COL_IK:ATT_COL_IK + IDX_DIM]
    n_phys = cache_k.shape[1]
    mask_s = _sample_select(iq_s, iw_s, ik_s, page_table, cache_idx_k[0])
    a_s = _dsa_sample(q_s, mask_s, k_s, v_s, bias_s, page_table,
                      cache_k[0].reshape(n_phys * PAGE_SIZE * ATT_KV_HEADS, ATT_HEAD_DIM),
                      cache_v[0].reshape(n_phys * PAGE_SIZE * ATT_KV_HEADS, ATT_HEAD_DIM))
    yp = tail(mixer_out(a, w_out, hp, 1024), pp, 1)
    ys = tail(mixer_out(bf(a_s.reshape(M_SAMPLE, ATT_Q_W)), w_out, hs, 1024), ps, 1)

    kv_p = lambda x: x.reshape(1, BATCH, SEQ, ATT_KV_HEADS, ATT_HEAD_DIM)
    kv_s = lambda x: x.reshape(1, DEC_BATCH, DEC_SEQ, ATT_KV_HEADS, ATT_HEAD_DIM)
    k_p, v_p = _kv_rows(zp)
    return (yp.reshape(BATCH, SEQ, D_MODEL),
            ys.reshape(DEC_BATCH, DEC_SEQ, D_MODEL),
            ret_state_p[None],
            ret_state_s,
            kv_p(k_p), kv_p(v_p),
            zp[:, ATT_COL_IK:ATT_COL_IK + IDX_DIM].reshape(1, BATCH, SEQ, IDX_DIM),
            kv_s(k_s), kv_s(v_s),
            ik_s.reshape(1, DEC_BATCH, DEC_SEQ, IDX_DIM))
```
